```python
import math
import jax
import jax.numpy as jnp
from jax import lax
import numpy as np

D_MODEL = 1024
BATCH = 2
SEQ = 8192
DEPTH = 4

HEAD_DIM = 64
ATTN_SCALE = HEAD_DIM ** -0.5
ROPE_THETA = 10000.0
Q_BLOCK = 128
NEG_INF = -1e30
NORM_EPS = 1e-6

A_HEADS = D_MODEL // (4 * HEAD_DIM)
A_KV_HEADS = max(1, A_HEADS // 2)
A_WINDOW = 128

B_HEADS = D_MODEL // (4 * HEAD_DIM)
B_KV_HEADS = 1
CMP_STRIDE = 16
CMP_LEN = 2 * CMP_STRIDE
CMP_HIDDEN = 4 * HEAD_DIM
SLC_LEN = 64
SLC_TOPK = 16
NSA_WINDOW = 512
SLC_FORCED_SCORE = 1e9

C_HEAD_DIM = 128
C_HEADS = D_MODEL // (2 * C_HEAD_DIM)
CONV_WIDTH = 4
GDN_CHUNK = 64

D_FF_RAW = -(-8 * D_MODEL // 3)
D_FF = -(-D_FF_RAW // 256) * 256

A_Q = A_HEADS * HEAD_DIM
A_KV = A_KV_HEADS * HEAD_DIM
B_Q = B_HEADS * HEAD_DIM
B_KV = B_KV_HEADS * HEAD_DIM
B_GATES = 3 * B_HEADS
C_QK = C_HEADS * C_HEAD_DIM
MIX_WIDTH = A_Q + B_Q + C_QK
IN_SPLITS = (A_Q, A_KV, A_KV, B_Q, B_KV, B_KV, B_KV, B_KV, B_KV, B_KV, B_GATES, 3 * C_QK, C_QK, C_HEADS, C_HEADS)
IN_WIDTH = sum(IN_SPLITS)

kernel_name = 'hymba_swa_nsa_gdn_adaln_trunk'


def rms_norm(x, gain):
    xf = x.astype(jnp.float32)
    y = xf * lax.rsqrt(jnp.mean(xf * xf, axis=-1, keepdims=True) + NORM_EPS)
    return (y * gain.astype(jnp.float32)).astype(x.dtype)


def l2norm(x):
    return x * lax.rsqrt(jnp.sum(x * x, axis=-1, keepdims=True) + NORM_EPS)


def rope_tables(seq, dim):
    inv = 1.0 / (ROPE_THETA ** (jnp.arange(0, dim, 2, dtype=jnp.float32) / dim))
    ang = jnp.arange(seq, dtype=jnp.float32)[:, None] * inv[None, :]
    return jnp.cos(ang), jnp.sin(ang)


def apply_rope(x, cos, sin):
    x1, x2 = jnp.split(x.astype(jnp.float32), 2, axis=-1)
    c = cos[None, :, None, :]
    s = sin[None, :, None, :]
    return jnp.concatenate([x1 * c - x2 * s, x2 * c + x1 * s], axis=-1).astype(x.dtype)


def banded_attention(q, k, v, window, sink=None):
    B, S, H, d = q.shape
    Hkv = k.shape[2]
    G = H // Hkv
    nb = S // Q_BLOCK
    n_prev = window // Q_BLOCK
    span = (n_prev + 1) * Q_BLOCK
    pad = ((0, 0), (window, 0), (0, 0), (0, 0))
    kb = jnp.pad(k, pad).reshape(B, nb + n_prev, Q_BLOCK, Hkv, d)
    vb = jnp.pad(v, pad).reshape(B, nb + n_prev, Q_BLOCK, Hkv, d)
    k_ctx = jnp.concatenate([kb[:, j:j + nb] for j in range(n_prev + 1)], axis=2)
    v_ctx = jnp.concatenate([vb[:, j:j + nb] for j in range(n_prev + 1)], axis=2)
    qb = q.reshape(B, nb, Q_BLOCK, Hkv, G, d)
    s = jnp.einsum('bnqhgd,bnkhd->bnhgqk', qb, k_ctx, preferred_element_type=jnp.float32) * ATTN_SCALE
    q_rel = jnp.arange(Q_BLOCK)[:, None]
    k_rel = jnp.arange(span)[None, :] - window
    band = (k_rel <= q_rel) & (k_rel > q_rel - window)
    in_seq = (jnp.arange(nb)[:, None] * Q_BLOCK + k_rel) >= 0
    mask = band[None, :, :] & in_seq[:, None, :]
    s = jnp.where(mask[None, :, None, None], s, NEG_INF)
    if sink is None:
        p = jax.nn.softmax(s, axis=-1)
    else:
        sink_col = jnp.broadcast_to(sink.astype(jnp.float32).reshape(1, 1, Hkv, G, 1, 1), s.shape[:-1] + (1,))
        p = jax.nn.softmax(jnp.concatenate([s, sink_col], axis=-1), axis=-1)[..., :-1]
    o = jnp.einsum('bnhgqk,bnkhd->bnqhgd', p.astype(v.dtype), v_ctx)
    return o.reshape(B, S, H, d)


def compress_tokens(x, w1, w2, pe):
    B, S, Hkv, d = x.shape
    strides = x.reshape(B, S // CMP_STRIDE, CMP_STRIDE, Hkv, d)
    win = jnp.concatenate([strides[:, :-1], strides[:, 1:]], axis=2) + pe[:, None, :]
    flat = win.transpose(0, 1, 3, 2, 4).reshape(B, S // CMP_STRIDE - 1, Hkv, CMP_LEN * d)
    return jax.nn.silu(flat @ w1) @ w2


def nsa_mixer(q, k_cmp_in, v_cmp_in, k_slc, v_slc, k_win, v_win, gate_logits, cos, sin,
              cmp_k_w1, cmp_k_w2, cmp_v_w1, cmp_v_w2, cmp_pe_k, cmp_pe_v):
    B, S, H, d = q.shape
    Hkv = k_slc.shape[2]
    G = H // Hkv
    n_cmp = S // CMP_STRIDE - 1
    n_slc = S // SLC_LEN
    nb = S // Q_BLOCK
    ratio = SLC_LEN // CMP_STRIDE
    top_k = min(SLC_TOPK, n_slc)

    k_cmp = compress_tokens(k_cmp_in, cmp_k_w1, cmp_k_w2, cmp_pe_k)
    v_cmp = compress_tokens(v_cmp_in, cmp_v_w1, cmp_v_w2, cmp_pe_v)
    q_rot = apply_rope(q, cos, sin)
    ks_blocks = apply_rope(k_slc, cos, sin).reshape(B, n_slc, SLC_LEN, Hkv, d).transpose(0, 3, 1, 2, 4)
    vs_blocks = v_slc.reshape(B, n_slc, SLC_LEN, Hkv, d).transpose(0, 3, 1, 2, 4)
    cmp_end = jnp.arange(n_cmp) * CMP_STRIDE + (CMP_LEN - 1)
    blk_ids = jnp.arange(n_slc)
    gather = jax.vmap(jax.vmap(lambda blocks, ids: blocks[ids]))

    def block_fn(args):
        i, qb, qrb = args
        t = i * Q_BLOCK + jnp.arange(Q_BLOCK)
        s = jnp.einsum('bqhgd,bnhd->bhgqn', qb, k_cmp, preferred_element_type=jnp.float32) * ATTN_SCALE
        valid = cmp_end[None, :] <= t[:, None]
        p_cmp = jnp.where(valid, jax.nn.softmax(jnp.where(valid, s, NEG_INF), axis=-1), 0.0)
        o_cmp = jnp.einsum('bhgqn,bnhd->bqhgd', p_cmp.astype(v_cmp.dtype), v_cmp)
        imp = jnp.pad(p_cmp.sum(axis=2), ((0, 0), (0, 0), (0, 0), (0, ratio * n_slc - n_cmp)))
        imp = imp.reshape(B, Hkv, Q_BLOCK, n_slc, ratio)
        imp = imp.sum(-1) + jnp.pad(imp[..., :-1, -1], ((0, 0), (0, 0), (0, 0), (1, 0)))
        cur = (t // SLC_LEN)[:, None]
        forced = (blk_ids == 0) | (blk_ids == cur) | (blk_ids == cur - 1)
        causal = blk_ids * SLC_LEN <= t[:, None]
        score = jnp.where(forced, SLC_FORCED_SCORE, jnp.where(causal, imp, NEG_INF))
        _, idx = lax.top_k(score, top_k)
        kg = gather(ks_blocks, idx).reshape(B, Hkv, Q_BLOCK, top_k * SLC_LEN, d)
        vg = gather(vs_blocks, idx).reshape(B, Hkv, Q_BLOCK, top_k * SLC_LEN, d)
        kpos = (idx[..., None] * SLC_LEN + jnp.arange(SLC_LEN)).reshape(B, Hkv, Q_BLOCK, top_k * SLC_LEN)
        sel_ok = kpos <= t[None, None, :, None]
        s = jnp.einsum('bqhgd,bhqkd->bhgqk', qrb, kg, preferred_element_type=jnp.float32) * ATTN_SCALE
        p = jax.nn.softmax(jnp.where(sel_ok[:, :, None], s, NEG_INF), axis=-1)
        o_slc = jnp.einsum('bhgqk,bhqkd->bqhgd', p.astype(vg.dtype), vg)
        return o_cmp, o_slc

    def to_blocks(a):
        return a.reshape(B, nb, Q_BLOCK, Hkv, G, d).transpose(1, 0, 2, 3, 4, 5)

    def from_blocks(a):
        return a.transpose(1, 0, 2, 3, 4, 5).reshape(B, S, H, d)

    o_cmp, o_slc = lax.map(block_fn, (jnp.arange(nb), to_blocks(q), to_blocks(q_rot)))
    o_win = banded_attention(q_rot, apply_rope(k_win, cos, sin), v_win, NSA_WINDOW)
    gates = jax.nn.sigmoid(gate_logits.astype(jnp.float32)).reshape(B, S, 3, H, 1).astype(q.dtype)
    o = gates[:, :, 0] * from_blocks(o_cmp) + gates[:, :, 1] * from_blocks(o_slc) + gates[:, :, 2] * o_win
    return o.reshape(B, S, H * d)


def short_causal_conv(x, w):
    y = lax.conv_general_dilated(x, w[:, None, :].astype(x.dtype), window_strides=(1,),
                                 padding=[(CONV_WIDTH - 1, 0)], dimension_numbers=('NWC', 'WIO', 'NWC'),
                                 feature_group_count=x.shape[-1])
    return jax.nn.silu(y)


def gated_delta_net(qkv, a, b, z, conv_w, A_log, dt_bias, norm_w):
    B, S, _ = qkv.shape
    H, dk = C_HEADS, C_HEAD_DIM
    C = GDN_CHUNK
    nc = S // C
    f32 = jnp.float32
    q, k, v = jnp.split(short_causal_conv(qkv, conv_w).astype(f32), 3, axis=-1)
    q = l2norm(q.reshape(B, S, H, dk)) * (dk ** -0.5)
    k = l2norm(k.reshape(B, S, H, dk))
    v = v.reshape(B, S, H, dk)
    beta = jax.nn.sigmoid(b.astype(f32))
    g = -jnp.exp(A_log.astype(f32)) * jax.nn.softplus(a.astype(f32) + dt_bias.astype(f32))

    def chunk(t):
        return t.reshape(B, nc, C, H, dk).transpose(0, 3, 1, 2, 4)

    q, k, v = chunk(q), chunk(k), chunk(v)
    beta = beta.reshape(B, nc, C, H).transpose(0, 3, 1, 2)
    gc = jnp.cumsum(g.reshape(B, nc, C, H).transpose(0, 3, 1, 2), axis=-1)
    lower = jnp.tril(jnp.ones((C, C), dtype=bool))
    strict = jnp.tril(jnp.ones((C, C), dtype=bool), -1)
    decay = jnp.where(lower, jnp.exp(jnp.where(lower, gc[..., :, None] - gc[..., None, :], 0.0)), 0.0)
    k_beta = k * beta[..., None]
    a_mat = jnp.where(strict, jnp.einsum('bhncd,bhnsd->bhncs', k_beta, k) * decay, 0.0)
    rhs = jnp.concatenate([v * beta[..., None], k_beta * jnp.exp(gc)[..., None]], axis=-1)
    sol = lax.linalg.triangular_solve(a_mat + jnp.eye(C, dtype=f32), rhs, left_side=True, lower=True,
                                      unit_diagonal=True)
    u, w = sol[..., :dk], sol[..., dk:]
    qk = jnp.where(lower, jnp.einsum('bhncd,bhnsd->bhncs', q, k) * decay, 0.0)
    q_dec = q * jnp.exp(gc)[..., None]
    k_dec = k * jnp.exp(gc[..., -1:] - gc)[..., None]
    g_last = jnp.exp(gc[..., -1])

    def step(state, inp):
        u_c, w_c, qk_c, qd_c, kd_c, gl_c = inp
        v_new = u_c - jnp.einsum('bhcd,bhde->bhce', w_c, state)
        o_c = jnp.einsum('bhcd,bhde->bhce', qd_c, state) + jnp.einsum('bhcs,bhse->bhce', qk_c, v_new)
        state = state * gl_c[..., None, None] + jnp.einsum('bhcd,bhce->bhde', kd_c, v_new)
        return state, o_c

    xs = tuple(jnp.moveaxis(t, 2, 0) for t in (u, w, qk, q_dec, k_dec, g_last))
    _, o = lax.scan(step, jnp.zeros((B, H, dk, dk), f32), xs)
    o = o.transpose(1, 0, 3, 2, 4).reshape(B, S, H, dk)
    o = rms_norm(o, norm_w) * jax.nn.silu(z.reshape(B, S, H, dk).astype(f32))
    return o.reshape(B, S, H * dk).astype(z.dtype)


def hybrid_layer(x, c, cos, sin, norm_mix, norm_ffn, ada_w, ada_b, w_in, attn_sinks,
                 cmp_k_w1, cmp_k_w2, cmp_v_w1, cmp_v_w2, cmp_pe_k, cmp_pe_v,
                 conv_w, A_log, dt_bias, gdn_norm, w_out, w_gate_up, w_down):
    B, S, _ = x.shape
    mod = jax.nn.silu(c) @ ada_w + ada_b
    sh_m, sc_m, gt_m, sh_f, sc_f, gt_f = jnp.split(mod[:, None, :], 6, axis=-1)

    h = rms_norm(x, norm_mix) * (1 + sc_m) + sh_m
    points = np.cumsum(IN_SPLITS)[:-1].tolist()
    (aq, ak, av, bq, bkc, bvc, bks, bvs, bkw, bvw, bg, cqkv, cz, ca, cb) = jnp.split(h @ w_in, points, axis=-1)

    def heads(t, n):
        return t.reshape(B, S, n, -1)

    o_a = banded_attention(apply_rope(heads(aq, A_HEADS), cos, sin), apply_rope(heads(ak, A_KV_HEADS), cos, sin),
                           heads(av, A_KV_HEADS), A_WINDOW, attn_sinks).reshape(B, S, A_Q)
    o_b = nsa_mixer(heads(bq, B_HEADS), heads(bkc, B_KV_HEADS), heads(bvc, B_KV_HEADS),
                    heads(bks, B_KV_HEADS), heads(bvs, B_KV_HEADS), heads(bkw, B_KV_HEADS), heads(bvw, B_KV_HEADS),
                    bg, cos, sin, cmp_k_w1, cmp_k_w2, cmp_v_w1, cmp_v_w2, cmp_pe_k, cmp_pe_v)
    o_c = gated_delta_net(cqkv, ca, cb, cz, conv_w, A_log, dt_bias, gdn_norm)
    x = x + gt_m * (jnp.concatenate([o_a, o_b, o_c], axis=-1) @ w_out)

    h = rms_norm(x, norm_ffn) * (1 + sc_f) + sh_f
    gate, up = jnp.split(h @ w_gate_up, 2, axis=-1)
    return x + gt_f * ((jax.nn.silu(gate) * up) @ w_down)


def setup_inputs(seed: int = 0) -> dict:
    key = jax.random.key(seed)
    ks = jax.random.split(key, 24)
    f32 = jnp.float32
    L, D = DEPTH, D_MODEL
    cmp_in = CMP_LEN * HEAD_DIM

    def nrm(k, shape, scale):
        return jax.random.normal(k, shape, f32) * scale

    dt = jnp.exp(jax.random.uniform(ks[16], (L, C_HEADS), f32, math.log(1e-3), math.log(1e-1)))
    return {
        'x': nrm(ks[0], (BATCH, SEQ, D), 1.0),
        'c': nrm(ks[1], (BATCH, D), 1.0),
        'norm_mix': 1.0 + nrm(ks[2], (L, D), 0.1),
        'norm_ffn': 1.0 + nrm(ks[3], (L, D), 0.1),
        'ada_w': nrm(ks[4], (L, D, 6 * D), 0.5 * D ** -0.5),
        'ada_b': nrm(ks[5], (L, 6 * D), 0.01),
        'w_in': nrm(ks[6], (L, D, IN_WIDTH), D ** -0.5),
        'attn_sinks': nrm(ks[7], (L, A_HEADS), 1.0),
        'cmp_k_w1': nrm(ks[8], (L, cmp_in, CMP_HIDDEN), cmp_in ** -0.5),
        'cmp_k_w2': nrm(ks[9], (L, CMP_HIDDEN, HEAD_DIM), CMP_HIDDEN ** -0.5),
        'cmp_v_w1': nrm(ks[10], (L, cmp_in, CMP_HIDDEN), cmp_in ** -0.5),
        'cmp_v_w2': nrm(ks[11], (L, CMP_HIDDEN, HEAD_DIM), CMP_HIDDEN ** -0.5),
        'cmp_pe_k': nrm(ks[12], (L, CMP_LEN, HEAD_DIM), 0.1),
        'cmp_pe_v': nrm(ks[13], (L, CMP_LEN, HEAD_DIM), 0.1),
        'gdn_conv_w': nrm(ks[14], (L, CONV_WIDTH, 3 * C_QK), CONV_WIDTH ** -0.5),
        'gdn_A_log': jnp.log(jax.random.uniform(ks[15], (L, C_HEADS), f32, 1.0, 16.0)),
        'gdn_dt_bias': dt + jnp.log(-jnp.expm1(-dt)),
        'gdn_norm': 1.0 + nrm(ks[17], (L, C_HEAD_DIM), 0.1),
        'w_out': nrm(ks[18], (L, MIX_WIDTH, D), MIX_WIDTH ** -0.5),
        'w_gate_up': nrm(ks[19], (L, D, 2 * D_FF), D ** -0.5),
        'w_down': nrm(ks[20], (L, D_FF, D), D_FF ** -0.5),
        'final_norm': 1.0 + nrm(ks[21], (D,), 0.1),
    }


def reference(x, c, norm_mix, norm_ffn, ada_w, ada_b, w_in, attn_sinks, cmp_k_w1, cmp_k_w2, cmp_v_w1,
              cmp_v_w2, cmp_pe_k, cmp_pe_v, gdn_conv_w, gdn_A_log, gdn_dt_bias, gdn_norm, w_out,
              w_gate_up, w_down, final_norm):
    cos, sin = rope_tables(x.shape[1], HEAD_DIM)
    for l in range(DEPTH):
        x = hybrid_layer(x, c, cos, sin, norm_mix[l], norm_ffn[l], ada_w[l], ada_b[l], w_in[l], attn_sinks[l],
                         cmp_k_w1[l], cmp_k_w2[l], cmp_v_w1[l], cmp_v_w2[l], cmp_pe_k[l], cmp_pe_v[l],
                         gdn_conv_w[l], gdn_A_log[l], gdn_dt_bias[l], gdn_norm[l], w_out[l],
                         w_gate_up[l], w_down[l])
    return rms_norm(x, final_norm)
```

```python
import functools
import math

import numpy as np
import jax
import jax.numpy as jnp
from jax import lax
from jax.experimental import pallas as pl
from jax.experimental.pallas import tpu as pltpu

F32 = jnp.float32
BF16 = jnp.bfloat16
HIGHEST = lax.Precision.HIGHEST

D_MODEL = 1024
DEPTH = 4
HEAD_DIM = 64
ATTN_SCALE = HEAD_DIM ** -0.5
ROPE_THETA = 10000.0
NEG_INF = -1e30
NORM_EPS = 1e-6

A_HEADS = 4
A_KV_HEADS = 2
A_WINDOW = 128

B_HEADS = 4
CMP_STRIDE = 16
CMP_LEN = 32
CMP_HIDDEN = 256
SLC_LEN = 64
SLC_TOPK = 16
NSA_WINDOW = 512
SLC_FORCED_SCORE = 1e9

C_HEAD_DIM = 128
C_HEADS = 4
CONV_WIDTH = 4
GDN_CHUNK = 64
C_QK = C_HEADS * C_HEAD_DIM

D_FF = 2816

A_Q = A_HEADS * HEAD_DIM
A_KV = A_KV_HEADS * HEAD_DIM
B_Q = B_HEADS * HEAD_DIM
B_KV = HEAD_DIM
B_GATES = 3 * B_HEADS
IN_SPLITS = (A_Q, A_KV, A_KV, B_Q, B_KV, B_KV, B_KV, B_KV, B_KV, B_KV, B_GATES, 3 * C_QK, C_QK, C_HEADS, C_HEADS)

W_ROPE = 768
W_V = 256
W_CMP = 128
W_SMALL = 128
W_GDN = 2048
W_TOTAL = W_ROPE + W_V + W_CMP + W_SMALL + W_GDN
SMALL_A = B_GATES
SMALL_B = B_GATES + C_HEADS

VMEM_LIMIT = 56 * 1024 * 1024

NT_DIMS = (((1,), (1,)), ((), ()))


def _cparams(*sem):
    return pltpu.CompilerParams(dimension_semantics=sem, vmem_limit_bytes=VMEM_LIMIT)


def _iota(shape, dim):
    return lax.broadcasted_iota(jnp.int32, shape, dim)


def _dot(a, b, precision=None):
    return jnp.dot(a, b, preferred_element_type=F32, precision=precision)


def _dot_nt(a, b):
    return lax.dot_general(a, b, NT_DIMS, preferred_element_type=F32)


def _silu(x):
    return x * jax.nn.sigmoid(x)


def _softplus(x):
    return jnp.maximum(x, 0.0) + jnp.log(1.0 + jnp.exp(-jnp.abs(x)))


def _rms_mod(x, gain, sc, sh):
    y = x * lax.rsqrt(jnp.mean(x * x, axis=-1, keepdims=True) + NORM_EPS)
    return (y * gain) * (1.0 + sc) + sh


def _mod_kernel(c_ref, w_ref, b_ref, o_ref):
    o_ref[...] = _dot(_silu(c_ref[...]), w_ref[...], HIGHEST) + b_ref[...]


def _adaln_mod(c, ada_w, ada_b):
    L, D, N = ada_w.shape
    tn = 1536
    c8 = jnp.zeros((8, D), F32).at[: c.shape[0]].set(c)
    out = pl.pallas_call(
        _mod_kernel,
        grid=(L, N // tn),
        in_specs=[
            pl.BlockSpec((8, D), lambda l, j: (0, 0)),
            pl.BlockSpec((None, D, tn), lambda l, j: (l, 0, j)),
            pl.BlockSpec((None, 1, tn), lambda l, j: (l, 0, j)),
        ],
        out_specs=pl.BlockSpec((None, 8, tn), lambda l, j: (l, 0, j)),
        out_shape=jax.ShapeDtypeStruct((L, 8, N), F32),
        compiler_params=_cparams("arbitrary", "arbitrary"),
        name="adaln_mod",
    )(c8, ada_w, ada_b.reshape(L, 1, N))
    return out.reshape(L, 8, 1, N)


def _inproj_kernel(x_ref, g_ref, sc_ref, sh_ref, w_ref, cos_ref, sa_ref, sb_ref,
                   rope_ref, bq_ref, v_ref, cmp_ref, small_ref, gdn_ref):
    h = _rms_mod(x_ref[...], g_ref[...], sc_ref[...], sh_ref[...]).astype(BF16)
    yr = _dot(h, w_ref[:, 0:W_ROPE])
    bq_ref[...] = yr[:, A_Q:A_Q + B_Q].astype(BF16)
    c, sa, sb = cos_ref[...], sa_ref[...], sb_ref[...]
    for g in range(W_ROPE // 128):
        xg = yr[:, 128 * g:128 * (g + 1)]
        rot = xg * c + pltpu.roll(xg, 96, 1) * sa + pltpu.roll(xg, 32, 1) * sb
        rope_ref[:, 128 * g:128 * (g + 1)] = rot.astype(BF16)
    o = W_ROPE
    yp = _dot(h, w_ref[:, o:o + W_V + W_CMP + W_SMALL])
    v_ref[...] = yp[:, 0:W_V].astype(BF16)
    cmp_ref[...] = yp[:, W_V:W_V + W_CMP]
    small_ref[...] = yp[:, W_V + W_CMP:W_V + W_CMP + W_SMALL]
    o += W_V + W_CMP + W_SMALL
    gdn_ref[...] = _dot(h, w_ref[:, o:o + W_GDN])


def _in_projection(x, gain, mod, w, cos_t, sa_t, sb_t, tm=256):
    B, S, D = x.shape
    row = lambda b, i: (b, i, 0)
    outs = pl.pallas_call(
        _inproj_kernel,
        grid=(B, S // tm),
        in_specs=[
            pl.BlockSpec((None, tm, D), row),
            pl.BlockSpec((1, D), lambda b, i: (0, 0)),
            pl.BlockSpec((None, 1, D), lambda b, i: (b, 0, 1)),
            pl.BlockSpec((None, 1, D), lambda b, i: (b, 0, 0)),
            pl.BlockSpec((D, W_TOTAL), lambda b, i: (0, 0)),
            pl.BlockSpec((tm, 128), lambda b, i: (i, 0)),
            pl.BlockSpec((tm, 128), lambda b, i: (i, 0)),
            pl.BlockSpec((tm, 128), lambda b, i: (i, 0)),
        ],
        out_specs=[
            pl.BlockSpec((None, tm, W_ROPE), row),
            pl.BlockSpec((None, tm, B_Q), row),
            pl.BlockSpec((None, tm, W_V), row),
            pl.BlockSpec((None, tm, W_CMP), row),
            pl.BlockSpec((None, tm, W_SMALL), row),
            pl.BlockSpec((None, tm, W_GDN), row),
        ],
        out_shape=[
            jax.ShapeDtypeStruct((B, S, W_ROPE), BF16),
            jax.ShapeDtypeStruct((B, S, B_Q), BF16),
            jax.ShapeDtypeStruct((B, S, W_V), BF16),
            jax.ShapeDtypeStruct((B, S, W_CMP), F32),
            jax.ShapeDtypeStruct((B, S, W_SMALL), F32),
            jax.ShapeDtypeStruct((B, S, W_GDN), F32),
        ],
        compiler_params=_cparams("arbitrary", "arbitrary"),
        name="in_projection",
    )(x, gain, mod, mod, w, cos_t, sa_t, sb_t)
    return outs


def _swa_kernel(sink_ref, q_ref, k_ref, v_ref, o_ref, *, tq, window):
    q0 = pl.program_id(1) * tq
    span = tq + window
    start = pl.multiple_of(jnp.maximum(q0 - window, 0), 128)
    kk = k_ref[pl.ds(start, span), :]
    vv = v_ref[pl.ds(start, span), :]
    qpos = q0 + _iota((tq, span), 0)
    kpos = start + _iota((tq, span), 1)
    mask = (kpos <= qpos) & (kpos > qpos - window)
    q = q_ref[...]
    group = A_HEADS // A_KV_HEADS
    outs = []
    for h in range(A_HEADS):
        g = h // group
        s = _dot_nt(q[:, HEAD_DIM * h:HEAD_DIM * (h + 1)], kk[:, HEAD_DIM * g:HEAD_DIM * (g + 1)]) * ATTN_SCALE
        s = jnp.where(mask, s, NEG_INF)
        sink = sink_ref[h]
        m = jnp.maximum(jnp.max(s, axis=-1, keepdims=True), sink)
        p = jnp.exp(s - m)
        den = jnp.sum(p, axis=-1, keepdims=True) + jnp.exp(sink - m)
        outs.append(_dot(p.astype(BF16), vv[:, HEAD_DIM * g:HEAD_DIM * (g + 1)]) / den)
    o_ref[...] = jnp.concatenate(outs, axis=1).astype(BF16)


def _swa_attention(rope, vals, sinks, tq=256):
    B, S, _ = rope.shape
    return pl.pallas_call(
        functools.partial(_swa_kernel, tq=tq, window=A_WINDOW),
        grid=(B, S // tq),
        in_specs=[
            pl.BlockSpec(memory_space=pltpu.SMEM),
            pl.BlockSpec((None, tq, A_Q), lambda b, i: (b, i, 0)),
            pl.BlockSpec((None, S, A_KV), lambda b, i: (b, 0, (A_Q + B_Q) // A_KV)),
            pl.BlockSpec((None, S, A_KV), lambda b, i: (b, 0, 0)),
        ],
        out_specs=pl.BlockSpec((None, tq, A_Q), lambda b, i: (b, i, 0)),
        out_shape=jax.ShapeDtypeStruct((B, S, A_Q), BF16),
        compiler_params=_cparams("arbitrary", "arbitrary"),
        name="swa_attention",
    )(sinks, rope, rope, vals)


def _compress_kernel(x_ref, pea_ref, peb_ref, w1a_ref, w1b_ref, w2_ref, o_ref):
    x = x_ref[...]
    n = x.shape[0]
    a = _dot((x + pea_ref[...]).astype(BF16), w1a_ref[...])
    b = _dot((x + peb_ref[...]).astype(BF16), w1b_ref[...])
    hid = _silu(a + pltpu.roll(b, n - 1, 0))
    o_ref[...] = _dot(hid.astype(BF16), w2_ref[...]).astype(BF16)


def _compress(xc, pea, peb, w1a, w1b, w2):
    B, n, K = xc.shape
    full = lambda a: pl.BlockSpec(a.shape, lambda b: (0,) * a.ndim)
    return pl.pallas_call(
        _compress_kernel,
        grid=(B,),
        in_specs=[pl.BlockSpec((None, n, K), lambda b: (b, 0, 0)), full(pea), full(peb), full(w1a), full(w1b), full(w2)],
        out_specs=pl.BlockSpec((None, n, 2 * HEAD_DIM), lambda b: (b, 0, 0)),
        out_shape=jax.ShapeDtypeStruct((B, n, 2 * HEAD_DIM), BF16),
        compiler_params=_cparams("arbitrary"),
        name="nsa_compress",
    )(xc, pea, peb, w1a, w1b, w2)


def _cmp_select_kernel(q_ref, kv_ref, pool_ref, o_ref, sel_ref, *, tq, top_k):
    q0 = pl.program_id(1) * tq
    n_cmp = kv_ref.shape[0]
    n_slc = pool_ref.shape[1]
    kc = kv_ref[:, 0:HEAD_DIM]
    vc = kv_ref[:, HEAD_DIM:2 * HEAD_DIM]
    t = q0 + _iota((tq, n_cmp), 0)
    valid = _iota((tq, n_cmp), 1) * CMP_STRIDE + (CMP_LEN - 1) <= t
    q = q_ref[...]
    psum = jnp.zeros((tq, n_cmp), F32)
    outs = []
    for h in range(B_HEADS):
        s = _dot_nt(q[:, HEAD_DIM * h:HEAD_DIM * (h + 1)], kc) * ATTN_SCALE
        s = jnp.where(valid, s, NEG_INF)
        e = jnp.where(valid, jnp.exp(s - jnp.max(s, axis=-1, keepdims=True)), 0.0)
        den = jnp.sum(e, axis=-1, keepdims=True)
        p = e / jnp.where(den > 0.0, den, 1.0)
        outs.append(_dot(p.astype(BF16), vc))
        psum = psum + p
    o_ref[...] = jnp.concatenate(outs, axis=1)

    imp = _dot(psum, pool_ref[...], HIGHEST)
    blk = _iota((tq, n_slc), 1)
    tt = q0 + _iota((tq, n_slc), 0)
    cur = tt >> 6
    forced = (blk == 0) | (blk == cur) | (blk == cur - 1)
    causal = blk * SLC_LEN <= tt
    score = jnp.where(forced, SLC_FORCED_SCORE, jnp.where(causal, imp, NEG_INF))
    sel = jnp.zeros((tq, n_slc), F32)
    blk_f = blk.astype(F32)
    for _ in range(top_k):
        mx = jnp.max(score, axis=-1, keepdims=True)
        first = jnp.min(jnp.where(score == mx, blk_f, float(n_slc)), axis=-1, keepdims=True)
        pick = blk_f == first
        sel = jnp.where(pick, 1.0, sel)
        score = jnp.where(pick, -jnp.inf, score)
    sel_ref[...] = jnp.where(causal, sel, 0.0).astype(BF16)


def _cmp_select(bq, kv_cmp, pool, tq=128):
    B, S, _ = bq.shape
    n_cmp, n_slc = pool.shape
    top_k = min(SLC_TOPK, n_slc)
    return pl.pallas_call(
        functools.partial(_cmp_select_kernel, tq=tq, top_k=top_k),
        grid=(B, S // tq),
        in_specs=[
            pl.BlockSpec((None, tq, B_Q), lambda b, i: (b, i, 0)),
            pl.BlockSpec((None, n_cmp, 2 * HEAD_DIM), lambda b, i: (b, 0, 0)),
            pl.BlockSpec((n_cmp, n_slc), lambda b, i: (0, 0)),
        ],
        out_specs=[
            pl.BlockSpec((None, tq, B_Q), lambda b, i: (b, i, 0)),
            pl.BlockSpec((None, tq, n_slc), lambda b, i: (b, i, 0)),
        ],
        out_shape=[jax.ShapeDtypeStruct((B, S, B_Q), F32), jax.ShapeDtypeStruct((B, S, n_slc), BF16)],
        compiler_params=_cparams("arbitrary", "arbitrary"),
        name="nsa_cmp_select",
    )(bq, kv_cmp, pool)


def _nsa_attn_kernel(q_ref, k_ref, v_ref, sel_ref, ocmp_ref, small_ref, o_ref, *, tq, tk, window):
    i = pl.program_id(1)
    q0 = i * tq
    H = B_HEADS
    q = q_ref[...]
    qs = jnp.concatenate([q[:, HEAD_DIM * h:HEAD_DIM * (h + 1)] for h in range(H)], axis=0)
    sel = sel_ref[...]
    n_slc = sel.shape[1]
    blocks_per_tile = tk // SLC_LEN

    def body(j, carry):
        m, l, acc = carry
        k0 = pl.multiple_of(j * tk, tk)
        kt = k_ref[pl.ds(k0, tk), 0:HEAD_DIM]
        vt = v_ref[pl.ds(k0, tk), 0:HEAD_DIM]
        s = _dot_nt(qs, kt) * ATTN_SCALE
        expand = (_iota((n_slc, tk), 0) == j * blocks_per_tile + (_iota((n_slc, tk), 1) >> 6)).astype(BF16)
        picked = _dot(sel, expand)
        ok = (picked > 0.5) & (k0 + _iota((tq, tk), 1) <= q0 + _iota((tq, tk), 0))
        ok = jnp.concatenate([ok] * H, axis=0)
        s = jnp.where(ok, s, NEG_INF)
        m_new = jnp.maximum(m, jnp.max(s, axis=-1, keepdims=True))
        alpha = jnp.exp(m - m_new)
        p = jnp.where(ok, jnp.exp(s - m_new), 0.0)
        l = l * alpha + jnp.sum(p, axis=-1, keepdims=True)
        acc = acc * alpha + _dot(p.astype(BF16), vt)
        return m_new, l, acc

    n_tiles = (q0 + tq + tk - 1) // tk
    init = (jnp.full((H * tq, 1), NEG_INF, F32), jnp.zeros((H * tq, 1), F32), jnp.zeros((H * tq, HEAD_DIM), F32))
    _, l, acc = lax.fori_loop(0, n_tiles, body, init)
    o_slc = acc / l

    span = tq + window
    start = pl.multiple_of(jnp.maximum(q0 - window, 0), 128)
    kw = k_ref[pl.ds(start, span), HEAD_DIM:2 * HEAD_DIM]
    vw = v_ref[pl.ds(start, span), HEAD_DIM:2 * HEAD_DIM]
    qpos = q0 + _iota((tq, span), 0)
    kpos = start + _iota((tq, span), 1)
    band = (kpos <= qpos) & (kpos > qpos - window)
    band = jnp.concatenate([band] * H, axis=0)
    s = jnp.where(band, _dot_nt(qs, kw) * ATTN_SCALE, NEG_INF)
    p = jnp.exp(s - jnp.max(s, axis=-1, keepdims=True))
    o_win = _dot(p.astype(BF16), vw) / jnp.sum(p, axis=-1, keepdims=True)

    gates = jax.nn.sigmoid(small_ref[...])
    o_cmp = ocmp_ref[...]
    outs = []
    for h in range(H):
        rows = slice(tq * h, tq * (h + 1))
        outs.append(gates[:, h:h + 1] * o_cmp[:, HEAD_DIM * h:HEAD_DIM * (h + 1)]
                    + gates[:, H + h:H + h + 1] * o_slc[rows]
                    + gates[:, 2 * H + h:2 * H + h + 1] * o_win[rows])
    o_ref[...] = jnp.concatenate(outs, axis=1).astype(BF16)


def _nsa_attention(rope, vals, sel, o_cmp, small, tq=128, tk=256):
    B, S, _ = rope.shape
    n_slc = sel.shape[2]
    return pl.pallas_call(
        functools.partial(_nsa_attn_kernel, tq=tq, tk=tk, window=NSA_WINDOW),
        grid=(B, S // tq),
        in_specs=[
            pl.BlockSpec((None, tq, B_Q), lambda b, i: (b, i, A_Q // B_Q)),
            pl.BlockSpec((None, S, 128), lambda b, i: (b, 0, (A_Q + B_Q + A_KV) // 128)),
            pl.BlockSpec((None, S, 128), lambda b, i: (b, 0, A_KV // 128)),
            pl.BlockSpec((None, tq, n_slc), lambda b, i: (b, i, 0)),
            pl.BlockSpec((None, tq, B_Q), lambda b, i: (b, i, 0)),
            pl.BlockSpec((None, tq, W_SMALL), lambda b, i: (b, i, 0)),
        ],
        out_specs=pl.BlockSpec((None, tq, B_Q), lambda b, i: (b, i, 0)),
        out_shape=jax.ShapeDtypeStruct((B, S, B_Q), BF16),
        compiler_params=_cparams("arbitrary", "arbitrary"),
        name="nsa_attention",
    )(rope, rope, vals, sel, o_cmp, small)


def _gdn_kernel(xq_ref, xk_ref, xv_ref, hq_ref, hk_ref, hv_ref, z_ref, small_ref, arow_ref,
                cw_ref, alog_row_ref, dtb_row_ref, alog_col_ref, dtb_col_ref, nw_ref,
                o_ref, q_s, k_s, v_s, g_s, b_s, o_s, state_s, *, tt):
    i = pl.program_id(1)
    C = GDN_CHUNK
    dk = C_HEAD_DIM
    H = C_HEADS

    @pl.when(i == 0)
    def _():
        state_s[...] = jnp.zeros_like(state_s)

    def conv(x_ref, halo_ref, col0):
        x = x_ref[...]
        halo = jnp.where(i > 0, halo_ref[...], 0.0)
        xc = jnp.concatenate([halo, x], axis=0)
        w = cw_ref[:, col0:col0 + C_QK]
        y = x * w[3:4]
        for kback in range(1, CONV_WIDTH):
            y = y + xc[8 - kback:8 - kback + tt] * w[3 - kback:4 - kback]
        return _silu(y)

    qf = conv(xq_ref, hq_ref, 0)
    kf = conv(xk_ref, hk_ref, C_QK)
    v_s[...] = conv(xv_ref, hv_ref, 2 * C_QK)
    for h in range(H):
        cols = slice(dk * h, dk * (h + 1))
        qh = qf[:, cols]
        kh = kf[:, cols]
        q_s[:, cols] = qh * lax.rsqrt(jnp.sum(qh * qh, axis=-1, keepdims=True) + NORM_EPS) * (dk ** -0.5)
        k_s[:, cols] = kh * lax.rsqrt(jnp.sum(kh * kh, axis=-1, keepdims=True) + NORM_EPS)
    small = small_ref[...]
    g_s[...] = -jnp.exp(alog_row_ref[...]) * _softplus(small + dtb_row_ref[...])
    b_s[...] = jax.nn.sigmoid(small)

    r_i = _iota((C, C), 0)
    c_i = _iota((C, C), 1)
    lower = r_i >= c_i
    strict = r_i > c_i
    tri_l = lower.astype(F32)
    tri_u = (r_i <= c_i).astype(F32)
    neg_alog_col = -jnp.exp(alog_col_ref[...])
    dtb_col = dtb_col_ref[...]

    def chunk(c, carry):
        r0 = pl.multiple_of(c * C, C)
        rows = pl.ds(r0, C)
        gc_col = _dot(tri_l, g_s[rows, :], HIGHEST)
        g_row = neg_alog_col * _softplus(arow_ref[c] + dtb_col)
        gc_row = _dot(g_row, tri_u, HIGHEST)
        beta_all = b_s[rows, :]
        for h in range(H):
            cols = slice(dk * h, dk * (h + 1))
            q = q_s[rows, cols]
            k = k_s[rows, cols]
            v = v_s[rows, cols]
            gcol = gc_col[:, SMALL_A + h:SMALL_A + h + 1]
            grow = gc_row[h:h + 1, :]
            beta = beta_all[:, SMALL_B + h:SMALL_B + h + 1]
            decay = jnp.where(lower, jnp.exp(jnp.where(lower, gcol - grow, 0.0)), 0.0)
            eg = jnp.exp(gcol)
            kb = k * beta
            k16 = k.astype(BF16)
            a_mat = jnp.where(strict, _dot_nt(kb.astype(BF16), k16) * decay, 0.0)
            qk = jnp.where(lower, _dot_nt(q.astype(BF16), k16) * decay, 0.0)
            x = -a_mat
            pw = a_mat
            for _ in range(5):
                pw16 = pw.astype(BF16)
                pw = _dot(pw16, pw16)
                x = x + pw + _dot(x.astype(BF16), pw.astype(BF16))
            rhs = jnp.concatenate([v * beta, kb * eg], axis=1)
            sol = rhs + _dot(x.astype(BF16), rhs.astype(BF16))
            u = sol[:, 0:dk]
            w = sol[:, dk:2 * dk]
            g_last = gcol[C - 1:C, :]
            kd = k * jnp.exp(g_last - gcol)
            st = state_s[h]
            st16 = st.astype(BF16)
            v_new = u - _dot(w.astype(BF16), st16)
            v16 = v_new.astype(BF16)
            o_s[rows, cols] = _dot((q * eg).astype(BF16), st16) + _dot(qk.astype(BF16), v16)
            state_s[h] = st * jnp.exp(g_last) + _dot(kd.T.astype(BF16), v16)
        return carry

    lax.fori_loop(0, tt // C, chunk, 0)

    nw = nw_ref[...]
    z = z_ref[...]
    for h in range(H):
        cols = slice(dk * h, dk * (h + 1))
        oh = o_s[:, cols]
        y = oh * lax.rsqrt(jnp.mean(oh * oh, axis=-1, keepdims=True) + NORM_EPS) * nw
        o_ref[:, cols] = (y * _silu(z[:, cols])).astype(BF16)


def _gated_delta_net(gdn, small, a_rows, conv_w, alog_row, dtb_row, alog_col, dtb_col, norm_w, tt=512):
    B, S, _ = gdn.shape
    nch = tt // GDN_CHUNK
    part = lambda j: pl.BlockSpec((None, tt, C_QK), lambda b, i: (b, i, j))
    halo = lambda j: pl.BlockSpec((None, 8, C_QK), lambda b, i: (b, jnp.maximum(i * (tt // 8) - 1, 0), j))
    full = lambda a: pl.BlockSpec(a.shape, lambda b, i: (0,) * a.ndim)
    return pl.pallas_call(
        functools.partial(_gdn_kernel, tt=tt),
        grid=(B, S // tt),
        in_specs=[part(0), part(1), part(2), halo(0), halo(1), halo(2), part(3),
                  pl.BlockSpec((None, tt, W_SMALL), lambda b, i: (b, i, 0)),
                  pl.BlockSpec((None, nch, 8, GDN_CHUNK), lambda b, i: (b, i, 0, 0)),
                  full(conv_w), full(alog_row), full(dtb_row), full(alog_col), full(dtb_col), full(norm_w)],
        out_specs=pl.BlockSpec((None, tt, C_QK), lambda b, i: (b, i, 0)),
        out_shape=jax.ShapeDtypeStruct((B, S, C_QK), BF16),
        scratch_shapes=[pltpu.VMEM((tt, C_QK), F32), pltpu.VMEM((tt, C_QK), F32), pltpu.VMEM((tt, C_QK), F32),
                        pltpu.VMEM((tt, W_SMALL), F32), pltpu.VMEM((tt, W_SMALL), F32),
                        pltpu.VMEM((tt, C_QK), F32), pltpu.VMEM((C_HEADS, C_HEAD_DIM, C_HEAD_DIM), F32)],
        compiler_params=_cparams("arbitrary", "arbitrary"),
        name="gated_delta_net",
    )(gdn, gdn, gdn, gdn, gdn, gdn, gdn, small, a_rows, conv_w, alog_row, dtb_row, alog_col, dtb_col, norm_w)


def _outproj_kernel(x_ref, oa_ref, ob_ref, oc_ref, w_ref, gt_ref, o_ref):
    y = _dot(oa_ref[...], w_ref[0:A_Q, :])
    y = y + _dot(ob_ref[...], w_ref[A_Q:A_Q + B_Q, :])
    y = y + _dot(oc_ref[...], w_ref[A_Q + B_Q:, :])
    o_ref[...] = x_ref[...] + gt_ref[...] * y


def _out_projection(x, o_a, o_b, o_c, w, mod, tm=512):
    B, S, D = x.shape
    row = lambda b, i: (b, i, 0)
    return pl.pallas_call(
        _outproj_kernel,
        grid=(B, S // tm),
        in_specs=[
            pl.BlockSpec((None, tm, D), row),
            pl.BlockSpec((None, tm, A_Q), row),
            pl.BlockSpec((None, tm, B_Q), row),
            pl.BlockSpec((None, tm, C_QK), row),
            pl.BlockSpec((D, D), lambda b, i: (0, 0)),
            pl.BlockSpec((None, 1, D), lambda b, i: (b, 0, 2)),
        ],
        out_specs=pl.BlockSpec((None, tm, D), row),
        out_shape=jax.ShapeDtypeStruct((B, S, D), F32),
        compiler_params=_cparams("arbitrary", "arbitrary"),
        name="out_projection",
    )(x, o_a, o_b, o_c, w, mod)


def _ffn_kernel(x_ref, g_ref, sc_ref, sh_ref, gt_ref, wg_ref, wu_ref, wd_ref, o_ref, h_s, acc_s):
    f = pl.program_id(2)

    @pl.when(f == 0)
    def _():
        h_s[...] = _rms_mod(x_ref[...], g_ref[...], sc_ref[...], sh_ref[...]).astype(BF16)
        acc_s[...] = jnp.zeros_like(acc_s)

    h = h_s[...]
    act = _silu(_dot(h, wg_ref[...])) * _dot(h, wu_ref[...])
    acc_s[...] += _dot(act.astype(BF16), wd_ref[...])

    @pl.when(f == pl.num_programs(2) - 1)
    def _():
        o_ref[...] = x_ref[...] + gt_ref[...] * acc_s[...]


def _ffn(x, gain, mod, w_gate_up, w_down, tm=512, tf=1408):
    B, S, D = x.shape
    nf = D_FF // tf
    row = lambda b, i, f: (b, i, 0)
    return pl.pallas_call(
        _ffn_kernel,
        grid=(B, S // tm, nf),
        in_specs=[
            pl.BlockSpec((None, tm, D), row),
            pl.BlockSpec((1, D), lambda b, i, f: (0, 0)),
            pl.BlockSpec((None, 1, D), lambda b, i, f: (b, 0, 4)),
            pl.BlockSpec((None, 1, D), lambda b, i, f: (b, 0, 3)),
            pl.BlockSpec((None, 1, D), lambda b, i, f: (b, 0, 5)),
            pl.BlockSpec((D, tf), lambda b, i, f: (0, f)),
            pl.BlockSpec((D, tf), lambda b, i, f: (0, nf + f)),
            pl.BlockSpec((tf, D), lambda b, i, f: (f, 0)),
        ],
        out_specs=pl.BlockSpec((None, tm, D), row),
        out_shape=jax.ShapeDtypeStruct((B, S, D), F32),
        scratch_shapes=[pltpu.VMEM((tm, D), BF16), pltpu.VMEM((tm, D), F32)],
        compiler_params=_cparams("arbitrary", "arbitrary", "arbitrary"),
        name="swiglu_ffn",
    )(x, gain, mod, mod, mod, w_gate_up, w_gate_up, w_down)


def _final_norm_kernel(x_ref, g_ref, o_ref):
    x = x_ref[...]
    o_ref[...] = x * lax.rsqrt(jnp.mean(x * x, axis=-1, keepdims=True) + NORM_EPS) * g_ref[...]


def _final_norm(x, gain, tm=512):
    B, S, D = x.shape
    return pl.pallas_call(
        _final_norm_kernel,
        grid=(B, S // tm),
        in_specs=[pl.BlockSpec((None, tm, D), lambda b, i: (b, i, 0)), pl.BlockSpec((1, D), lambda b, i: (0, 0))],
        out_specs=pl.BlockSpec((None, tm, D), lambda b, i: (b, i, 0)),
        out_shape=jax.ShapeDtypeStruct((B, S, D), F32),
        compiler_params=_cparams("arbitrary", "arbitrary"),
        name="final_norm",
    )(x, gain)


def _rope_tables(seq):
    inv = 1.0 / (ROPE_THETA ** (jnp.arange(0, HEAD_DIM, 2, dtype=F32) / HEAD_DIM))
    ang = jnp.arange(seq, dtype=F32)[:, None] * inv[None, :]
    cos, sin = jnp.cos(ang), jnp.sin(ang)
    zero = jnp.zeros_like(sin)
    cos_t = jnp.tile(cos, (1, 4))
    sa_t = jnp.tile(jnp.concatenate([-sin, zero], axis=1), (1, 2))
    sb_t = jnp.tile(jnp.concatenate([zero, sin], axis=1), (1, 2))
    return cos_t, sa_t, sb_t


def _permute_w_in(w_in):
    pts = np.cumsum(IN_SPLITS)[:-1].tolist()
    aq, ak, av, bq, bkc, bvc, bks, bvs, bkw, bvw, bg, cqkv, cz, ca, cb = jnp.split(w_in, pts, axis=-1)
    pad = jnp.zeros(w_in.shape[:-1] + (W_SMALL - B_GATES - 2 * C_HEADS,), w_in.dtype)
    return jnp.concatenate([aq, bq, ak, bks, bkw, av, bvs, bvw, bkc, bvc, bg, ca, cb, pad, cqkv, cz], axis=-1).astype(BF16)


def _compress_weights(k_w1, k_w2, v_w1, v_w2, pe_k, pe_v):
    L = k_w1.shape[0]
    half = CMP_STRIDE * HEAD_DIM

    def first_layer(lo):
        wk = k_w1[:, lo:lo + half].reshape(L, CMP_STRIDE, HEAD_DIM, CMP_HIDDEN)
        wv = v_w1[:, lo:lo + half].reshape(L, CMP_STRIDE, HEAD_DIM, CMP_HIDDEN)
        z = jnp.zeros_like(wk)
        w = jnp.concatenate([jnp.concatenate([wk, z], axis=-1), jnp.concatenate([z, wv], axis=-1)], axis=2)
        return w.reshape(L, 2 * half, 2 * CMP_HIDDEN).astype(BF16)

    def pe_rows(lo):
        return jnp.concatenate([pe_k[:, lo:lo + CMP_STRIDE], pe_v[:, lo:lo + CMP_STRIDE]], axis=-1).reshape(L, 1, 2 * half)

    z2 = jnp.zeros_like(k_w2)
    w2 = jnp.concatenate([jnp.concatenate([k_w2, z2], axis=-1), jnp.concatenate([z2, v_w2], axis=-1)], axis=1).astype(BF16)
    return pe_rows(0), pe_rows(CMP_STRIDE), first_layer(0), first_layer(half), w2


def _pool_matrix(n_cmp_rows, n_slc):
    ratio = SLC_LEN // CMP_STRIDE
    n = np.arange(n_cmp_rows)[:, None]
    j = np.arange(n_slc)[None, :]
    return jnp.asarray(((n // ratio == j) | (n == ratio * j - 1)).astype(np.float32))


def _lane_row(v, lane0):
    L, n = v.shape
    return jnp.zeros((L, 1, 128), F32).at[:, 0, lane0:lane0 + n].set(v)


def _sublane_col(v):
    L, n = v.shape
    return jnp.zeros((L, 8, 1), F32).at[:, :n, 0].set(v)


def kernel(x, c, norm_mix, norm_ffn, ada_w, ada_b, w_in, attn_sinks, cmp_k_w1, cmp_k_w2, cmp_v_w1, cmp_v_w2, cmp_pe_k, cmp_pe_v, gdn_conv_w, gdn_A_log, gdn_dt_bias, gdn_norm, w_out, w_gate_up, w_down, final_norm):
    B, S, D = x.shape
    L = w_in.shape[0]
    n16 = S // CMP_STRIDE
    n_slc = S // SLC_LEN

    cos_t, sa_t, sb_t = _rope_tables(S)
    mod = _adaln_mod(c, ada_w, ada_b)
    pea, peb, w1a, w1b, w2c = _compress_weights(cmp_k_w1, cmp_k_w2, cmp_v_w1, cmp_v_w2, cmp_pe_k, cmp_pe_v)
    pool = _pool_matrix(n16, n_slc)
    per_layer = dict(
        mod=mod,
        norm_mix=norm_mix.reshape(L, 1, D), norm_ffn=norm_ffn.reshape(L, 1, D),
        w_in=_permute_w_in(w_in), sinks=attn_sinks,
        pea=pea, peb=peb, w1a=w1a, w1b=w1b, w2c=w2c,
        conv_w=gdn_conv_w,
        alog_row=_lane_row(gdn_A_log, SMALL_A), dtb_row=_lane_row(gdn_dt_bias, SMALL_A),
        alog_col=_sublane_col(gdn_A_log), dtb_col=_sublane_col(gdn_dt_bias),
        gdn_norm=gdn_norm.reshape(L, 1, C_HEAD_DIM),
        w_out=w_out.astype(BF16), w_gate_up=w_gate_up.astype(BF16), w_down=w_down.astype(BF16),
    )

    def layer(xc, p):
        rope, bq, vals, cmp_in, small, gdn = _in_projection(xc, p["norm_mix"], p["mod"], p["w_in"], cos_t, sa_t, sb_t)
        o_a = _swa_attention(rope, vals, p["sinks"])
        kv_cmp = _compress(cmp_in.reshape(B, n16, CMP_STRIDE * W_CMP), p["pea"], p["peb"], p["w1a"], p["w1b"], p["w2c"])
        o_cmp, sel = _cmp_select(bq, kv_cmp, pool)
        o_b = _nsa_attention(rope, vals, sel, o_cmp, small)
        a_rows = small[:, :, SMALL_A:SMALL_A + 8].reshape(B, S // GDN_CHUNK, GDN_CHUNK, 8).transpose(0, 1, 3, 2)
        o_c = _gated_delta_net(gdn, small, a_rows, p["conv_w"], p["alog_row"], p["dtb_row"], p["alog_col"], p["dtb_col"], p["gdn_norm"])
        xc = _out_projection(xc, o_a, o_b, o_c, p["w_out"], p["mod"])
        xc = _ffn(xc, p["norm_ffn"], p["mod"], p["w_gate_up"], p["w_down"])
        return xc, None

    x, _ = lax.scan(layer, x, per_layer)
    return _final_norm(x, final_norm.reshape(1, D))
```

```python
import functools
import math

import numpy as np
import jax
import jax.numpy as jnp
from jax import lax
from jax.experimental import pallas as pl
from jax.experimental.pallas import tpu as pltpu

F32 = jnp.float32
BF16 = jnp.bfloat16
HIGHEST = lax.Precision.HIGHEST

D_MODEL = 1024
DEPTH = 4
HEAD_DIM = 64
ATTN_SCALE = HEAD_DIM ** -0.5
ROPE_THETA = 10000.0
NEG_INF = -1e30
NORM_EPS = 1e-6

A_HEADS = 4
A_KV_HEADS = 2
A_WINDOW = 128

B_HEADS = 4
CMP_STRIDE = 16
CMP_LEN = 32
CMP_HIDDEN = 256
SLC_LEN = 64
SLC_TOPK = 16
NSA_WINDOW = 512
SLC_FORCED_SCORE = 1e9

C_HEAD_DIM = 128
C_HEADS = 4
CONV_WIDTH = 4
GDN_CHUNK = 64
C_QK = C_HEADS * C_HEAD_DIM

D_FF = 2816

A_Q = A_HEADS * HEAD_DIM
A_KV = A_KV_HEADS * HEAD_DIM
B_Q = B_HEADS * HEAD_DIM
B_KV = HEAD_DIM
B_GATES = 3 * B_HEADS
IN_SPLITS = (A_Q, A_KV, A_KV, B_Q, B_KV, B_KV, B_KV, B_KV, B_KV, B_KV, B_GATES, 3 * C_QK, C_QK, C_HEADS, C_HEADS)

W_ROPE = 768
W_V = 256
W_CMP = 128
W_SMALL = 128
W_GDN = 2048
W_TOTAL = W_ROPE + W_V + W_CMP + W_SMALL + W_GDN
SMALL_A = B_GATES
SMALL_B = B_GATES + C_HEADS

VMEM_LIMIT = 56 * 1024 * 1024

NT_DIMS = (((1,), (1,)), ((), ()))


def _cparams(*sem):
    return pltpu.CompilerParams(dimension_semantics=sem, vmem_limit_bytes=VMEM_LIMIT)


def _iota(shape, dim):
    return lax.broadcasted_iota(jnp.int32, shape, dim)


def _dot(a, b, precision=None):
    return jnp.dot(a, b, preferred_element_type=F32, precision=precision)


def _dot_nt(a, b):
    return lax.dot_general(a, b, NT_DIMS, preferred_element_type=F32)


def _silu(x):
    return x * jax.nn.sigmoid(x)


def _softplus(x):
    return jnp.maximum(x, 0.0) + jnp.log(1.0 + jnp.exp(-jnp.abs(x)))


def _rms_mod(x, gain, sc, sh):
    y = x * lax.rsqrt(jnp.mean(x * x, axis=-1, keepdims=True) + NORM_EPS)
    return (y * gain) * (1.0 + sc) + sh


def _mod_kernel(c_ref, w_ref, b_ref, o_ref):
    o_ref[...] = _dot(_silu(c_ref[...]), w_ref[...], HIGHEST) + b_ref[...]


def _adaln_mod(c, ada_w, ada_b):
    L, D, N = ada_w.shape
    tn = 1536
    c8 = jnp.zeros((8, D), F32).at[: c.shape[0]].set(c)
    out = pl.pallas_call(
        _mod_kernel,
        grid=(L, N // tn),
        in_specs=[
            pl.BlockSpec((8, D), lambda l, j: (0, 0)),
            pl.BlockSpec((None, D, tn), lambda l, j: (l, 0, j)),
            pl.BlockSpec((None, 1, tn), lambda l, j: (l, 0, j)),
        ],
        out_specs=pl.BlockSpec((None, 8, tn), lambda l, j: (l, 0, j)),
        out_shape=jax.ShapeDtypeStruct((L, 8, N), F32),
        compiler_params=_cparams("arbitrary", "arbitrary"),
        name="adaln_mod",
    )(c8, ada_w, ada_b.reshape(L, 1, N))
    return out.reshape(L, 8, 1, N)


def _inproj_kernel(x_ref, g_ref, sc_ref, sh_ref, w_ref, cos_ref, sa_ref, sb_ref,
                   rope_ref, bq_ref, v_ref, cmp_ref, small_ref, gdn_ref):
    h = _rms_mod(x_ref[...], g_ref[...], sc_ref[...], sh_ref[...]).astype(BF16)
    yr = _dot(h, w_ref[:, 0:W_ROPE])
    bq_ref[...] = yr[:, A_Q:A_Q + B_Q].astype(BF16)
    c, sa, sb = cos_ref[...], sa_ref[...], sb_ref[...]
    for g in range(W_ROPE // 128):
        xg = yr[:, 128 * g:128 * (g + 1)]
        rot = xg * c + pltpu.roll(xg, 96, 1) * sa + pltpu.roll(xg, 32, 1) * sb
        rope_ref[:, 128 * g:128 * (g + 1)] = rot.astype(BF16)
    o = W_ROPE
    yp = _dot(h, w_ref[:, o:o + W_V + W_CMP + W_SMALL])
    v_ref[...] = yp[:, 0:W_V].astype(BF16)
    cmp_ref[...] = yp[:, W_V:W_V + W_CMP]
    small_ref[...] = yp[:, W_V + W_CMP:W_V + W_CMP + W_SMALL]
    o += W_V + W_CMP + W_SMALL
    gdn_ref[...] = _dot(h, w_ref[:, o:o + W_GDN])


def _in_projection(x, gain, mod, w, cos_t, sa_t, sb_t, tm=256):
    B, S, D = x.shape
    row = lambda b, i: (b, i, 0)
    outs = pl.pallas_call(
        _inproj_kernel,
        grid=(B, S // tm),
        in_specs=[
            pl.BlockSpec((None, tm, D), row),
            pl.BlockSpec((1, D), lambda b, i: (0, 0)),
            pl.BlockSpec((None, 1, D), lambda b, i: (b, 0, 1)),
            pl.BlockSpec((None, 1, D), lambda b, i: (b, 0, 0)),
            pl.BlockSpec((D, W_TOTAL), lambda b, i: (0, 0)),
            pl.BlockSpec((tm, 128), lambda b, i: (i, 0)),
            pl.BlockSpec((tm, 128), lambda b, i: (i, 0)),
            pl.BlockSpec((tm, 128), lambda b, i: (i, 0)),
        ],
        out_specs=[
            pl.BlockSpec((None, tm, W_ROPE), row),
            pl.BlockSpec((None, tm, B_Q), row),
            pl.BlockSpec((None, tm, W_V), row),
            pl.BlockSpec((None, tm, W_CMP), row),
            pl.BlockSpec((None, tm, W_SMALL), row),
            pl.BlockSpec((None, tm, W_GDN), row),
        ],
        out_shape=[
            jax.ShapeDtypeStruct((B, S, W_ROPE), BF16),
            jax.ShapeDtypeStruct((B, S, B_Q), BF16),
            jax.ShapeDtypeStruct((B, S, W_V), BF16),
            jax.ShapeDtypeStruct((B, S, W_CMP), F32),
            jax.ShapeDtypeStruct((B, S, W_SMALL), F32),
            jax.ShapeDtypeStruct((B, S, W_GDN), F32),
        ],
        compiler_params=_cparams("arbitrary", "arbitrary"),
        name="in_projection",
    )(x, gain, mod, mod, w, cos_t, sa_t, sb_t)
    return outs


def _swa_kernel(sink_ref, q_ref, k_ref, v_ref, o_ref, *, tq, window):
    q0 = pl.program_id(1) * tq
    span = tq + window
    start = pl.multiple_of(jnp.maximum(q0 - window, 0), 128)
    kk = k_ref[pl.ds(start, span), :]
    vv = v_ref[pl.ds(start, span), :]
    qpos = q0 + _iota((tq, span), 0)
    kpos = start + _iota((tq, span), 1)
    mask = (kpos <= qpos) & (kpos > qpos - window)
    q = q_ref[...]
    group = A_HEADS // A_KV_HEADS
    outs = []
    for h in range(A_HEADS):
        g = h // group
        s = _dot_nt(q[:, HEAD_DIM * h:HEAD_DIM * (h + 1)], kk[:, HEAD_DIM * g:HEAD_DIM * (g + 1)]) * ATTN_SCALE
        s = jnp.where(mask, s, NEG_INF)
        sink = sink_ref[h]
        m = jnp.maximum(jnp.max(s, axis=-1, keepdims=True), sink)
        p = jnp.exp(s - m)
        den = jnp.sum(p, axis=-1, keepdims=True) + jnp.exp(sink - m)
        outs.append(_dot(p.astype(BF16), vv[:, HEAD_DIM * g:HEAD_DIM * (g + 1)]) / den)
    o_ref[...] = jnp.concatenate(outs, axis=1).astype(BF16)


def _swa_attention(rope, vals, sinks, tq=256):
    B, S, _ = rope.shape
    return pl.pallas_call(
        functools.partial(_swa_kernel, tq=tq, window=A_WINDOW),
        grid=(B, S // tq),
        in_specs=[
            pl.BlockSpec(memory_space=pltpu.SMEM),
            pl.BlockSpec((None, tq, A_Q), lambda b, i: (b, i, 0)),
            pl.BlockSpec((None, S, A_KV), lambda b, i: (b, 0, (A_Q + B_Q) // A_KV)),
            pl.BlockSpec((None, S, A_KV), lambda b, i: (b, 0, 0)),
        ],
        out_specs=pl.BlockSpec((None, tq, A_Q), lambda b, i: (b, i, 0)),
        out_shape=jax.ShapeDtypeStruct((B, S, A_Q), BF16),
        compiler_params=_cparams("arbitrary", "arbitrary"),
        name="swa_attention",
    )(sinks, rope, rope, vals)


def _compress_kernel(x_ref, pea_ref, peb_ref, w1a_ref, w1b_ref, w2_ref, o_ref, ot_ref):
    x = x_ref[...]
    n = x.shape[0]
    a = _dot((x + pea_ref[...]).astype(BF16), w1a_ref[...])
    b = _dot((x + peb_ref[...]).astype(BF16), w1b_ref[...])
    hid = _silu(a + pltpu.roll(b, n - 1, 0))
    out = _dot(hid.astype(BF16), w2_ref[...])
    o_ref[...] = out.astype(BF16)
    ot_ref[...] = out.T.astype(BF16)


def _compress(xc, pea, peb, w1a, w1b, w2):
    B, n, K = xc.shape
    full = lambda a: pl.BlockSpec(a.shape, lambda b: (0,) * a.ndim)
    return pl.pallas_call(
        _compress_kernel,
        grid=(B,),
        in_specs=[pl.BlockSpec((None, n, K), lambda b: (b, 0, 0)), full(pea), full(peb), full(w1a), full(w1b), full(w2)],
        out_specs=[pl.BlockSpec((None, n, 2 * HEAD_DIM), lambda b: (b, 0, 0)),
                   pl.BlockSpec((None, 2 * HEAD_DIM, n), lambda b: (b, 0, 0))],
        out_shape=[jax.ShapeDtypeStruct((B, n, 2 * HEAD_DIM), BF16), jax.ShapeDtypeStruct((B, 2 * HEAD_DIM, n), BF16)],
        compiler_params=_cparams("arbitrary"),
        name="nsa_compress",
    )(xc, pea, peb, w1a, w1b, w2)


def _heads_on_lanes(q):
    qt = q.astype(F32).T
    return jnp.concatenate([qt[HEAD_DIM * h:HEAD_DIM * (h + 1)] for h in range(B_HEADS)], axis=1).astype(BF16)


def _heads_on_sublanes(ot, tq):
    return jnp.concatenate([ot[:, tq * h:tq * (h + 1)] for h in range(B_HEADS)], axis=0)


def _cmp_select_kernel(q_ref, k_ref, vt_ref, pool_ref, o_ref, sel_ref, *, tq, top_k):
    q0 = pl.program_id(1) * tq
    H = B_HEADS
    n_cmp = k_ref.shape[0]
    n_slc = pool_ref.shape[0]
    qt = _heads_on_lanes(q_ref[...])
    s = _dot(k_ref[:, 0:HEAD_DIM], qt) * ATTN_SCALE
    valid = _iota((n_cmp, tq), 0) * CMP_STRIDE + (CMP_LEN - 1) <= q0 + _iota((n_cmp, tq), 1)
    valid = jnp.concatenate([valid] * H, axis=1)
    s = jnp.where(valid, s, NEG_INF)
    e = jnp.where(valid, jnp.exp(s - jnp.max(s, axis=0, keepdims=True)), 0.0)
    den = jnp.sum(e, axis=0, keepdims=True)
    p = e / jnp.where(den > 0.0, den, 1.0)
    o_t = _dot(vt_ref[HEAD_DIM:2 * HEAD_DIM, :], p.astype(BF16))
    o_ref[...] = _heads_on_sublanes(o_t, tq)
    psum = p[:, 0:tq]
    for h in range(1, H):
        psum = psum + p[:, tq * h:tq * (h + 1)]

    imp = _dot(pool_ref[...], psum, HIGHEST)
    blk = _iota((n_slc, tq), 0)
    tt = q0 + _iota((n_slc, tq), 1)
    cur = tt >> 6
    forced = (blk == 0) | (blk == cur) | (blk == cur - 1)
    causal = blk * SLC_LEN <= tt
    score = jnp.where(forced, SLC_FORCED_SCORE, jnp.where(causal, imp, NEG_INF))
    sel = jnp.zeros((n_slc, tq), F32)
    blk_f = blk.astype(F32)
    for _ in range(top_k):
        mx = jnp.max(score, axis=0, keepdims=True)
        first = jnp.min(jnp.where(score == mx, blk_f, float(n_slc)), axis=0, keepdims=True)
        pick = blk_f == first
        sel = jnp.where(pick, 1.0, sel)
        score = jnp.where(pick, -jnp.inf, score)
    sel_ref[...] = jnp.where(causal, sel, 0.0)


def _cmp_select(bq, kv_cmp, kv_cmp_t, pool_t, tq=128):
    B, S, _ = bq.shape
    n_slc, n_cmp = pool_t.shape
    top_k = min(SLC_TOPK, n_slc)
    return pl.pallas_call(
        functools.partial(_cmp_select_kernel, tq=tq, top_k=top_k),
        grid=(B, S // tq),
        in_specs=[
            pl.BlockSpec((None, tq, B_Q), lambda b, i: (b, i, 0)),
            pl.BlockSpec((None, n_cmp, 2 * HEAD_DIM), lambda b, i: (b, 0, 0)),
            pl.BlockSpec((None, 2 * HEAD_DIM, n_cmp), lambda b, i: (b, 0, 0)),
            pl.BlockSpec((n_slc, n_cmp), lambda b, i: (0, 0)),
        ],
        out_specs=[
            pl.BlockSpec((None, B_Q, tq), lambda b, i: (b, 0, i)),
            pl.BlockSpec((None, n_slc, tq), lambda b, i: (b, 0, i)),
        ],
        out_shape=[jax.ShapeDtypeStruct((B, B_Q, S), F32), jax.ShapeDtypeStruct((B, n_slc, S), F32)],
        compiler_params=_cparams("arbitrary", "arbitrary"),
        name="nsa_cmp_select",
    )(bq, kv_cmp, kv_cmp_t, pool_t)


KV_BLOCK = 128


def _nsa_attn_kernel(q_ref, k_ref, vt_ref, sel_ref, ocmp_ref, small_ref, o_ref, *, tq, tk, window):
    i = pl.program_id(1)
    q0 = i * tq
    H = B_HEADS
    qt = _heads_on_lanes(q_ref[...])
    blocks_per_tile = tk // SLC_LEN
    vblocks_per_tile = tk // KV_BLOCK
    qpos = q0 + _iota((tk, tq), 1)

    def body(j, carry):
        m, l, acc = carry
        k0 = pl.multiple_of(j * tk, tk)
        s = _dot(k_ref[pl.ds(k0, tk), 0:HEAD_DIM], qt) * ATTN_SCALE
        sel = sel_ref[pl.ds(pl.multiple_of(j * blocks_per_tile, blocks_per_tile), blocks_per_tile), :]
        picked = jnp.concatenate(
            [jnp.broadcast_to(sel[r:r + 1, :], (SLC_LEN, tq)) for r in range(blocks_per_tile)], axis=0)
        ok = (picked > 0.5) & (k0 + _iota((tk, tq), 0) <= qpos)
        ok = jnp.concatenate([ok] * H, axis=1)
        s = jnp.where(ok, s, NEG_INF)
        m_new = jnp.maximum(m, jnp.max(s, axis=0, keepdims=True))
        alpha = jnp.exp(m - m_new)
        p = jnp.where(ok, jnp.exp(s - m_new), 0.0)
        l = l * alpha + jnp.sum(p, axis=0, keepdims=True)
        vb = vt_ref[pl.ds(pl.multiple_of(j * vblocks_per_tile, vblocks_per_tile), vblocks_per_tile)]
        vt = jnp.concatenate([vb[r, 0:HEAD_DIM, :] for r in range(vblocks_per_tile)], axis=1)
        acc = acc * alpha + _dot(vt, p.astype(BF16))
        return m_new, l, acc

    n_tiles = (q0 + tq + tk - 1) // tk
    init = (jnp.full((1, H * tq), NEG_INF, F32), jnp.zeros((1, H * tq), F32), jnp.zeros((HEAD_DIM, H * tq), F32))
    _, l, acc = lax.fori_loop(0, n_tiles, body, init)
    o_slc = _heads_on_sublanes(acc / l, tq)

    span = tq + window
    start = pl.multiple_of(jnp.maximum(q0 - window, 0), KV_BLOCK)
    s = _dot(k_ref[pl.ds(start, span), HEAD_DIM:2 * HEAD_DIM], qt) * ATTN_SCALE
    kpos = start + _iota((span, tq), 0)
    qp = q0 + _iota((span, tq), 1)
    band = (kpos <= qp) & (kpos > qp - window)
    band = jnp.concatenate([band] * H, axis=1)
    s = jnp.where(band, s, NEG_INF)
    p = jnp.exp(s - jnp.max(s, axis=0, keepdims=True))
    vb = vt_ref[pl.ds(start // KV_BLOCK, span // KV_BLOCK)]
    vt = jnp.concatenate([vb[r, HEAD_DIM:2 * HEAD_DIM, :] for r in range(span // KV_BLOCK)], axis=1)
    o_win = _heads_on_sublanes(_dot(vt, p.astype(BF16)) / jnp.sum(p, axis=0, keepdims=True), tq)

    gates = jax.nn.sigmoid(small_ref[...]).T
    o_cmp = ocmp_ref[...]
    outs = []
    for h in range(H):
        rows = slice(HEAD_DIM * h, HEAD_DIM * (h + 1))
        outs.append(gates[h:h + 1, :] * o_cmp[rows] + gates[H + h:H + h + 1, :] * o_slc[rows]
                    + gates[2 * H + h:2 * H + h + 1, :] * o_win[rows])
    o_ref[...] = jnp.concatenate(outs, axis=0).T.astype(BF16)


def _nsa_attention(rope, vals_t, sel, o_cmp, small, tq=128, tk=512):
    B, S, _ = rope.shape
    n_slc = sel.shape[1]
    return pl.pallas_call(
        functools.partial(_nsa_attn_kernel, tq=tq, tk=tk, window=NSA_WINDOW),
        grid=(B, S // tq),
        in_specs=[
            pl.BlockSpec((None, tq, B_Q), lambda b, i: (b, i, A_Q // B_Q)),
            pl.BlockSpec((None, S, 128), lambda b, i: (b, 0, (A_Q + B_Q + A_KV) // 128)),
            pl.BlockSpec((None, S // KV_BLOCK, 128, KV_BLOCK), lambda b, i: (b, 0, 0, 0)),
            pl.BlockSpec((None, n_slc, tq), lambda b, i: (b, 0, i)),
            pl.BlockSpec((None, B_Q, tq), lambda b, i: (b, 0, i)),
            pl.BlockSpec((None, tq, W_SMALL), lambda b, i: (b, i, 0)),
        ],
        out_specs=pl.BlockSpec((None, tq, B_Q), lambda b, i: (b, i, 0)),
        out_shape=jax.ShapeDtypeStruct((B, S, B_Q), BF16),
        compiler_params=_cparams("arbitrary", "arbitrary"),
        name="nsa_attention",
    )(rope, rope, vals_t, sel, o_cmp, small)


def _gdn_kernel(xq_ref, xk_ref, xv_ref, hq_ref, hk_ref, hv_ref, z_ref, small_ref, arow_ref,
                cw_ref, alog_row_ref, dtb_row_ref, alog_col_ref, dtb_col_ref, nw_ref,
                o_ref, q_s, k_s, v_s, g_s, b_s, o_s, state_s, u_s, wq_s, qk_s, kdt_s, gl_s, *, tt):
    i = pl.program_id(1)
    C = GDN_CHUNK
    dk = C_HEAD_DIM
    H = C_HEADS

    @pl.when(i == 0)
    def _():
        state_s[...] = jnp.zeros_like(state_s)

    def conv(x_ref, halo_ref, col0):
        x = x_ref[...]
        halo = jnp.where(i > 0, halo_ref[...], 0.0)
        xc = jnp.concatenate([halo, x], axis=0)
        w = cw_ref[:, col0:col0 + C_QK]
        y = x * w[3:4]
        for kback in range(1, CONV_WIDTH):
            y = y + xc[8 - kback:8 - kback + tt] * w[3 - kback:4 - kback]
        return _silu(y)

    qf = conv(xq_ref, hq_ref, 0)
    kf = conv(xk_ref, hk_ref, C_QK)
    v_s[...] = conv(xv_ref, hv_ref, 2 * C_QK)
    for h in range(H):
        cols = slice(dk * h, dk * (h + 1))
        qh = qf[:, cols]
        kh = kf[:, cols]
        q_s[:, cols] = qh * lax.rsqrt(jnp.sum(qh * qh, axis=-1, keepdims=True) + NORM_EPS) * (dk ** -0.5)
        k_s[:, cols] = kh * lax.rsqrt(jnp.sum(kh * kh, axis=-1, keepdims=True) + NORM_EPS)
    small = small_ref[...]
    g_s[...] = -jnp.exp(alog_row_ref[...]) * _softplus(small + dtb_row_ref[...])
    b_s[...] = jax.nn.sigmoid(small)

    r_i = _iota((C, C), 0)
    c_i = _iota((C, C), 1)
    lower = r_i >= c_i
    strict = r_i > c_i
    tri_l = lower.astype(F32)
    tri_u = (r_i <= c_i).astype(F32)
    neg_alog_col = -jnp.exp(alog_col_ref[...])
    dtb_col = dtb_col_ref[...]

    HC = H * C
    rr = _iota((HC, HC), 0)
    cc = _iota((HC, HC), 1)
    same_head = (rr >> 6) == (cc >> 6)
    lower_bd = same_head & (rr >= cc)
    strict_bd = same_head & (rr > cc)
    state_bd = (_iota((HC, H * dk), 0) >> 6) == (_iota((HC, H * dk), 1) >> 7)

    def stack_rows(ref, rows):
        return jnp.concatenate([ref[rows, dk * h:dk * (h + 1)] for h in range(H)], axis=0)

    def prepare(c):
        r0 = pl.multiple_of(c * C, C)
        rows = pl.ds(r0, C)
        gc_col = _dot(tri_l, g_s[rows, :], HIGHEST)
        g_row = neg_alog_col * _softplus(arow_ref[c] + dtb_col)
        gc_row = _dot(g_row, tri_u, HIGHEST)
        beta_all = b_s[rows, :]
        gcol = jnp.concatenate([gc_col[:, SMALL_A + h:SMALL_A + h + 1] for h in range(H)], axis=0)
        grow = jnp.concatenate([gc_row[h:h + 1, :] for h in range(H)], axis=1)
        beta = jnp.concatenate([beta_all[:, SMALL_B + h:SMALL_B + h + 1] for h in range(H)], axis=0)
        g_last = [gc_col[C - 1:C, SMALL_A + h:SMALL_A + h + 1] for h in range(H)]
        glast = jnp.concatenate([jnp.broadcast_to(g, (C, 1)) for g in g_last], axis=0)
        q = stack_rows(q_s, rows)
        k = stack_rows(k_s, rows)
        v = stack_rows(v_s, rows)
        decay = jnp.where(lower_bd, jnp.exp(jnp.where(lower_bd, gcol - grow, 0.0)), 0.0)
        eg = jnp.exp(gcol)
        kb = k * beta
        gram = _dot_nt(jnp.concatenate([kb, q], axis=0).astype(BF16), k.astype(BF16))
        a_mat = jnp.where(strict_bd, gram[0:HC] * decay, 0.0)
        qk = jnp.where(lower_bd, gram[HC:2 * HC] * decay, 0.0)
        pw = -a_mat
        x = pw
        pw16 = pw.astype(BF16)
        pw = _dot(pw16, pw16)
        for _ in range(4):
            pw16 = pw.astype(BF16)
            r = _dot(pw16, jnp.concatenate([x.astype(BF16), pw16], axis=1))
            x = x + pw + r[:, 0:HC]
            pw = r[:, HC:]
        x = x + pw + _dot(pw.astype(BF16), x.astype(BF16))
        rhs = jnp.concatenate([v * beta, kb * eg], axis=1)
        y = rhs + _dot(x.astype(BF16), rhs.astype(BF16))
        kd = k * jnp.exp(glast - gcol)
        u_s[c] = y[:, 0:dk]
        wq_s[c, 0:HC, :] = y[:, dk:2 * dk].astype(BF16)
        wq_s[c, HC:2 * HC, :] = (q * eg).astype(BF16)
        qk_s[c] = qk.astype(BF16)
        kdt_s[c] = kd.T.astype(BF16)
        gl_s[c, 0:1, :] = jnp.concatenate([jnp.broadcast_to(jnp.exp(g), (1, dk)) for g in g_last], axis=1)

    def prepare_pair(cp, carry):
        prepare(2 * cp)
        prepare(2 * cp + 1)
        return carry

    lax.fori_loop(0, tt // (2 * C), prepare_pair, 0)

    def head_blocks(x, row0):
        return jnp.concatenate([x[row0 + C * h:row0 + C * (h + 1), dk * h:dk * (h + 1)] for h in range(H)], axis=0)

    def scan(c, carry):
        r0 = pl.multiple_of(c * C, C)
        rows = pl.ds(r0, C)
        st = state_s[...]
        d1 = _dot(wq_s[c], st.astype(BF16))
        v_new = u_s[c] - head_blocks(d1, 0)
        v16 = v_new.astype(BF16)
        o = head_blocks(d1, HC) + _dot(qk_s[c], v16)
        v_bd = jnp.where(state_bd, jnp.concatenate([v16] * H, axis=1), jnp.zeros((), BF16))
        state_s[...] = st * gl_s[c, 0:1, :] + _dot(kdt_s[c], v_bd)
        for h in range(H):
            o_s[rows, dk * h:dk * (h + 1)] = o[C * h:C * (h + 1)]
        return carry

    lax.fori_loop(0, tt // C, scan, 0)

    nw = nw_ref[...]
    z = z_ref[...]
    for h in range(H):
        cols = slice(dk * h, dk * (h + 1))
        oh = o_s[:, cols]
        y = oh * lax.rsqrt(jnp.mean(oh * oh, axis=-1, keepdims=True) + NORM_EPS) * nw
        o_ref[:, cols] = (y * _silu(z[:, cols])).astype(BF16)


def _gated_delta_net(gdn, small, a_rows, conv_w, alog_row, dtb_row, alog_col, dtb_col, norm_w, tt=512):
    B, S, _ = gdn.shape
    nch = tt // GDN_CHUNK
    hc = C_HEADS * GDN_CHUNK
    part = lambda j: pl.BlockSpec((None, tt, C_QK), lambda b, i: (b, i, j))
    halo = lambda j: pl.BlockSpec((None, 8, C_QK), lambda b, i: (b, jnp.maximum(i * (tt // 8) - 1, 0), j))
    full = lambda a: pl.BlockSpec(a.shape, lambda b, i: (0,) * a.ndim)
    return pl.pallas_call(
        functools.partial(_gdn_kernel, tt=tt),
        grid=(B, S // tt),
        in_specs=[part(0), part(1), part(2), halo(0), halo(1), halo(2), part(3),
                  pl.BlockSpec((None, tt, W_SMALL), lambda b, i: (b, i, 0)),
                  pl.BlockSpec((None, nch, 8, GDN_CHUNK), lambda b, i: (b, i, 0, 0)),
                  full(conv_w), full(alog_row), full(dtb_row), full(alog_col), full(dtb_col), full(norm_w)],
        out_specs=pl.BlockSpec((None, tt, C_QK), lambda b, i: (b, i, 0)),
        out_shape=jax.ShapeDtypeStruct((B, S, C_QK), BF16),
        scratch_shapes=[pltpu.VMEM((tt, C_QK), F32), pltpu.VMEM((tt, C_QK), F32), pltpu.VMEM((tt, C_QK), F32),
                        pltpu.VMEM((tt, W_SMALL), F32), pltpu.VMEM((tt, W_SMALL), F32),
                        pltpu.VMEM((tt, C_QK), F32), pltpu.VMEM((C_HEAD_DIM, C_QK), F32),
                        pltpu.VMEM((nch, hc, C_HEAD_DIM), F32), pltpu.VMEM((nch, 2 * hc, C_HEAD_DIM), BF16),
                        pltpu.VMEM((nch, hc, hc), BF16), pltpu.VMEM((nch, C_HEAD_DIM, hc), BF16),
                        pltpu.VMEM((nch, 8, C_QK), F32)],
        compiler_params=_cparams("arbitrary", "arbitrary"),
        name="gated_delta_net",
    )(gdn, gdn, gdn, gdn, gdn, gdn, gdn, small, a_rows, conv_w, alog_row, dtb_row, alog_col, dtb_col, norm_w)


def _outproj_kernel(x_ref, oa_ref, ob_ref, oc_ref, w_ref, gt_ref, o_ref):
    y = _dot(oa_ref[...], w_ref[0:A_Q, :])
    y = y + _dot(ob_ref[...], w_ref[A_Q:A_Q + B_Q, :])
    y = y + _dot(oc_ref[...], w_ref[A_Q + B_Q:, :])
    o_ref[...] = x_ref[...] + gt_ref[...] * y


def _out_projection(x, o_a, o_b, o_c, w, mod, tm=512):
    B, S, D = x.shape
    row = lambda b, i: (b, i, 0)
    return pl.pallas_call(
        _outproj_kernel,
        grid=(B, S // tm),
        in_specs=[
            pl.BlockSpec((None, tm, D), row),
            pl.BlockSpec((None, tm, A_Q), row),
            pl.BlockSpec((None, tm, B_Q), row),
            pl.BlockSpec((None, tm, C_QK), row),
            pl.BlockSpec((D, D), lambda b, i: (0, 0)),
            pl.BlockSpec((None, 1, D), lambda b, i: (b, 0, 2)),
        ],
        out_specs=pl.BlockSpec((None, tm, D), row),
        out_shape=jax.ShapeDtypeStruct((B, S, D), F32),
        compiler_params=_cparams("arbitrary", "arbitrary"),
        name="out_projection",
    )(x, o_a, o_b, o_c, w, mod)


def _ffn_kernel(x_ref, g_ref, sc_ref, sh_ref, gt_ref, wg_ref, wu_ref, wd_ref, o_ref, h_s, acc_s):
    f = pl.program_id(2)

    @pl.when(f == 0)
    def _():
        h_s[...] = _rms_mod(x_ref[...], g_ref[...], sc_ref[...], sh_ref[...]).astype(BF16)
        acc_s[...] = jnp.zeros_like(acc_s)

    h = h_s[...]
    act = _silu(_dot(h, wg_ref[...])) * _dot(h, wu_ref[...])
    acc_s[...] += _dot(act.astype(BF16), wd_ref[...])

    @pl.when(f == pl.num_programs(2) - 1)
    def _():
        o_ref[...] = x_ref[...] + gt_ref[...] * acc_s[...]


def _ffn(x, gain, mod, w_gate_up, w_down, tm=512, tf=1408):
    B, S, D = x.shape
    nf = D_FF // tf
    row = lambda b, i, f: (b, i, 0)
    return pl.pallas_call(
        _ffn_kernel,
        grid=(B, S // tm, nf),
        in_specs=[
            pl.BlockSpec((None, tm, D), row),
            pl.BlockSpec((1, D), lambda b, i, f: (0, 0)),
            pl.BlockSpec((None, 1, D), lambda b, i, f: (b, 0, 4)),
            pl.BlockSpec((None, 1, D), lambda b, i, f: (b, 0, 3)),
            pl.BlockSpec((None, 1, D), lambda b, i, f: (b, 0, 5)),
            pl.BlockSpec((D, tf), lambda b, i, f: (0, f)),
            pl.BlockSpec((D, tf), lambda b, i, f: (0, nf + f)),
            pl.BlockSpec((tf, D), lambda b, i, f: (f, 0)),
        ],
        out_specs=pl.BlockSpec((None, tm, D), row),
        out_shape=jax.ShapeDtypeStruct((B, S, D), F32),
        scratch_shapes=[pltpu.VMEM((tm, D), BF16), pltpu.VMEM((tm, D), F32)],
        compiler_params=_cparams("arbitrary", "arbitrary", "arbitrary"),
        name="swiglu_ffn",
    )(x, gain, mod, mod, mod, w_gate_up, w_gate_up, w_down)


def _final_norm_kernel(x_ref, g_ref, o_ref):
    x = x_ref[...]
    o_ref[...] = x * lax.rsqrt(jnp.mean(x * x, axis=-1, keepdims=True) + NORM_EPS) * g_ref[...]


def _final_norm(x, gain, tm=512):
    B, S, D = x.shape
    return pl.pallas_call(
        _final_norm_kernel,
        grid=(B, S // tm),
        in_specs=[pl.BlockSpec((None, tm, D), lambda b, i: (b, i, 0)), pl.BlockSpec((1, D), lambda b, i: (0, 0))],
        out_specs=pl.BlockSpec((None, tm, D), lambda b, i: (b, i, 0)),
        out_shape=jax.ShapeDtypeStruct((B, S, D), F32),
        compiler_params=_cparams("arbitrary", "arbitrary"),
        name="final_norm",
    )(x, gain)


def _rope_tables(seq):
    inv = 1.0 / (ROPE_THETA ** (jnp.arange(0, HEAD_DIM, 2, dtype=F32) / HEAD_DIM))
    ang = jnp.arange(seq, dtype=F32)[:, None] * inv[None, :]
    cos, sin = jnp.cos(ang), jnp.sin(ang)
    zero = jnp.zeros_like(sin)
    cos_t = jnp.tile(cos, (1, 4))
    sa_t = jnp.tile(jnp.concatenate([-sin, zero], axis=1), (1, 2))
    sb_t = jnp.tile(jnp.concatenate([zero, sin], axis=1), (1, 2))
    return cos_t, sa_t, sb_t


def _permute_w_in(w_in):
    pts = np.cumsum(IN_SPLITS)[:-1].tolist()
    aq, ak, av, bq, bkc, bvc, bks, bvs, bkw, bvw, bg, cqkv, cz, ca, cb = jnp.split(w_in, pts, axis=-1)
    pad = jnp.zeros(w_in.shape[:-1] + (W_SMALL - B_GATES - 2 * C_HEADS,), w_in.dtype)
    return jnp.concatenate([aq, bq, ak, bks, bkw, av, bvs, bvw, bkc, bvc, bg, ca, cb, pad, cqkv, cz], axis=-1).astype(BF16)


def _compress_weights(k_w1, k_w2, v_w1, v_w2, pe_k, pe_v):
    L = k_w1.shape[0]
    half = CMP_STRIDE * HEAD_DIM

    def first_layer(lo):
        wk = k_w1[:, lo:lo + half].reshape(L, CMP_STRIDE, HEAD_DIM, CMP_HIDDEN)
        wv = v_w1[:, lo:lo + half].reshape(L, CMP_STRIDE, HEAD_DIM, CMP_HIDDEN)
        z = jnp.zeros_like(wk)
        w = jnp.concatenate([jnp.concatenate([wk, z], axis=-1), jnp.concatenate([z, wv], axis=-1)], axis=2)
        return w.reshape(L, 2 * half, 2 * CMP_HIDDEN).astype(BF16)

    def pe_rows(lo):
        return jnp.concatenate([pe_k[:, lo:lo + CMP_STRIDE], pe_v[:, lo:lo + CMP_STRIDE]], axis=-1).reshape(L, 1, 2 * half)

    z2 = jnp.zeros_like(k_w2)
    w2 = jnp.concatenate([jnp.concatenate([k_w2, z2], axis=-1), jnp.concatenate([z2, v_w2], axis=-1)], axis=1).astype(BF16)
    return pe_rows(0), pe_rows(CMP_STRIDE), first_layer(0), first_layer(half), w2


def _pool_matrix(n_cmp_rows, n_slc):
    ratio = SLC_LEN // CMP_STRIDE
    n = np.arange(n_cmp_rows)[:, None]
    j = np.arange(n_slc)[None, :]
    return jnp.asarray(((n // ratio == j) | (n == ratio * j - 1)).astype(np.float32))


def _lane_row(v, lane0):
    L, n = v.shape
    return jnp.zeros((L, 1, 128), F32).at[:, 0, lane0:lane0 + n].set(v)


def _sublane_col(v):
    L, n = v.shape
    return jnp.zeros((L, 8, 1), F32).at[:, :n, 0].set(v)


def kernel(x, c, norm_mix, norm_ffn, ada_w, ada_b, w_in, attn_sinks, cmp_k_w1, cmp_k_w2, cmp_v_w1, cmp_v_w2, cmp_pe_k, cmp_pe_v, gdn_conv_w, gdn_A_log, gdn_dt_bias, gdn_norm, w_out, w_gate_up, w_down, final_norm):
    B, S, D = x.shape
    L = w_in.shape[0]
    n16 = S // CMP_STRIDE
    n_slc = S // SLC_LEN

    cos_t, sa_t, sb_t = _rope_tables(S)
    mod = _adaln_mod(c, ada_w, ada_b)
    pea, peb, w1a, w1b, w2c = _compress_weights(cmp_k_w1, cmp_k_w2, cmp_v_w1, cmp_v_w2, cmp_pe_k, cmp_pe_v)
    pool_t = _pool_matrix(n16, n_slc).T
    per_layer = dict(
        mod=mod,
        norm_mix=norm_mix.reshape(L, 1, D), norm_ffn=norm_ffn.reshape(L, 1, D),
        w_in=_permute_w_in(w_in), sinks=attn_sinks,
        pea=pea, peb=peb, w1a=w1a, w1b=w1b, w2c=w2c,
        conv_w=gdn_conv_w,
        alog_row=_lane_row(gdn_A_log, SMALL_A), dtb_row=_lane_row(gdn_dt_bias, SMALL_A),
        alog_col=_sublane_col(gdn_A_log), dtb_col=_sublane_col(gdn_dt_bias),
        gdn_norm=gdn_norm.reshape(L, 1, C_HEAD_DIM),
        w_out=w_out.astype(BF16), w_gate_up=w_gate_up.astype(BF16), w_down=w_down.astype(BF16),
    )

    def layer(xc, p):
        rope, bq, vals, cmp_in, small, gdn = _in_projection(xc, p["norm_mix"], p["mod"], p["w_in"], cos_t, sa_t, sb_t)
        o_a = _swa_attention(rope, vals, p["sinks"])
        kv_cmp, kv_cmp_t = _compress(cmp_in.reshape(B, n16, CMP_STRIDE * W_CMP), p["pea"], p["peb"], p["w1a"], p["w1b"], p["w2c"])
        o_cmp_t, sel = _cmp_select(bq, kv_cmp, kv_cmp_t, pool_t)
        vals_t = vals[:, :, A_KV:].reshape(B, S // KV_BLOCK, KV_BLOCK, 2 * HEAD_DIM).transpose(0, 1, 3, 2)
        o_b = _nsa_attention(rope, vals_t, sel, o_cmp_t, small)
        a_rows = small[:, :, SMALL_A:SMALL_A + 8].reshape(B, S // GDN_CHUNK, GDN_CHUNK, 8).transpose(0, 1, 3, 2)
        o_c = _gated_delta_net(gdn, small, a_rows, p["conv_w"], p["alog_row"], p["dtb_row"], p["alog_col"], p["dtb_col"], p["gdn_norm"])
        xc = _out_projection(xc, o_a, o_b, o_c, p["w_out"], p["mod"])
        xc = _ffn(xc, p["norm_ffn"], p["mod"], p["w_gate_up"], p["w_down"])
        return xc, None

    x, _ = lax.scan(layer, x, per_layer)
    return _final_norm(x, final_norm.reshape(1, D))
```

```python
import functools
import math

import numpy as np
import jax
import jax.numpy as jnp
from jax import lax
from jax.experimental import pallas as pl
from jax.experimental.pallas import tpu as pltpu

F32 = jnp.float32
BF16 = jnp.bfloat16
HIGHEST = lax.Precision.HIGHEST

D_MODEL = 1024
DEPTH = 4
HEAD_DIM = 64
ATTN_SCALE = HEAD_DIM ** -0.5
ROPE_THETA = 10000.0
NEG_INF = -1e30
NORM_EPS = 1e-6

A_HEADS = 4
A_KV_HEADS = 2
A_WINDOW = 128

B_HEADS = 4
CMP_STRIDE = 16
CMP_LEN = 32
CMP_HIDDEN = 256
SLC_LEN = 64
SLC_TOPK = 16
NSA_WINDOW = 512
SLC_FORCED_SCORE = 1e9

C_HEAD_DIM = 128
C_HEADS = 4
CONV_WIDTH = 4
GDN_CHUNK = 64
C_QK = C_HEADS * C_HEAD_DIM

D_FF = 2816

A_Q = A_HEADS * HEAD_DIM
A_KV = A_KV_HEADS * HEAD_DIM
B_Q = B_HEADS * HEAD_DIM
B_KV = HEAD_DIM
B_GATES = 3 * B_HEADS
IN_SPLITS = (A_Q, A_KV, A_KV, B_Q, B_KV, B_KV, B_KV, B_KV, B_KV, B_KV, B_GATES, 3 * C_QK, C_QK, C_HEADS, C_HEADS)

W_ROPE = 768
W_V = 256
W_CMP = 128
W_SMALL = 128
W_GDN = 2048
W_TOTAL = W_ROPE + W_V + W_CMP + W_SMALL + W_GDN
SMALL_A = B_GATES
SMALL_B = B_GATES + C_HEADS

VMEM_LIMIT = 56 * 1024 * 1024
PREP_GROUP = 4

NT_DIMS = (((1,), (1,)), ((), ()))


def _cparams(*sem):
    return pltpu.CompilerParams(dimension_semantics=sem, vmem_limit_bytes=VMEM_LIMIT)


def _iota(shape, dim):
    return lax.broadcasted_iota(jnp.int32, shape, dim)


def _dot(a, b, precision=None):
    return jnp.dot(a, b, preferred_element_type=F32, precision=precision)


def _dot_nt(a, b):
    return lax.dot_general(a, b, NT_DIMS, preferred_element_type=F32)


def _silu(x):
    return x * jax.nn.sigmoid(x)


def _softplus(x):
    return jnp.maximum(x, 0.0) + jnp.log(1.0 + jnp.exp(-jnp.abs(x)))


def _rms_mod(x, gain, sc, sh):
    y = x * lax.rsqrt(jnp.mean(x * x, axis=-1, keepdims=True) + NORM_EPS)
    return (y * gain) * (1.0 + sc) + sh


def _mod_kernel(c_ref, w_ref, b_ref, o_ref):
    o_ref[...] = _dot(_silu(c_ref[...]), w_ref[...], HIGHEST) + b_ref[...]


def _adaln_mod(c, ada_w, ada_b):
    L, D, N = ada_w.shape
    tn = 1536
    c8 = jnp.zeros((8, D), F32).at[: c.shape[0]].set(c)
    out = pl.pallas_call(
        _mod_kernel,
        grid=(L, N // tn),
        in_specs=[
            pl.BlockSpec((8, D), lambda l, j: (0, 0)),
            pl.BlockSpec((None, D, tn), lambda l, j: (l, 0, j)),
            pl.BlockSpec((None, 1, tn), lambda l, j: (l, 0, j)),
        ],
        out_specs=pl.BlockSpec((None, 8, tn), lambda l, j: (l, 0, j)),
        out_shape=jax.ShapeDtypeStruct((L, 8, N), F32),
        compiler_params=_cparams("arbitrary", "arbitrary"),
        name="adaln_mod",
    )(c8, ada_w, ada_b.reshape(L, 1, N))
    return out.reshape(L, 8, 1, N)


def _inproj_kernel(x_ref, g_ref, sc_ref, sh_ref, w_ref, cos_ref, sa_ref, sb_ref,
                   rope_ref, bq_ref, v_ref, cmp_ref, small_ref, gdn_ref):
    h = _rms_mod(x_ref[...], g_ref[...], sc_ref[...], sh_ref[...]).astype(BF16)
    yr = _dot(h, w_ref[:, 0:W_ROPE])
    bq_ref[...] = yr[:, A_Q:A_Q + B_Q].astype(BF16)
    c, sa, sb = cos_ref[...], sa_ref[...], sb_ref[...]
    for g in range(W_ROPE // 128):
        xg = yr[:, 128 * g:128 * (g + 1)]
        rot = xg * c + pltpu.roll(xg, 96, 1) * sa + pltpu.roll(xg, 32, 1) * sb
        rope_ref[:, 128 * g:128 * (g + 1)] = rot.astype(BF16)
    o = W_ROPE
    yp = _dot(h, w_ref[:, o:o + W_V + W_CMP + W_SMALL])
    v_ref[...] = yp[:, 0:W_V].astype(BF16)
    cmp_ref[...] = yp[:, W_V:W_V + W_CMP]
    small_ref[...] = yp[:, W_V + W_CMP:W_V + W_CMP + W_SMALL]
    o += W_V + W_CMP + W_SMALL
    gdn_ref[...] = _dot(h, w_ref[:, o:o + W_GDN])


def _in_projection(x, gain, mod, w, cos_t, sa_t, sb_t, tm=512):
    B, S, D = x.shape
    row = lambda b, i: (b, i, 0)
    outs = pl.pallas_call(
        _inproj_kernel,
        grid=(B, S // tm),
        in_specs=[
            pl.BlockSpec((None, tm, D), row),
            pl.BlockSpec((1, D), lambda b, i: (0, 0)),
            pl.BlockSpec((None, 1, D), lambda b, i: (b, 0, 1)),
            pl.BlockSpec((None, 1, D), lambda b, i: (b, 0, 0)),
            pl.BlockSpec((D, W_TOTAL), lambda b, i: (0, 0)),
            pl.BlockSpec((tm, 128), lambda b, i: (i, 0)),
            pl.BlockSpec((tm, 128), lambda b, i: (i, 0)),
            pl.BlockSpec((tm, 128), lambda b, i: (i, 0)),
        ],
        out_specs=[
            pl.BlockSpec((None, tm, W_ROPE), row),
            pl.BlockSpec((None, tm, B_Q), row),
            pl.BlockSpec((None, tm, W_V), row),
            pl.BlockSpec((None, tm, W_CMP), row),
            pl.BlockSpec((None, tm, W_SMALL), row),
            pl.BlockSpec((None, tm, W_GDN), row),
        ],
        out_shape=[
            jax.ShapeDtypeStruct((B, S, W_ROPE), BF16),
            jax.ShapeDtypeStruct((B, S, B_Q), BF16),
            jax.ShapeDtypeStruct((B, S, W_V), BF16),
            jax.ShapeDtypeStruct((B, S, W_CMP), F32),
            jax.ShapeDtypeStruct((B, S, W_SMALL), F32),
            jax.ShapeDtypeStruct((B, S, W_GDN), F32),
        ],
        compiler_params=_cparams("arbitrary", "arbitrary"),
        name="in_projection",
    )(x, gain, mod, mod, w, cos_t, sa_t, sb_t)
    return outs


def _swa_kernel(sink_ref, q_ref, k_ref, v_ref, o_ref, *, tq, window):
    q0 = pl.program_id(1) * tq
    span = tq + window
    start = pl.multiple_of(jnp.maximum(q0 - window, 0), 128)
    kk = k_ref[pl.ds(start, span), :]
    vv = v_ref[pl.ds(start, span), :]
    qpos = q0 + _iota((tq, span), 0)
    kpos = start + _iota((tq, span), 1)
    mask = (kpos <= qpos) & (kpos > qpos - window)
    q = q_ref[...]
    group = A_HEADS // A_KV_HEADS
    outs = []
    for h in range(A_HEADS):
        g = h // group
        s = _dot_nt(q[:, HEAD_DIM * h:HEAD_DIM * (h + 1)], kk[:, HEAD_DIM * g:HEAD_DIM * (g + 1)]) * ATTN_SCALE
        s = jnp.where(mask, s, NEG_INF)
        sink = sink_ref[h]
        m = jnp.maximum(jnp.max(s, axis=-1, keepdims=True), sink)
        p = jnp.exp(s - m)
        den = jnp.sum(p, axis=-1, keepdims=True) + jnp.exp(sink - m)
        outs.append(_dot(p.astype(BF16), vv[:, HEAD_DIM * g:HEAD_DIM * (g + 1)]) / den)
    o_ref[...] = jnp.concatenate(outs, axis=1).astype(BF16)


def _swa_attention(rope, vals, sinks, tq=256):
    B, S, _ = rope.shape
    return pl.pallas_call(
        functools.partial(_swa_kernel, tq=tq, window=A_WINDOW),
        grid=(B, S // tq),
        in_specs=[
            pl.BlockSpec(memory_space=pltpu.SMEM),
            pl.BlockSpec((None, tq, A_Q), lambda b, i: (b, i, 0)),
            pl.BlockSpec((None, S, A_KV), lambda b, i: (b, 0, (A_Q + B_Q) // A_KV)),
            pl.BlockSpec((None, S, A_KV), lambda b, i: (b, 0, 0)),
        ],
        out_specs=pl.BlockSpec((None, tq, A_Q), lambda b, i: (b, i, 0)),
        out_shape=jax.ShapeDtypeStruct((B, S, A_Q), BF16),
        compiler_params=_cparams("arbitrary", "arbitrary"),
        name="swa_attention",
    )(sinks, rope, rope, vals)


def _compress_kernel(x_ref, pea_ref, peb_ref, w1a_ref, w1b_ref, w2_ref, o_ref, ot_ref):
    x = x_ref[...]
    n = x.shape[0]
    a = _dot((x + pea_ref[...]).astype(BF16), w1a_ref[...])
    b = _dot((x + peb_ref[...]).astype(BF16), w1b_ref[...])
    hid = _silu(a + pltpu.roll(b, n - 1, 0))
    out = _dot(hid.astype(BF16), w2_ref[...])
    o_ref[...] = out.astype(BF16)
    ot_ref[...] = out.T.astype(BF16)


def _compress(xc, pea, peb, w1a, w1b, w2):
    B, n, K = xc.shape
    full = lambda a: pl.BlockSpec(a.shape, lambda b: (0,) * a.ndim)
    return pl.pallas_call(
        _compress_kernel,
        grid=(B,),
        in_specs=[pl.BlockSpec((None, n, K), lambda b: (b, 0, 0)), full(pea), full(peb), full(w1a), full(w1b), full(w2)],
        out_specs=[pl.BlockSpec((None, n, 2 * HEAD_DIM), lambda b: (b, 0, 0)),
                   pl.BlockSpec((None, 2 * HEAD_DIM, n), lambda b: (b, 0, 0))],
        out_shape=[jax.ShapeDtypeStruct((B, n, 2 * HEAD_DIM), BF16), jax.ShapeDtypeStruct((B, 2 * HEAD_DIM, n), BF16)],
        compiler_params=_cparams("arbitrary"),
        name="nsa_compress",
    )(xc, pea, peb, w1a, w1b, w2)


def _heads_on_lanes(q):
    qt = q.astype(F32).T * ATTN_SCALE
    return jnp.concatenate([qt[HEAD_DIM * h:HEAD_DIM * (h + 1)] for h in range(B_HEADS)], axis=1).astype(BF16)


def _heads_on_sublanes(ot, tq):
    return jnp.concatenate([ot[:, tq * h:tq * (h + 1)] for h in range(B_HEADS)], axis=0)


def _cmp_select_kernel(q_ref, k_ref, vt_ref, pool_ref, o_ref, sel_ref, *, tq, top_k):
    q0 = pl.program_id(1) * tq
    H = B_HEADS
    n_cmp = k_ref.shape[0]
    n_slc = pool_ref.shape[0]
    qt = _heads_on_lanes(q_ref[...])
    s = _dot(k_ref[:, 0:HEAD_DIM], qt)
    valid = _iota((n_cmp, tq), 0) * CMP_STRIDE + (CMP_LEN - 1) <= q0 + _iota((n_cmp, tq), 1)
    valid = jnp.concatenate([valid] * H, axis=1)
    s = jnp.where(valid, s, NEG_INF)
    e = jnp.where(valid, jnp.exp(s - jnp.max(s, axis=0, keepdims=True)), 0.0)
    den = jnp.sum(e, axis=0, keepdims=True)
    p = e * (1.0 / jnp.where(den > 0.0, den, 1.0))
    o_t = _dot(vt_ref[HEAD_DIM:2 * HEAD_DIM, :], p.astype(BF16))
    o_ref[...] = _heads_on_sublanes(o_t, tq)
    psum = p[:, 0:tq]
    for h in range(1, H):
        psum = psum + p[:, tq * h:tq * (h + 1)]

    imp = _dot(pool_ref[...], psum, HIGHEST)
    blk = _iota((n_slc, tq), 0)
    tt = q0 + _iota((n_slc, tq), 1)
    cur = tt >> 6
    forced = (blk == 0) | (blk == cur) | (blk == cur - 1)
    causal = blk * SLC_LEN <= tt
    score = jnp.where(forced, SLC_FORCED_SCORE, jnp.where(causal, imp, NEG_INF))
    sel = jnp.zeros((n_slc, tq), F32)
    blk_f = blk.astype(F32)
    for _ in range(top_k):
        mx = jnp.max(score, axis=0, keepdims=True)
        first = jnp.min(jnp.where(score == mx, blk_f, float(n_slc)), axis=0, keepdims=True)
        pick = blk_f == first
        sel = jnp.where(pick, 1.0, sel)
        score = jnp.where(pick, -jnp.inf, score)
    sel_ref[...] = jnp.where(causal, sel, 0.0)


def _cmp_select(bq, kv_cmp, kv_cmp_t, pool_t, tq=256):
    B, S, _ = bq.shape
    n_slc, n_cmp = pool_t.shape
    top_k = min(SLC_TOPK, n_slc)
    return pl.pallas_call(
        functools.partial(_cmp_select_kernel, tq=tq, top_k=top_k),
        grid=(B, S // tq),
        in_specs=[
            pl.BlockSpec((None, tq, B_Q), lambda b, i: (b, i, 0)),
            pl.BlockSpec((None, n_cmp, 2 * HEAD_DIM), lambda b, i: (b, 0, 0)),
            pl.BlockSpec((None, 2 * HEAD_DIM, n_cmp), lambda b, i: (b, 0, 0)),
            pl.BlockSpec((n_slc, n_cmp), lambda b, i: (0, 0)),
        ],
        out_specs=[
            pl.BlockSpec((None, B_Q, tq), lambda b, i: (b, 0, i)),
            pl.BlockSpec((None, n_slc, tq), lambda b, i: (b, 0, i)),
        ],
        out_shape=[jax.ShapeDtypeStruct((B, B_Q, S), F32), jax.ShapeDtypeStruct((B, n_slc, S), F32)],
        compiler_params=_cparams("arbitrary", "arbitrary"),
        name="nsa_cmp_select",
    )(bq, kv_cmp, kv_cmp_t, pool_t)


KV_BLOCK = 128


def _nsa_attn_kernel(q_ref, k_ref, vt_ref, sel_ref, ocmp_ref, small_ref, o_ref, *, tq, tk, window):
    i = pl.program_id(1)
    q0 = i * tq
    H = B_HEADS
    qt = _heads_on_lanes(q_ref[...])
    blocks_per_tile = tk // SLC_LEN
    vblocks_per_tile = tk // KV_BLOCK
    qpos = q0 + _iota((tk, tq), 1)

    def tile(j, carry):
        m, l, acc = carry
        k0 = pl.multiple_of(j * tk, tk)
        s = _dot(k_ref[pl.ds(k0, tk), 0:HEAD_DIM], qt)
        sel = sel_ref[pl.ds(pl.multiple_of(j * blocks_per_tile, blocks_per_tile), blocks_per_tile), :]
        picked = jnp.concatenate(
            [jnp.broadcast_to(sel[r:r + 1, :], (SLC_LEN, tq)) for r in range(blocks_per_tile)], axis=0)
        ok = (picked > 0.5) & (k0 + _iota((tk, tq), 0) <= qpos)
        bias = jnp.where(ok, 0.0, NEG_INF)
        s = s + jnp.concatenate([bias] * H, axis=1)
        m_new = jnp.maximum(m, jnp.max(s, axis=0, keepdims=True))
        alpha = jnp.exp(m - m_new)
        p = jnp.exp(s - m_new)
        l = l * alpha + jnp.sum(p, axis=0, keepdims=True)
        vb = vt_ref[pl.ds(pl.multiple_of(j * vblocks_per_tile, vblocks_per_tile), vblocks_per_tile)]
        vt = jnp.concatenate([vb[r, 0:HEAD_DIM, :] for r in range(vblocks_per_tile)], axis=1)
        acc = acc * alpha + _dot(vt, p.astype(BF16))
        return m_new, l, acc

    def tile_pair(jp, carry):
        return tile(2 * jp + 1, tile(2 * jp, carry))

    n_pairs = (q0 + tq + 2 * tk - 1) // (2 * tk)
    init = (jnp.full((1, H * tq), NEG_INF, F32), jnp.zeros((1, H * tq), F32), jnp.zeros((HEAD_DIM, H * tq), F32))
    _, l, acc = lax.fori_loop(0, n_pairs, tile_pair, init)
    o_slc = _heads_on_sublanes(acc * (1.0 / l), tq)

    span = tq + window
    start = pl.multiple_of(jnp.maximum(q0 - window, 0), KV_BLOCK)
    s = _dot(k_ref[pl.ds(start, span), HEAD_DIM:2 * HEAD_DIM], qt)
    kpos = start + _iota((span, tq), 0)
    qp = q0 + _iota((span, tq), 1)
    bias = jnp.where((kpos <= qp) & (kpos > qp - window), 0.0, NEG_INF)
    s = s + jnp.concatenate([bias] * H, axis=1)
    p = jnp.exp(s - jnp.max(s, axis=0, keepdims=True))
    vb = vt_ref[pl.ds(start // KV_BLOCK, span // KV_BLOCK)]
    vt = jnp.concatenate([vb[r, HEAD_DIM:2 * HEAD_DIM, :] for r in range(span // KV_BLOCK)], axis=1)
    o_win = _heads_on_sublanes(_dot(vt, p.astype(BF16)) * (1.0 / jnp.sum(p, axis=0, keepdims=True)), tq)

    gates = jax.nn.sigmoid(small_ref[...]).T
    o_cmp = ocmp_ref[...]
    outs = []
    for h in range(H):
        rows = slice(HEAD_DIM * h, HEAD_DIM * (h + 1))
        outs.append(gates[h:h + 1, :] * o_cmp[rows] + gates[H + h:H + h + 1, :] * o_slc[rows]
                    + gates[2 * H + h:2 * H + h + 1, :] * o_win[rows])
    o_ref[...] = jnp.concatenate(outs, axis=0).T.astype(BF16)


def _nsa_attention(rope, vals_t, sel, o_cmp, small, tq=128, tk=512):
    B, S, _ = rope.shape
    n_slc = sel.shape[1]
    return pl.pallas_call(
        functools.partial(_nsa_attn_kernel, tq=tq, tk=tk, window=NSA_WINDOW),
        grid=(B, S // tq),
        in_specs=[
            pl.BlockSpec((None, tq, B_Q), lambda b, i: (b, i, A_Q // B_Q)),
            pl.BlockSpec((None, S, 128), lambda b, i: (b, 0, (A_Q + B_Q + A_KV) // 128)),
            pl.BlockSpec((None, S // KV_BLOCK, 128, KV_BLOCK), lambda b, i: (b, 0, 0, 0)),
            pl.BlockSpec((None, n_slc, tq), lambda b, i: (b, 0, i)),
            pl.BlockSpec((None, B_Q, tq), lambda b, i: (b, 0, i)),
            pl.BlockSpec((None, tq, W_SMALL), lambda b, i: (b, i, 0)),
        ],
        out_specs=pl.BlockSpec((None, tq, B_Q), lambda b, i: (b, i, 0)),
        out_shape=jax.ShapeDtypeStruct((B, S, B_Q), BF16),
        compiler_params=_cparams("arbitrary", "arbitrary"),
        name="nsa_attention",
    )(rope, rope, vals_t, sel, o_cmp, small)


def _gdn_kernel(xq_ref, xk_ref, xv_ref, hq_ref, hk_ref, hv_ref, z_ref, small_ref, arow_ref,
                cw_ref, alog_row_ref, dtb_row_ref, alog_col_ref, dtb_col_ref, nw_ref,
                o_ref, q_s, k_s, v_s, g_s, b_s, o_s, state_s, u_s, wq_s, qk_s, kdt_s, gl_s, pq_s, pk_s, pv_s, *, tt):
    i = pl.program_id(1)
    C = GDN_CHUNK
    dk = C_HEAD_DIM
    H = C_HEADS

    @pl.when(i == 0)
    def _():
        state_s[...] = jnp.zeros_like(state_s)

    def conv(x_ref, halo_ref, pad_s, col0):
        pad_s[0:8, :] = jnp.where(i > 0, halo_ref[...], 0.0)
        pad_s[8:8 + tt, :] = x_ref[...]
        w = cw_ref[:, col0:col0 + C_QK]
        y = x_ref[...] * w[3:4]
        for kback in range(1, CONV_WIDTH):
            y = y + pad_s[8 - kback:8 - kback + tt, :] * w[3 - kback:4 - kback]
        return _silu(y)

    qf = conv(xq_ref, hq_ref, pq_s, 0)
    kf = conv(xk_ref, hk_ref, pk_s, C_QK)
    v_s[...] = conv(xv_ref, hv_ref, pv_s, 2 * C_QK)
    for h in range(H):
        cols = slice(dk * h, dk * (h + 1))
        qh = qf[:, cols]
        kh = kf[:, cols]
        q_s[:, cols] = qh * lax.rsqrt(jnp.sum(qh * qh, axis=-1, keepdims=True) + NORM_EPS) * (dk ** -0.5)
        k_s[:, cols] = kh * lax.rsqrt(jnp.sum(kh * kh, axis=-1, keepdims=True) + NORM_EPS)
    small = small_ref[...]
    g_s[...] = -jnp.exp(alog_row_ref[...]) * _softplus(small + dtb_row_ref[...])
    b_s[...] = jax.nn.sigmoid(small)

    r_i = _iota((C, C), 0)
    c_i = _iota((C, C), 1)
    lower = r_i >= c_i
    strict = r_i > c_i
    tri_l = lower.astype(F32)
    tri_u = (r_i <= c_i).astype(F32)
    neg_alog_col = -jnp.exp(alog_col_ref[...])
    dtb_col = dtb_col_ref[...]

    HC = H * C
    rr = _iota((HC, HC), 0)
    cc = _iota((HC, HC), 1)
    same_head = (rr >> 6) == (cc >> 6)
    lower_bd = same_head & (rr >= cc)
    strict_bd = same_head & (rr > cc)
    state_bd = (_iota((HC, H * dk), 0) >> 6) == (_iota((HC, H * dk), 1) >> 7)

    def stack_rows(ref, rows):
        return jnp.concatenate([ref[rows, dk * h:dk * (h + 1)] for h in range(H)], axis=0)

    def prepare(c):
        r0 = pl.multiple_of(c * C, C)
        rows = pl.ds(r0, C)
        gc_col = _dot(tri_l, g_s[rows, :], HIGHEST)
        g_row = neg_alog_col * _softplus(arow_ref[c] + dtb_col)
        gc_row = _dot(g_row, tri_u, HIGHEST)
        beta_all = b_s[rows, :]
        gcol = jnp.concatenate([gc_col[:, SMALL_A + h:SMALL_A + h + 1] for h in range(H)], axis=0)
        grow = jnp.concatenate([gc_row[h:h + 1, :] for h in range(H)], axis=1)
        beta = jnp.concatenate([beta_all[:, SMALL_B + h:SMALL_B + h + 1] for h in range(H)], axis=0)
        g_last = [gc_col[C - 1:C, SMALL_A + h:SMALL_A + h + 1] for h in range(H)]
        glast = jnp.concatenate([jnp.broadcast_to(g, (C, 1)) for g in g_last], axis=0)
        q = stack_rows(q_s, rows)
        k = stack_rows(k_s, rows)
        v = stack_rows(v_s, rows)
        decay = jnp.where(lower_bd, jnp.exp(jnp.where(lower_bd, gcol - grow, 0.0)), 0.0)
        eg = jnp.exp(gcol)
        kb = k * beta
        gram = _dot_nt(jnp.concatenate([kb, q], axis=0).astype(BF16), k.astype(BF16))
        a_mat = jnp.where(strict_bd, gram[0:HC] * decay, 0.0)
        qk = jnp.where(lower_bd, gram[HC:2 * HC] * decay, 0.0)
        pw = -a_mat
        x = pw
        pw16 = pw.astype(BF16)
        pw = _dot(pw16, pw16)
        for _ in range(4):
            pw16 = pw.astype(BF16)
            r = _dot(pw16, jnp.concatenate([x.astype(BF16), pw16], axis=1))
            x = x + pw + r[:, 0:HC]
            pw = r[:, HC:]
        x = x + pw + _dot(pw.astype(BF16), x.astype(BF16))
        rhs = jnp.concatenate([v * beta, kb * eg], axis=1)
        y = rhs + _dot(x.astype(BF16), rhs.astype(BF16))
        kd = k * jnp.exp(glast - gcol)
        u_s[c] = y[:, 0:dk]
        wq_s[c, 0:HC, :] = y[:, dk:2 * dk].astype(BF16)
        wq_s[c, HC:2 * HC, :] = (q * eg).astype(BF16)
        qk_s[c] = qk.astype(BF16)
        kdt_s[c] = kd.T.astype(BF16)
        gl_s[c, 0:1, :] = jnp.concatenate([jnp.broadcast_to(jnp.exp(g), (1, dk)) for g in g_last], axis=1)

    def prepare_group(cg, carry):
        for r in range(PREP_GROUP):
            prepare(PREP_GROUP * cg + r)
        return carry

    lax.fori_loop(0, tt // (PREP_GROUP * C), prepare_group, 0)

    def head_blocks(x, row0):
        return jnp.concatenate([x[row0 + C * h:row0 + C * (h + 1), dk * h:dk * (h + 1)] for h in range(H)], axis=0)

    def scan(c, carry):
        r0 = pl.multiple_of(c * C, C)
        rows = pl.ds(r0, C)
        st = state_s[...]
        d1 = _dot(wq_s[c], st.astype(BF16))
        v_new = u_s[c] - head_blocks(d1, 0)
        v16 = v_new.astype(BF16)
        o = head_blocks(d1, HC) + _dot(qk_s[c], v16)
        v_bd = jnp.where(state_bd, jnp.concatenate([v16] * H, axis=1), jnp.zeros((), BF16))
        state_s[...] = st * gl_s[c, 0:1, :] + _dot(kdt_s[c], v_bd)
        for h in range(H):
            o_s[rows, dk * h:dk * (h + 1)] = o[C * h:C * (h + 1)]
        return carry

    lax.fori_loop(0, tt // C, scan, 0)

    nw = nw_ref[...]
    z = z_ref[...]
    for h in range(H):
        cols = slice(dk * h, dk * (h + 1))
        oh = o_s[:, cols]
        y = oh * lax.rsqrt(jnp.mean(oh * oh, axis=-1, keepdims=True) + NORM_EPS) * nw
        o_ref[:, cols] = (y * _silu(z[:, cols])).astype(BF16)


def _gated_delta_net(gdn, small, a_rows, conv_w, alog_row, dtb_row, alog_col, dtb_col, norm_w, tt=512):
    B, S, _ = gdn.shape
    nch = tt // GDN_CHUNK
    hc = C_HEADS * GDN_CHUNK
    part = lambda j: pl.BlockSpec((None, tt, C_QK), lambda b, i: (b, i, j))
    halo = lambda j: pl.BlockSpec((None, 8, C_QK), lambda b, i: (b, jnp.maximum(i * (tt // 8) - 1, 0), j))
    full = lambda a: pl.BlockSpec(a.shape, lambda b, i: (0,) * a.ndim)
    return pl.pallas_call(
        functools.partial(_gdn_kernel, tt=tt),
        grid=(B, S // tt),
        in_specs=[part(0), part(1), part(2), halo(0), halo(1), halo(2), part(3),
                  pl.BlockSpec((None, tt, W_SMALL), lambda b, i: (b, i, 0)),
                  pl.BlockSpec((None, nch, 8, GDN_CHUNK), lambda b, i: (b, i, 0, 0)),
                  full(conv_w), full(alog_row), full(dtb_row), full(alog_col), full(dtb_col), full(norm_w)],
        out_specs=pl.BlockSpec((None, tt, C_QK), lambda b, i: (b, i, 0)),
        out_shape=jax.ShapeDtypeStruct((B, S, C_QK), BF16),
        scratch_shapes=[pltpu.VMEM((tt, C_QK), F32), pltpu.VMEM((tt, C_QK), F32), pltpu.VMEM((tt, C_QK), F32),
                        pltpu.VMEM((tt, W_SMALL), F32), pltpu.VMEM((tt, W_SMALL), F32),
                        pltpu.VMEM((tt, C_QK), F32), pltpu.VMEM((C_HEAD_DIM, C_QK), F32),
                        pltpu.VMEM((nch, hc, C_HEAD_DIM), F32), pltpu.VMEM((nch, 2 * hc, C_HEAD_DIM), BF16),
                        pltpu.VMEM((nch, hc, hc), BF16), pltpu.VMEM((nch, C_HEAD_DIM, hc), BF16),
                        pltpu.VMEM((nch, 8, C_QK), F32)] + [pltpu.VMEM((tt + 8, C_QK), F32)] * 3,
        compiler_params=_cparams("arbitrary", "arbitrary"),
        name="gated_delta_net",
    )(gdn, gdn, gdn, gdn, gdn, gdn, gdn, small, a_rows, conv_w, alog_row, dtb_row, alog_col, dtb_col, norm_w)


def _outproj_kernel(x_ref, oa_ref, ob_ref, oc_ref, w_ref, gt_ref, o_ref):
    y = _dot(oa_ref[...], w_ref[0:A_Q, :])
    y = y + _dot(ob_ref[...], w_ref[A_Q:A_Q + B_Q, :])
    y = y + _dot(oc_ref[...], w_ref[A_Q + B_Q:, :])
    o_ref[...] = x_ref[...] + gt_ref[...] * y


def _out_projection(x, o_a, o_b, o_c, w, mod, tm=1024):
    B, S, D = x.shape
    row = lambda b, i: (b, i, 0)
    return pl.pallas_call(
        _outproj_kernel,
        grid=(B, S // tm),
        in_specs=[
            pl.BlockSpec((None, tm, D), row),
            pl.BlockSpec((None, tm, A_Q), row),
            pl.BlockSpec((None, tm, B_Q), row),
            pl.BlockSpec((None, tm, C_QK), row),
            pl.BlockSpec((D, D), lambda b, i: (0, 0)),
            pl.BlockSpec((None, 1, D), lambda b, i: (b, 0, 2)),
        ],
        out_specs=pl.BlockSpec((None, tm, D), row),
        out_shape=jax.ShapeDtypeStruct((B, S, D), F32),
        compiler_params=_cparams("arbitrary", "arbitrary"),
        name="out_projection",
    )(x, o_a, o_b, o_c, w, mod)


def _ffn_kernel(x_ref, g_ref, sc_ref, sh_ref, gt_ref, wg_ref, wu_ref, wd_ref, o_ref, h_s, acc_s):
    f = pl.program_id(2)

    @pl.when(f == 0)
    def _():
        h_s[...] = _rms_mod(x_ref[...], g_ref[...], sc_ref[...], sh_ref[...]).astype(BF16)
        acc_s[...] = jnp.zeros_like(acc_s)

    h = h_s[...]
    act = _silu(_dot(h, wg_ref[...])) * _dot(h, wu_ref[...])
    acc_s[...] += _dot(act.astype(BF16), wd_ref[...])

    @pl.when(f == pl.num_programs(2) - 1)
    def _():
        o_ref[...] = x_ref[...] + gt_ref[...] * acc_s[...]


def _ffn(x, gain, mod, w_gate_up, w_down, tm=512, tf=1408):
    B, S, D = x.shape
    nf = D_FF // tf
    row = lambda b, i, f: (b, i, 0)
    return pl.pallas_call(
        _ffn_kernel,
        grid=(B, S // tm, nf),
        in_specs=[
            pl.BlockSpec((None, tm, D), row),
            pl.BlockSpec((1, D), lambda b, i, f: (0, 0)),
            pl.BlockSpec((None, 1, D), lambda b, i, f: (b, 0, 4)),
            pl.BlockSpec((None, 1, D), lambda b, i, f: (b, 0, 3)),
            pl.BlockSpec((None, 1, D), lambda b, i, f: (b, 0, 5)),
            pl.BlockSpec((D, tf), lambda b, i, f: (0, f)),
            pl.BlockSpec((D, tf), lambda b, i, f: (0, nf + f)),
            pl.BlockSpec((tf, D), lambda b, i, f: (f, 0)),
        ],
        out_specs=pl.BlockSpec((None, tm, D), row),
        out_shape=jax.ShapeDtypeStruct((B, S, D), F32),
        scratch_shapes=[pltpu.VMEM((tm, D), BF16), pltpu.VMEM((tm, D), F32)],
        compiler_params=_cparams("arbitrary", "arbitrary", "arbitrary"),
        name="swiglu_ffn",
    )(x, gain, mod, mod, mod, w_gate_up, w_gate_up, w_down)


def _final_norm_kernel(x_ref, g_ref, o_ref):
    x = x_ref[...]
    o_ref[...] = x * lax.rsqrt(jnp.mean(x * x, axis=-1, keepdims=True) + NORM_EPS) * g_ref[...]


def _final_norm(x, gain, tm=512):
    B, S, D = x.shape
    return pl.pallas_call(
        _final_norm_kernel,
        grid=(B, S // tm),
        in_specs=[pl.BlockSpec((None, tm, D), lambda b, i: (b, i, 0)), pl.BlockSpec((1, D), lambda b, i: (0, 0))],
        out_specs=pl.BlockSpec((None, tm, D), lambda b, i: (b, i, 0)),
        out_shape=jax.ShapeDtypeStruct((B, S, D), F32),
        compiler_params=_cparams("arbitrary", "arbitrary"),
        name="final_norm",
    )(x, gain)


def _rope_tables(seq):
    inv = 1.0 / (ROPE_THETA ** (jnp.arange(0, HEAD_DIM, 2, dtype=F32) / HEAD_DIM))
    ang = jnp.arange(seq, dtype=F32)[:, None] * inv[None, :]
    cos, sin = jnp.cos(ang), jnp.sin(ang)
    zero = jnp.zeros_like(sin)
    cos_t = jnp.tile(cos, (1, 4))
    sa_t = jnp.tile(jnp.concatenate([-sin, zero], axis=1), (1, 2))
    sb_t = jnp.tile(jnp.concatenate([zero, sin], axis=1), (1, 2))
    return cos_t, sa_t, sb_t


def _permute_w_in(w_in):
    pts = np.cumsum(IN_SPLITS)[:-1].tolist()
    aq, ak, av, bq, bkc, bvc, bks, bvs, bkw, bvw, bg, cqkv, cz, ca, cb = jnp.split(w_in, pts, axis=-1)
    pad = jnp.zeros(w_in.shape[:-1] + (W_SMALL - B_GATES - 2 * C_HEADS,), w_in.dtype)
    return jnp.concatenate([aq, bq, ak, bks, bkw, av, bvs, bvw, bkc, bvc, bg, ca, cb, pad, cqkv, cz], axis=-1).astype(BF16)


def _compress_weights(k_w1, k_w2, v_w1, v_w2, pe_k, pe_v):
    L = k_w1.shape[0]
    half = CMP_STRIDE * HEAD_DIM

    def first_layer(lo):
        wk = k_w1[:, lo:lo + half].reshape(L, CMP_STRIDE, HEAD_DIM, CMP_HIDDEN)
        wv = v_w1[:, lo:lo + half].reshape(L, CMP_STRIDE, HEAD_DIM, CMP_HIDDEN)
        z = jnp.zeros_like(wk)
        w = jnp.concatenate([jnp.concatenate([wk, z], axis=-1), jnp.concatenate([z, wv], axis=-1)], axis=2)
        return w.reshape(L, 2 * half, 2 * CMP_HIDDEN).astype(BF16)

    def pe_rows(lo):
        return jnp.concatenate([pe_k[:, lo:lo + CMP_STRIDE], pe_v[:, lo:lo + CMP_STRIDE]], axis=-1).reshape(L, 1, 2 * half)

    z2 = jnp.zeros_like(k_w2)
    w2 = jnp.concatenate([jnp.concatenate([k_w2, z2], axis=-1), jnp.concatenate([z2, v_w2], axis=-1)], axis=1).astype(BF16)
    return pe_rows(0), pe_rows(CMP_STRIDE), first_layer(0), first_layer(half), w2


def _pool_matrix(n_cmp_rows, n_slc):
    ratio = SLC_LEN // CMP_STRIDE
    n = np.arange(n_cmp_rows)[:, None]
    j = np.arange(n_slc)[None, :]
    return jnp.asarray(((n // ratio == j) | (n == ratio * j - 1)).astype(np.float32))


def _lane_row(v, lane0):
    L, n = v.shape
    return jnp.zeros((L, 1, 128), F32).at[:, 0, lane0:lane0 + n].set(v)


def _sublane_col(v):
    L, n = v.shape
    return jnp.zeros((L, 8, 1), F32).at[:, :n, 0].set(v)


def kernel(x, c, norm_mix, norm_ffn, ada_w, ada_b, w_in, attn_sinks, cmp_k_w1, cmp_k_w2, cmp_v_w1, cmp_v_w2, cmp_pe_k, cmp_pe_v, gdn_conv_w, gdn_A_log, gdn_dt_bias, gdn_norm, w_out, w_gate_up, w_down, final_norm):
    B, S, D = x.shape
    L = w_in.shape[0]
    n16 = S // CMP_STRIDE
    n_slc = S // SLC_LEN

    cos_t, sa_t, sb_t = _rope_tables(S)
    mod = _adaln_mod(c, ada_w, ada_b)
    pea, peb, w1a, w1b, w2c = _compress_weights(cmp_k_w1, cmp_k_w2, cmp_v_w1, cmp_v_w2, cmp_pe_k, cmp_pe_v)
    pool_t = _pool_matrix(n16, n_slc).T
    per_layer = dict(
        mod=mod,
        norm_mix=norm_mix.reshape(L, 1, D), norm_ffn=norm_ffn.reshape(L, 1, D),
        w_in=_permute_w_in(w_in), sinks=attn_sinks,
        pea=pea, peb=peb, w1a=w1a, w1b=w1b, w2c=w2c,
        conv_w=gdn_conv_w,
        alog_row=_lane_row(gdn_A_log, SMALL_A), dtb_row=_lane_row(gdn_dt_bias, SMALL_A),
        alog_col=_sublane_col(gdn_A_log), dtb_col=_sublane_col(gdn_dt_bias),
        gdn_norm=gdn_norm.reshape(L, 1, C_HEAD_DIM),
        w_out=w_out.astype(BF16), w_gate_up=w_gate_up.astype(BF16), w_down=w_down.astype(BF16),
    )

    def layer(xc, p):
        rope, bq, vals, cmp_in, small, gdn = _in_projection(xc, p["norm_mix"], p["mod"], p["w_in"], cos_t, sa_t, sb_t)
        o_a = _swa_attention(rope, vals, p["sinks"])
        kv_cmp, kv_cmp_t = _compress(cmp_in.reshape(B, n16, CMP_STRIDE * W_CMP), p["pea"], p["peb"], p["w1a"], p["w1b"], p["w2c"])
        o_cmp_t, sel = _cmp_select(bq, kv_cmp, kv_cmp_t, pool_t)
        vals_t = vals[:, :, A_KV:].reshape(B, S // KV_BLOCK, KV_BLOCK, 2 * HEAD_DIM).transpose(0, 1, 3, 2)
        o_b = _nsa_attention(rope, vals_t, sel, o_cmp_t, small)
        a_rows = small[:, :, SMALL_A:SMALL_A + 8].reshape(B, S // GDN_CHUNK, GDN_CHUNK, 8).transpose(0, 1, 3, 2)
        o_c = _gated_delta_net(gdn, small, a_rows, p["conv_w"], p["alog_row"], p["dtb_row"], p["alog_col"], p["dtb_col"], p["gdn_norm"])
        xc = _out_projection(xc, o_a, o_b, o_c, p["w_out"], p["mod"])
        xc = _ffn(xc, p["norm_ffn"], p["mod"], p["w_gate_up"], p["w_down"])
        return xc, None

    x, _ = lax.scan(layer, x, per_layer)
    return _final_norm(x, final_norm.reshape(1, D))
```

```python
import functools
import math

import numpy as np
import jax
import jax.numpy as jnp
from jax import lax
from jax.experimental import pallas as pl
from jax.experimental.pallas import tpu as pltpu

F32 = jnp.float32
BF16 = jnp.bfloat16
HIGHEST = lax.Precision.HIGHEST

D_MODEL = 1024
DEPTH = 4
HEAD_DIM = 64
ATTN_SCALE = HEAD_DIM ** -0.5
ROPE_THETA = 10000.0
NEG_INF = -1e30
NORM_EPS = 1e-6

A_HEADS = 4
A_KV_HEADS = 2
A_WINDOW = 128

B_HEADS = 4
CMP_STRIDE = 16
CMP_LEN = 32
CMP_HIDDEN = 256
SLC_LEN = 64
SLC_TOPK = 16
NSA_WINDOW = 512
SLC_FORCED_SCORE = 1e9

C_HEAD_DIM = 128
C_HEADS = 4
CONV_WIDTH = 4
GDN_CHUNK = 64
C_QK = C_HEADS * C_HEAD_DIM

D_FF = 2816

A_Q = A_HEADS * HEAD_DIM
A_KV = A_KV_HEADS * HEAD_DIM
B_Q = B_HEADS * HEAD_DIM
B_KV = HEAD_DIM
B_GATES = 3 * B_HEADS
IN_SPLITS = (A_Q, A_KV, A_KV, B_Q, B_KV, B_KV, B_KV, B_KV, B_KV, B_KV, B_GATES, 3 * C_QK, C_QK, C_HEADS, C_HEADS)

W_ROPE = 768
W_V = 256
W_CMP = 128
W_SMALL = 128
W_GDN = 2048
W_TOTAL = W_ROPE + W_V + W_CMP + W_SMALL + W_GDN
SMALL_A = B_GATES
SMALL_B = B_GATES + C_HEADS

VMEM_LIMIT = 56 * 1024 * 1024
PREP_GROUP = 4

NT_DIMS = (((1,), (1,)), ((), ()))


def _cparams(*sem):
    return pltpu.CompilerParams(dimension_semantics=sem, vmem_limit_bytes=VMEM_LIMIT)


def _iota(shape, dim):
    return lax.broadcasted_iota(jnp.int32, shape, dim)


def _dot(a, b, precision=None):
    return jnp.dot(a, b, preferred_element_type=F32, precision=precision)


def _dot_nt(a, b):
    return lax.dot_general(a, b, NT_DIMS, preferred_element_type=F32)


def _silu(x):
    return x * jax.nn.sigmoid(x)


def _softplus(x):
    return jnp.maximum(x, 0.0) + jnp.log(1.0 + jnp.exp(-jnp.abs(x)))


def _rms_mod(x, gain, sc, sh):
    y = x * lax.rsqrt(jnp.mean(x * x, axis=-1, keepdims=True) + NORM_EPS)
    return (y * gain) * (1.0 + sc) + sh


def _mod_kernel(c_ref, w_ref, b_ref, o_ref):
    o_ref[...] = _dot(_silu(c_ref[...]), w_ref[...], HIGHEST) + b_ref[...]


def _adaln_mod(c, ada_w, ada_b):
    L, D, N = ada_w.shape
    tn = 1536
    c8 = jnp.zeros((8, D), F32).at[: c.shape[0]].set(c)
    out = pl.pallas_call(
        _mod_kernel,
        grid=(L, N // tn),
        in_specs=[
            pl.BlockSpec((8, D), lambda l, j: (0, 0)),
            pl.BlockSpec((None, D, tn), lambda l, j: (l, 0, j)),
            pl.BlockSpec((None, 1, tn), lambda l, j: (l, 0, j)),
        ],
        out_specs=pl.BlockSpec((None, 8, tn), lambda l, j: (l, 0, j)),
        out_shape=jax.ShapeDtypeStruct((L, 8, N), F32),
        compiler_params=_cparams("arbitrary", "arbitrary"),
        name="adaln_mod",
    )(c8, ada_w, ada_b.reshape(L, 1, N))
    return out.reshape(L, 8, 1, N)


def _inproj_kernel(x_ref, g_ref, sc_ref, sh_ref, w_ref, cos_ref, sa_ref, sb_ref,
                   rope_ref, bq_ref, v_ref, cmp_ref, small_ref, gdn_ref):
    h = _rms_mod(x_ref[...], g_ref[...], sc_ref[...], sh_ref[...]).astype(BF16)
    yr = _dot(h, w_ref[:, 0:W_ROPE])
    bq_ref[...] = yr[:, A_Q:A_Q + B_Q].astype(BF16)
    c, sa, sb = cos_ref[...], sa_ref[...], sb_ref[...]
    for g in range(W_ROPE // 128):
        xg = yr[:, 128 * g:128 * (g + 1)]
        rot = xg * c + pltpu.roll(xg, 96, 1) * sa + pltpu.roll(xg, 32, 1) * sb
        rope_ref[:, 128 * g:128 * (g + 1)] = rot.astype(BF16)
    o = W_ROPE
    yp = _dot(h, w_ref[:, o:o + W_V + W_CMP + W_SMALL])
    v_ref[...] = yp[:, 0:W_V].astype(BF16)
    cmp_ref[...] = yp[:, W_V:W_V + W_CMP]
    small_ref[...] = yp[:, W_V + W_CMP:W_V + W_CMP + W_SMALL]
    o += W_V + W_CMP + W_SMALL
    gdn_ref[...] = _dot(h, w_ref[:, o:o + W_GDN])


def _in_projection(x, gain, mod, w, cos_t, sa_t, sb_t, tm=512):
    B, S, D = x.shape
    row = lambda b, i: (b, i, 0)
    outs = pl.pallas_call(
        _inproj_kernel,
        grid=(B, S // tm),
        in_specs=[
            pl.BlockSpec((None, tm, D), row),
            pl.BlockSpec((1, D), lambda b, i: (0, 0)),
            pl.BlockSpec((None, 1, D), lambda b, i: (b, 0, 1)),
            pl.BlockSpec((None, 1, D), lambda b, i: (b, 0, 0)),
            pl.BlockSpec((D, W_TOTAL), lambda b, i: (0, 0)),
            pl.BlockSpec((tm, 128), lambda b, i: (i, 0)),
            pl.BlockSpec((tm, 128), lambda b, i: (i, 0)),
            pl.BlockSpec((tm, 128), lambda b, i: (i, 0)),
        ],
        out_specs=[
            pl.BlockSpec((None, tm, W_ROPE), row),
            pl.BlockSpec((None, tm, B_Q), row),
            pl.BlockSpec((None, tm, W_V), row),
            pl.BlockSpec((None, tm, W_CMP), row),
            pl.BlockSpec((None, tm, W_SMALL), row),
            pl.BlockSpec((None, tm, W_GDN), row),
        ],
        out_shape=[
            jax.ShapeDtypeStruct((B, S, W_ROPE), BF16),
            jax.ShapeDtypeStruct((B, S, B_Q), BF16),
            jax.ShapeDtypeStruct((B, S, W_V), BF16),
            jax.ShapeDtypeStruct((B, S, W_CMP), F32),
            jax.ShapeDtypeStruct((B, S, W_SMALL), F32),
            jax.ShapeDtypeStruct((B, S, W_GDN), F32),
        ],
        compiler_params=_cparams("arbitrary", "arbitrary"),
        name="in_projection",
    )(x, gain, mod, mod, w, cos_t, sa_t, sb_t)
    return outs


def _swa_kernel(sink_ref, q_ref, k_ref, vt_ref, o_ref, *, tq, window):
    q0 = pl.program_id(1) * tq
    span = tq + window
    start = pl.multiple_of(jnp.maximum(q0 - window, 0), KV_BLOCK)
    qt = q_ref[...].astype(F32).T * ATTN_SCALE
    kpos = start + _iota((span, tq), 0)
    qpos = q0 + _iota((span, tq), 1)
    bias = jnp.where((kpos <= qpos) & (kpos > qpos - window), 0.0, NEG_INF)
    vb = vt_ref[pl.ds(start // KV_BLOCK, span // KV_BLOCK)]
    group = A_HEADS // A_KV_HEADS
    first_head = _iota((1, group * tq), 1) < tq
    outs = []
    for g in range(A_KV_HEADS):
        heads = [group * g + r for r in range(group)]
        qg = jnp.concatenate([qt[HEAD_DIM * h:HEAD_DIM * (h + 1)] for h in heads], axis=1).astype(BF16)
        s = _dot(k_ref[pl.ds(start, span), HEAD_DIM * g:HEAD_DIM * (g + 1)], qg)
        s = s + jnp.concatenate([bias] * group, axis=1)
        sink = jnp.where(first_head, sink_ref[heads[0]], sink_ref[heads[1]])
        m = jnp.maximum(jnp.max(s, axis=0, keepdims=True), sink)
        p = jnp.exp(s - m)
        den = jnp.sum(p, axis=0, keepdims=True) + jnp.exp(sink - m)
        vt = jnp.concatenate([vb[r, HEAD_DIM * g:HEAD_DIM * (g + 1), :] for r in range(span // KV_BLOCK)], axis=1)
        o_t = _dot(vt, p.astype(BF16)) * (1.0 / den)
        outs += [o_t[:, tq * r:tq * (r + 1)] for r in range(group)]
    o_ref[...] = jnp.concatenate(outs, axis=0).T.astype(BF16)


def _swa_attention(rope, vals_t, sinks, tq=256):
    assert A_HEADS // A_KV_HEADS == 2
    B, S, _ = rope.shape
    return pl.pallas_call(
        functools.partial(_swa_kernel, tq=tq, window=A_WINDOW),
        grid=(B, S // tq),
        in_specs=[
            pl.BlockSpec(memory_space=pltpu.SMEM),
            pl.BlockSpec((None, tq, A_Q), lambda b, i: (b, i, 0)),
            pl.BlockSpec((None, S, A_KV), lambda b, i: (b, 0, (A_Q + B_Q) // A_KV)),
            pl.BlockSpec((None, S // KV_BLOCK, A_KV, KV_BLOCK), lambda b, i: (b, 0, 0, 0)),
        ],
        out_specs=pl.BlockSpec((None, tq, A_Q), lambda b, i: (b, i, 0)),
        out_shape=jax.ShapeDtypeStruct((B, S, A_Q), BF16),
        compiler_params=_cparams("arbitrary", "arbitrary"),
        name="swa_attention",
    )(sinks, rope, rope, vals_t)


def _compress_kernel(x_ref, pea_ref, peb_ref, w1a_ref, w1b_ref, w2_ref, o_ref, ot_ref):
    x = x_ref[...]
    n = x.shape[0]
    a = _dot((x + pea_ref[...]).astype(BF16), w1a_ref[...])
    b = _dot((x + peb_ref[...]).astype(BF16), w1b_ref[...])
    hid = _silu(a + pltpu.roll(b, n - 1, 0))
    out = _dot(hid.astype(BF16), w2_ref[...])
    o_ref[...] = out.astype(BF16)
    ot_ref[...] = out.T.astype(BF16)


def _compress(xc, pea, peb, w1a, w1b, w2):
    B, n, K = xc.shape
    full = lambda a: pl.BlockSpec(a.shape, lambda b: (0,) * a.ndim)
    return pl.pallas_call(
        _compress_kernel,
        grid=(B,),
        in_specs=[pl.BlockSpec((None, n, K), lambda b: (b, 0, 0)), full(pea), full(peb), full(w1a), full(w1b), full(w2)],
        out_specs=[pl.BlockSpec((None, n, 2 * HEAD_DIM), lambda b: (b, 0, 0)),
                   pl.BlockSpec((None, 2 * HEAD_DIM, n), lambda b: (b, 0, 0))],
        out_shape=[jax.ShapeDtypeStruct((B, n, 2 * HEAD_DIM), BF16), jax.ShapeDtypeStruct((B, 2 * HEAD_DIM, n), BF16)],
        compiler_params=_cparams("arbitrary"),
        name="nsa_compress",
    )(xc, pea, peb, w1a, w1b, w2)


def _heads_on_lanes(q):
    qt = q.astype(F32).T * ATTN_SCALE
    return jnp.concatenate([qt[HEAD_DIM * h:HEAD_DIM * (h + 1)] for h in range(B_HEADS)], axis=1).astype(BF16)


def _heads_on_sublanes(ot, tq):
    return jnp.concatenate([ot[:, tq * h:tq * (h + 1)] for h in range(B_HEADS)], axis=0)


def _cmp_select_kernel(q_ref, k_ref, vt_ref, pool_ref, o_ref, sel_ref, *, tq, top_k):
    q0 = pl.program_id(1) * tq
    H = B_HEADS
    n_cmp = k_ref.shape[0]
    n_slc = pool_ref.shape[0]
    qt = _heads_on_lanes(q_ref[...])
    s = _dot(k_ref[:, 0:HEAD_DIM], qt)
    valid = _iota((n_cmp, tq), 0) * CMP_STRIDE + (CMP_LEN - 1) <= q0 + _iota((n_cmp, tq), 1)
    valid = jnp.concatenate([valid] * H, axis=1)
    s = jnp.where(valid, s, NEG_INF)
    e = jnp.where(valid, jnp.exp(s - jnp.max(s, axis=0, keepdims=True)), 0.0)
    den = jnp.sum(e, axis=0, keepdims=True)
    p = e * (1.0 / jnp.where(den > 0.0, den, 1.0))
    o_t = _dot(vt_ref[HEAD_DIM:2 * HEAD_DIM, :], p.astype(BF16))
    o_ref[...] = _heads_on_sublanes(o_t, tq)
    psum = p[:, 0:tq]
    for h in range(1, H):
        psum = psum + p[:, tq * h:tq * (h + 1)]

    imp = _dot(pool_ref[...], psum, HIGHEST)
    blk = _iota((n_slc, tq), 0)
    tt = q0 + _iota((n_slc, tq), 1)
    cur = tt >> 6
    forced = (blk == 0) | (blk == cur) | (blk == cur - 1)
    causal = blk * SLC_LEN <= tt
    score = jnp.where(forced, SLC_FORCED_SCORE, jnp.where(causal, imp, NEG_INF))
    sel = jnp.zeros((n_slc, tq), F32)
    blk_f = blk.astype(F32)
    for _ in range(top_k):
        mx = jnp.max(score, axis=0, keepdims=True)
        first = jnp.min(jnp.where(score == mx, blk_f, float(n_slc)), axis=0, keepdims=True)
        pick = blk_f == first
        sel = jnp.where(pick, 1.0, sel)
        score = jnp.where(pick, -jnp.inf, score)
    sel_ref[...] = jnp.where(causal, sel, 0.0)


def _cmp_select(bq, kv_cmp, kv_cmp_t, pool_t, tq=256):
    B, S, _ = bq.shape
    n_slc, n_cmp = pool_t.shape
    top_k = min(SLC_TOPK, n_slc)
    return pl.pallas_call(
        functools.partial(_cmp_select_kernel, tq=tq, top_k=top_k),
        grid=(B, S // tq),
        in_specs=[
            pl.BlockSpec((None, tq, B_Q), lambda b, i: (b, i, 0)),
            pl.BlockSpec((None, n_cmp, 2 * HEAD_DIM), lambda b, i: (b, 0, 0)),
            pl.BlockSpec((None, 2 * HEAD_DIM, n_cmp), lambda b, i: (b, 0, 0)),
            pl.BlockSpec((n_slc, n_cmp), lambda b, i: (0, 0)),
        ],
        out_specs=[
            pl.BlockSpec((None, B_Q, tq), lambda b, i: (b, 0, i)),
            pl.BlockSpec((None, n_slc, tq), lambda b, i: (b, 0, i)),
        ],
        out_shape=[jax.ShapeDtypeStruct((B, B_Q, S), F32), jax.ShapeDtypeStruct((B, n_slc, S), F32)],
        compiler_params=_cparams("arbitrary", "arbitrary"),
        name="nsa_cmp_select",
    )(bq, kv_cmp, kv_cmp_t, pool_t)


KV_BLOCK = 128


def _nsa_attn_kernel(q_ref, k_ref, vt_ref, sel_ref, ocmp_ref, small_ref, o_ref, *, tq, tk, window):
    i = pl.program_id(1)
    q0 = i * tq
    H = B_HEADS
    qt = _heads_on_lanes(q_ref[...])
    blocks_per_tile = tk // SLC_LEN
    vblocks_per_tile = tk // KV_BLOCK
    qpos = q0 + _iota((tk, tq), 1)

    def tile(j, s, carry):
        m, l, acc = carry
        k0 = pl.multiple_of(j * tk, tk)
        sel = sel_ref[pl.ds(pl.multiple_of(j * blocks_per_tile, blocks_per_tile), blocks_per_tile), :]
        picked = jnp.concatenate(
            [jnp.broadcast_to(sel[r:r + 1, :], (SLC_LEN, tq)) for r in range(blocks_per_tile)], axis=0)
        ok = (picked > 0.5) & (k0 + _iota((tk, tq), 0) <= qpos)
        bias = jnp.where(ok, 0.0, NEG_INF)
        s = s + jnp.concatenate([bias] * H, axis=1)
        m_new = jnp.maximum(m, jnp.max(s, axis=0, keepdims=True))
        alpha = jnp.exp(m - m_new)
        p = jnp.exp(s - m_new)
        l = l * alpha + jnp.sum(p, axis=0, keepdims=True)
        vb = vt_ref[pl.ds(pl.multiple_of(j * vblocks_per_tile, vblocks_per_tile), vblocks_per_tile)]
        vt = jnp.concatenate([vb[r, 0:HEAD_DIM, :] for r in range(vblocks_per_tile)], axis=1)
        acc = acc * alpha + _dot(vt, p.astype(BF16))
        return m_new, l, acc

    def scores(j):
        return _dot(k_ref[pl.ds(pl.multiple_of(j * tk, tk), tk), 0:HEAD_DIM], qt)

    def tile_pair(jp, carry):
        s0 = scores(2 * jp)
        s1 = scores(2 * jp + 1)
        return tile(2 * jp + 1, s1, tile(2 * jp, s0, carry))

    n_pairs = (q0 + tq + 2 * tk - 1) // (2 * tk)
    init = (jnp.full((1, H * tq), NEG_INF, F32), jnp.zeros((1, H * tq), F32), jnp.zeros((HEAD_DIM, H * tq), F32))
    _, l, acc = lax.fori_loop(0, n_pairs, tile_pair, init)
    o_slc = _heads_on_sublanes(acc * (1.0 / l), tq)

    span = tq + window
    start = pl.multiple_of(jnp.maximum(q0 - window, 0), KV_BLOCK)
    s = _dot(k_ref[pl.ds(start, span), HEAD_DIM:2 * HEAD_DIM], qt)
    kpos = start + _iota((span, tq), 0)
    qp = q0 + _iota((span, tq), 1)
    bias = jnp.where((kpos <= qp) & (kpos > qp - window), 0.0, NEG_INF)
    s = s + jnp.concatenate([bias] * H, axis=1)
    p = jnp.exp(s - jnp.max(s, axis=0, keepdims=True))
    vb = vt_ref[pl.ds(start // KV_BLOCK, span // KV_BLOCK)]
    vt = jnp.concatenate([vb[r, HEAD_DIM:2 * HEAD_DIM, :] for r in range(span // KV_BLOCK)], axis=1)
    o_win = _heads_on_sublanes(_dot(vt, p.astype(BF16)) * (1.0 / jnp.sum(p, axis=0, keepdims=True)), tq)

    gates = jax.nn.sigmoid(small_ref[...]).T
    o_cmp = ocmp_ref[...]
    outs = []
    for h in range(H):
        rows = slice(HEAD_DIM * h, HEAD_DIM * (h + 1))
        outs.append(gates[h:h + 1, :] * o_cmp[rows] + gates[H + h:H + h + 1, :] * o_slc[rows]
                    + gates[2 * H + h:2 * H + h + 1, :] * o_win[rows])
    o_ref[...] = jnp.concatenate(outs, axis=0).T.astype(BF16)


def _nsa_attention(rope, vals_t, sel, o_cmp, small, tq=256, tk=512):
    B, S, _ = rope.shape
    n_slc = sel.shape[1]
    return pl.pallas_call(
        functools.partial(_nsa_attn_kernel, tq=tq, tk=tk, window=NSA_WINDOW),
        grid=(B, S // tq),
        in_specs=[
            pl.BlockSpec((None, tq, B_Q), lambda b, i: (b, i, A_Q // B_Q)),
            pl.BlockSpec((None, S, 128), lambda b, i: (b, 0, (A_Q + B_Q + A_KV) // 128)),
            pl.BlockSpec((None, S // KV_BLOCK, 2 * HEAD_DIM, KV_BLOCK), lambda b, i: (b, 0, A_KV // (2 * HEAD_DIM), 0)),
            pl.BlockSpec((None, n_slc, tq), lambda b, i: (b, 0, i)),
            pl.BlockSpec((None, B_Q, tq), lambda b, i: (b, 0, i)),
            pl.BlockSpec((None, tq, W_SMALL), lambda b, i: (b, i, 0)),
        ],
        out_specs=pl.BlockSpec((None, tq, B_Q), lambda b, i: (b, i, 0)),
        out_shape=jax.ShapeDtypeStruct((B, S, B_Q), BF16),
        compiler_params=_cparams("arbitrary", "arbitrary"),
        name="nsa_attention",
    )(rope, rope, vals_t, sel, o_cmp, small)


def _gdn_kernel(xq_ref, xk_ref, xv_ref, hq_ref, hk_ref, hv_ref, z_ref, small_ref, arow_ref,
                cw_ref, alog_row_ref, dtb_row_ref, alog_col_ref, dtb_col_ref, nw_ref,
                o_ref, q_s, k_s, v_s, g_s, b_s, o_s, state_s, u_s, wq_s, qk_s, kdt_s, gl_s, pq_s, pk_s, pv_s, *, tt):
    i = pl.program_id(1)
    C = GDN_CHUNK
    dk = C_HEAD_DIM
    H = C_HEADS

    @pl.when(i == 0)
    def _():
        state_s[...] = jnp.zeros_like(state_s)

    def conv(x_ref, halo_ref, pad_s, col0):
        pad_s[0:8, :] = jnp.where(i > 0, halo_ref[...], 0.0)
        pad_s[8:8 + tt, :] = x_ref[...]
        w = cw_ref[:, col0:col0 + C_QK]
        y = x_ref[...] * w[3:4]
        for kback in range(1, CONV_WIDTH):
            y = y + pad_s[8 - kback:8 - kback + tt, :] * w[3 - kback:4 - kback]
        return _silu(y)

    qf = conv(xq_ref, hq_ref, pq_s, 0)
    kf = conv(xk_ref, hk_ref, pk_s, C_QK)
    v_s[...] = conv(xv_ref, hv_ref, pv_s, 2 * C_QK)
    for h in range(H):
        cols = slice(dk * h, dk * (h + 1))
        qh = qf[:, cols]
        kh = kf[:, cols]
        q_s[:, cols] = qh * lax.rsqrt(jnp.sum(qh * qh, axis=-1, keepdims=True) + NORM_EPS) * (dk ** -0.5)
        k_s[:, cols] = kh * lax.rsqrt(jnp.sum(kh * kh, axis=-1, keepdims=True) + NORM_EPS)
    small = small_ref[...]
    g_s[...] = -jnp.exp(alog_row_ref[...]) * _softplus(small + dtb_row_ref[...])
    b_s[...] = jax.nn.sigmoid(small)

    r_i = _iota((C, C), 0)
    c_i = _iota((C, C), 1)
    lower = r_i >= c_i
    strict = r_i > c_i
    tri_l = lower.astype(F32)
    tri_u = (r_i <= c_i).astype(F32)
    neg_alog_col = -jnp.exp(alog_col_ref[...])
    dtb_col = dtb_col_ref[...]

    HC = H * C
    rr = _iota((HC, HC), 0)
    cc = _iota((HC, HC), 1)
    same_head = (rr >> 6) == (cc >> 6)
    lower_bd = same_head & (rr >= cc)
    strict_bd = same_head & (rr > cc)
    state_bd = (_iota((HC, H * dk), 0) >> 6) == (_iota((HC, H * dk), 1) >> 7)

    def stack_rows(ref, rows):
        return jnp.concatenate([ref[rows, dk * h:dk * (h + 1)] for h in range(H)], axis=0)

    def prepare(c):
        r0 = pl.multiple_of(c * C, C)
        rows = pl.ds(r0, C)
        gc_col = _dot(tri_l, g_s[rows, :], HIGHEST)
        g_row = neg_alog_col * _softplus(arow_ref[c] + dtb_col)
        gc_row = _dot(g_row, tri_u, HIGHEST)
        beta_all = b_s[rows, :]
        gcol = jnp.concatenate([gc_col[:, SMALL_A + h:SMALL_A + h + 1] for h in range(H)], axis=0)
        grow = jnp.concatenate([gc_row[h:h + 1, :] for h in range(H)], axis=1)
        beta = jnp.concatenate([beta_all[:, SMALL_B + h:SMALL_B + h + 1] for h in range(H)], axis=0)
        g_last = [gc_col[C - 1:C, SMALL_A + h:SMALL_A + h + 1] for h in range(H)]
        glast = jnp.concatenate([jnp.broadcast_to(g, (C, 1)) for g in g_last], axis=0)
        q = stack_rows(q_s, rows)
        k = stack_rows(k_s, rows)
        v = stack_rows(v_s, rows)
        decay = jnp.where(lower_bd, jnp.exp(jnp.where(lower_bd, gcol - grow, 0.0)), 0.0)
        eg = jnp.exp(gcol)
        kb = k * beta
        gram = _dot_nt(jnp.concatenate([kb, q], axis=0).astype(BF16), k.astype(BF16))
        a_mat = jnp.where(strict_bd, gram[0:HC] * decay, 0.0)
        qk = jnp.where(lower_bd, gram[HC:2 * HC] * decay, 0.0)
        pw = -a_mat
        x = pw
        pw16 = pw.astype(BF16)
        pw = _dot(pw16, pw16)
        for _ in range(4):
            pw16 = pw.astype(BF16)
            x, pw = x + pw + _dot(pw16, x.astype(BF16)), _dot(pw16, pw16)
        x = x + pw + _dot(pw.astype(BF16), x.astype(BF16))
        rhs = jnp.concatenate([v * beta, kb * eg], axis=1)
        y = rhs + _dot(x.astype(BF16), rhs.astype(BF16))
        kd = k * jnp.exp(glast - gcol)
        u_s[c] = y[:, 0:dk]
        wq_s[c, 0:HC, :] = y[:, dk:2 * dk].astype(BF16)
        wq_s[c, HC:2 * HC, :] = (q * eg).astype(BF16)
        qk_s[c] = qk.astype(BF16)
        kdt_s[c] = kd.T.astype(BF16)
        gl_s[c, 0:1, :] = jnp.concatenate([jnp.broadcast_to(jnp.exp(g), (1, dk)) for g in g_last], axis=1)

    def prepare_group(cg, carry):
        for r in range(PREP_GROUP):
            prepare(PREP_GROUP * cg + r)
        return carry

    lax.fori_loop(0, tt // (PREP_GROUP * C), prepare_group, 0)

    def head_blocks(x, row0):
        return jnp.concatenate([x[row0 + C * h:row0 + C * (h + 1), dk * h:dk * (h + 1)] for h in range(H)], axis=0)

    def scan(c, carry):
        r0 = pl.multiple_of(c * C, C)
        rows = pl.ds(r0, C)
        st = state_s[...]
        d1 = _dot(wq_s[c], st.astype(BF16))
        v_new = u_s[c] - head_blocks(d1, 0)
        v16 = v_new.astype(BF16)
        o = head_blocks(d1, HC) + _dot(qk_s[c], v16)
        v_bd = jnp.where(state_bd, jnp.concatenate([v16] * H, axis=1), jnp.zeros((), BF16))
        state_s[...] = st * gl_s[c, 0:1, :] + _dot(kdt_s[c], v_bd)
        for h in range(H):
            o_s[rows, dk * h:dk * (h + 1)] = o[C * h:C * (h + 1)]
        return carry

    lax.fori_loop(0, tt // C, scan, 0)

    nw = nw_ref[...]
    z = z_ref[...]
    for h in range(H):
        cols = slice(dk * h, dk * (h + 1))
        oh = o_s[:, cols]
        y = oh * lax.rsqrt(jnp.mean(oh * oh, axis=-1, keepdims=True) + NORM_EPS) * nw
        o_ref[:, cols] = (y * _silu(z[:, cols])).astype(BF16)


def _gated_delta_net(gdn, small, a_rows, conv_w, alog_row, dtb_row, alog_col, dtb_col, norm_w, tt=512):
    B, S, _ = gdn.shape
    nch = tt // GDN_CHUNK
    hc = C_HEADS * GDN_CHUNK
    part = lambda j: pl.BlockSpec((None, tt, C_QK), lambda b, i: (b, i, j))
    halo = lambda j: pl.BlockSpec((None, 8, C_QK), lambda b, i: (b, jnp.maximum(i * (tt // 8) - 1, 0), j))
    full = lambda a: pl.BlockSpec(a.shape, lambda b, i: (0,) * a.ndim)
    return pl.pallas_call(
        functools.partial(_gdn_kernel, tt=tt),
        grid=(B, S // tt),
        in_specs=[part(0), part(1), part(2), halo(0), halo(1), halo(2), part(3),
                  pl.BlockSpec((None, tt, W_SMALL), lambda b, i: (b, i, 0)),
                  pl.BlockSpec((None, nch, 8, GDN_CHUNK), lambda b, i: (b, i, 0, 0)),
                  full(conv_w), full(alog_row), full(dtb_row), full(alog_col), full(dtb_col), full(norm_w)],
        out_specs=pl.BlockSpec((None, tt, C_QK), lambda b, i: (b, i, 0)),
        out_shape=jax.ShapeDtypeStruct((B, S, C_QK), BF16),
        scratch_shapes=[pltpu.VMEM((tt, C_QK), F32), pltpu.VMEM((tt, C_QK), F32), pltpu.VMEM((tt, C_QK), F32),
                        pltpu.VMEM((tt, W_SMALL), F32), pltpu.VMEM((tt, W_SMALL), F32),
                        pltpu.VMEM((tt, C_QK), F32), pltpu.VMEM((C_HEAD_DIM, C_QK), F32),
                        pltpu.VMEM((nch, hc, C_HEAD_DIM), F32), pltpu.VMEM((nch, 2 * hc, C_HEAD_DIM), BF16),
                        pltpu.VMEM((nch, hc, hc), BF16), pltpu.VMEM((nch, C_HEAD_DIM, hc), BF16),
                        pltpu.VMEM((nch, 8, C_QK), F32)] + [pltpu.VMEM((tt + 8, C_QK), F32)] * 3,
        compiler_params=_cparams("arbitrary", "arbitrary"),
        name="gated_delta_net",
    )(gdn, gdn, gdn, gdn, gdn, gdn, gdn, small, a_rows, conv_w, alog_row, dtb_row, alog_col, dtb_col, norm_w)


def _outproj_kernel(x_ref, oa_ref, ob_ref, oc_ref, w_ref, gt_ref, o_ref):
    y = _dot(oa_ref[...], w_ref[0:A_Q, :])
    y = y + _dot(ob_ref[...], w_ref[A_Q:A_Q + B_Q, :])
    y = y + _dot(oc_ref[...], w_ref[A_Q + B_Q:, :])
    o_ref[...] = x_ref[...] + gt_ref[...] * y


def _out_projection(x, o_a, o_b, o_c, w, mod, tm=1024):
    B, S, D = x.shape
    row = lambda b, i: (b, i, 0)
    return pl.pallas_call(
        _outproj_kernel,
        grid=(B, S // tm),
        in_specs=[
            pl.BlockSpec((None, tm, D), row),
            pl.BlockSpec((None, tm, A_Q), row),
            pl.BlockSpec((None, tm, B_Q), row),
            pl.BlockSpec((None, tm, C_QK), row),
            pl.BlockSpec((D, D), lambda b, i: (0, 0)),
            pl.BlockSpec((None, 1, D), lambda b, i: (b, 0, 2)),
        ],
        out_specs=pl.BlockSpec((None, tm, D), row),
        out_shape=jax.ShapeDtypeStruct((B, S, D), F32),
        compiler_params=_cparams("arbitrary", "arbitrary"),
        name="out_projection",
    )(x, o_a, o_b, o_c, w, mod)


def _ffn_kernel(x_ref, g_ref, sc_ref, sh_ref, gt_ref, wgu_ref, wd_ref, o_ref, *, tf):
    x = x_ref[...]
    h = _rms_mod(x, g_ref[...], sc_ref[...], sh_ref[...]).astype(BF16)
    acc = None
    for f in range(D_FF // tf):
        gate = _dot(h, wgu_ref[:, tf * f:tf * (f + 1)])
        up = _dot(h, wgu_ref[:, D_FF + tf * f:D_FF + tf * (f + 1)])
        part = _dot((_silu(gate) * up).astype(BF16), wd_ref[tf * f:tf * (f + 1), :])
        acc = part if acc is None else acc + part
    o_ref[...] = x + gt_ref[...] * acc


def _ffn(x, gain, mod, w_gate_up, w_down, tm=512, tf=1408):
    B, S, D = x.shape
    row = lambda b, i: (b, i, 0)
    resident = lambda a: pl.BlockSpec(a.shape, lambda b, i: (0, 0), pipeline_mode=pl.Buffered(1))
    return pl.pallas_call(
        functools.partial(_ffn_kernel, tf=tf),
        grid=(B, S // tm),
        in_specs=[
            pl.BlockSpec((None, tm, D), row),
            pl.BlockSpec((1, D), lambda b, i: (0, 0)),
            pl.BlockSpec((None, 1, D), lambda b, i: (b, 0, 4)),
            pl.BlockSpec((None, 1, D), lambda b, i: (b, 0, 3)),
            pl.BlockSpec((None, 1, D), lambda b, i: (b, 0, 5)),
            resident(w_gate_up),
            resident(w_down),
        ],
        out_specs=pl.BlockSpec((None, tm, D), row),
        out_shape=jax.ShapeDtypeStruct((B, S, D), F32),
        compiler_params=_cparams("arbitrary", "arbitrary"),
        name="swiglu_ffn",
    )(x, gain, mod, mod, mod, w_gate_up, w_down)


def _final_norm_kernel(x_ref, g_ref, o_ref):
    x = x_ref[...]
    o_ref[...] = x * lax.rsqrt(jnp.mean(x * x, axis=-1, keepdims=True) + NORM_EPS) * g_ref[...]


def _final_norm(x, gain, tm=512):
    B, S, D = x.shape
    return pl.pallas_call(
        _final_norm_kernel,
        grid=(B, S // tm),
        in_specs=[pl.BlockSpec((None, tm, D), lambda b, i: (b, i, 0)), pl.BlockSpec((1, D), lambda b, i: (0, 0))],
        out_specs=pl.BlockSpec((None, tm, D), lambda b, i: (b, i, 0)),
        out_shape=jax.ShapeDtypeStruct((B, S, D), F32),
        compiler_params=_cparams("arbitrary", "arbitrary"),
        name="final_norm",
    )(x, gain)


def _rope_tables(seq):
    inv = 1.0 / (ROPE_THETA ** (jnp.arange(0, HEAD_DIM, 2, dtype=F32) / HEAD_DIM))
    ang = jnp.arange(seq, dtype=F32)[:, None] * inv[None, :]
    cos, sin = jnp.cos(ang), jnp.sin(ang)
    zero = jnp.zeros_like(sin)
    cos_t = jnp.tile(cos, (1, 4))
    sa_t = jnp.tile(jnp.concatenate([-sin, zero], axis=1), (1, 2))
    sb_t = jnp.tile(jnp.concatenate([zero, sin], axis=1), (1, 2))
    return cos_t, sa_t, sb_t


def _permute_w_in(w_in):
    pts = np.cumsum(IN_SPLITS)[:-1].tolist()
    aq, ak, av, bq, bkc, bvc, bks, bvs, bkw, bvw, bg, cqkv, cz, ca, cb = jnp.split(w_in, pts, axis=-1)
    pad = jnp.zeros(w_in.shape[:-1] + (W_SMALL - B_GATES - 2 * C_HEADS,), w_in.dtype)
    return jnp.concatenate([aq, bq, ak, bks, bkw, av, bvs, bvw, bkc, bvc, bg, ca, cb, pad, cqkv, cz], axis=-1).astype(BF16)


def _compress_weights(k_w1, k_w2, v_w1, v_w2, pe_k, pe_v):
    L = k_w1.shape[0]
    half = CMP_STRIDE * HEAD_DIM

    def first_layer(lo):
        wk = k_w1[:, lo:lo + half].reshape(L, CMP_STRIDE, HEAD_DIM, CMP_HIDDEN)
        wv = v_w1[:, lo:lo + half].reshape(L, CMP_STRIDE, HEAD_DIM, CMP_HIDDEN)
        z = jnp.zeros_like(wk)
        w = jnp.concatenate([jnp.concatenate([wk, z], axis=-1), jnp.concatenate([z, wv], axis=-1)], axis=2)
        return w.reshape(L, 2 * half, 2 * CMP_HIDDEN).astype(BF16)

    def pe_rows(lo):
        return jnp.concatenate([pe_k[:, lo:lo + CMP_STRIDE], pe_v[:, lo:lo + CMP_STRIDE]], axis=-1).reshape(L, 1, 2 * half)

    z2 = jnp.zeros_like(k_w2)
    w2 = jnp.concatenate([jnp.concatenate([k_w2, z2], axis=-1), jnp.concatenate([z2, v_w2], axis=-1)], axis=1).astype(BF16)
    return pe_rows(0), pe_rows(CMP_STRIDE), first_layer(0), first_layer(half), w2


def _pool_matrix(n_cmp_rows, n_slc):
    ratio = SLC_LEN // CMP_STRIDE
    n = np.arange(n_cmp_rows)[:, None]
    j = np.arange(n_slc)[None, :]
    return jnp.asarray(((n // ratio == j) | (n == ratio * j - 1)).astype(np.float32))


def _lane_row(v, lane0):
    L, n = v.shape
    return jnp.zeros((L, 1, 128), F32).at[:, 0, lane0:lane0 + n].set(v)


def _sublane_col(v):
    L, n = v.shape
    return jnp.zeros((L, 8, 1), F32).at[:, :n, 0].set(v)


def kernel(x, c, norm_mix, norm_ffn, ada_w, ada_b, w_in, attn_sinks, cmp_k_w1, cmp_k_w2, cmp_v_w1, cmp_v_w2, cmp_pe_k, cmp_pe_v, gdn_conv_w, gdn_A_log, gdn_dt_bias, gdn_norm, w_out, w_gate_up, w_down, final_norm):
    B, S, D = x.shape
    L = w_in.shape[0]
    n16 = S // CMP_STRIDE
    n_slc = S // SLC_LEN

    cos_t, sa_t, sb_t = _rope_tables(S)
    mod = _adaln_mod(c, ada_w, ada_b)
    pea, peb, w1a, w1b, w2c = _compress_weights(cmp_k_w1, cmp_k_w2, cmp_v_w1, cmp_v_w2, cmp_pe_k, cmp_pe_v)
    pool_t = _pool_matrix(n16, n_slc).T
    per_layer = dict(
        mod=mod,
        norm_mix=norm_mix.reshape(L, 1, D), norm_ffn=norm_ffn.reshape(L, 1, D),
        w_in=_permute_w_in(w_in), sinks=attn_sinks,
        pea=pea, peb=peb, w1a=w1a, w1b=w1b, w2c=w2c,
        conv_w=gdn_conv_w,
        alog_row=_lane_row(gdn_A_log, SMALL_A), dtb_row=_lane_row(gdn_dt_bias, SMALL_A),
        alog_col=_sublane_col(gdn_A_log), dtb_col=_sublane_col(gdn_dt_bias),
        gdn_norm=gdn_norm.reshape(L, 1, C_HEAD_DIM),
        w_out=w_out.astype(BF16), w_gate_up=w_gate_up.astype(BF16), w_down=w_down.astype(BF16),
    )

    def layer(xc, p):
        rope, bq, vals, cmp_in, small, gdn = _in_projection(xc, p["norm_mix"], p["mod"], p["w_in"], cos_t, sa_t, sb_t)
        vals_all_t = vals.reshape(B, S // KV_BLOCK, KV_BLOCK, W_V).transpose(0, 1, 3, 2)
        o_a = _swa_attention(rope, vals_all_t, p["sinks"])
        kv_cmp, kv_cmp_t = _compress(cmp_in.reshape(B, n16, CMP_STRIDE * W_CMP), p["pea"], p["peb"], p["w1a"], p["w1b"], p["w2c"])
        o_cmp_t, sel = _cmp_select(bq, kv_cmp, kv_cmp_t, pool_t)
        o_b = _nsa_attention(rope, vals_all_t, sel, o_cmp_t, small)
        a_rows = small[:, :, SMALL_A:SMALL_A + 8].reshape(B, S // GDN_CHUNK, GDN_CHUNK, 8).transpose(0, 1, 3, 2)
        o_c = _gated_delta_net(gdn, small, a_rows, p["conv_w"], p["alog_row"], p["dtb_row"], p["alog_col"], p["dtb_col"], p["gdn_norm"])
        xc = _out_projection(xc, o_a, o_b, o_c, p["w_out"], p["mod"])
        xc = _ffn(xc, p["norm_ffn"], p["mod"], p["w_gate_up"], p["w_down"])
        return xc, None

    x, _ = lax.scan(layer, x, per_layer)
    return _final_norm(x, final_norm.reshape(1, D))
```

```python
import functools
import math

import numpy as np
import jax
import jax.numpy as jnp
from jax import lax
from jax.experimental import pallas as pl
from jax.experimental.pallas import tpu as pltpu

F32 = jnp.float32
BF16 = jnp.bfloat16
HIGHEST = lax.Precision.HIGHEST

D_MODEL = 1024
DEPTH = 4
HEAD_DIM = 64
ATTN_SCALE = HEAD_DIM ** -0.5
ROPE_THETA = 10000.0
NEG_INF = -1e30
NORM_EPS = 1e-6

A_HEADS = 4
A_KV_HEADS = 2
A_WINDOW = 128

B_HEADS = 4
CMP_STRIDE = 16
CMP_LEN = 32
CMP_HIDDEN = 256
SLC_LEN = 64
SLC_TOPK = 16
NSA_WINDOW = 512
SLC_FORCED_SCORE = 1e9

C_HEAD_DIM = 128
C_HEADS = 4
CONV_WIDTH = 4
GDN_CHUNK = 64
C_QK = C_HEADS * C_HEAD_DIM

D_FF = 2816

A_Q = A_HEADS * HEAD_DIM
A_KV = A_KV_HEADS * HEAD_DIM
B_Q = B_HEADS * HEAD_DIM
B_KV = HEAD_DIM
B_GATES = 3 * B_HEADS
IN_SPLITS = (A_Q, A_KV, A_KV, B_Q, B_KV, B_KV, B_KV, B_KV, B_KV, B_KV, B_GATES, 3 * C_QK, C_QK, C_HEADS, C_HEADS)

W_ROPE = 768
W_V = 256
W_CMP = 128
W_SMALL = 128
W_GDN = 2048
W_TOTAL = W_ROPE + W_V + W_CMP + W_SMALL + W_GDN
SMALL_A = B_GATES
SMALL_B = B_GATES + C_HEADS

VMEM_LIMIT = 56 * 1024 * 1024
PREP_GROUP = 4

NT_DIMS = (((1,), (1,)), ((), ()))


def _cparams(*sem):
    return pltpu.CompilerParams(dimension_semantics=sem, vmem_limit_bytes=VMEM_LIMIT)


def _iota(shape, dim):
    return lax.broadcasted_iota(jnp.int32, shape, dim)


def _dot(a, b, precision=None):
    return jnp.dot(a, b, preferred_element_type=F32, precision=precision)


def _dot_nt(a, b):
    return lax.dot_general(a, b, NT_DIMS, preferred_element_type=F32)


def _silu(x):
    return x * jax.nn.sigmoid(x)


def _softplus(x):
    return jnp.maximum(x, 0.0) + jnp.log(1.0 + jnp.exp(-jnp.abs(x)))


def _rms_mod(x, gain, sc, sh):
    y = x * lax.rsqrt(jnp.mean(x * x, axis=-1, keepdims=True) + NORM_EPS)
    return (y * gain) * (1.0 + sc) + sh


def _mod_kernel(c_ref, w_ref, b_ref, o_ref):
    o_ref[...] = _dot(_silu(c_ref[...]), w_ref[...], HIGHEST) + b_ref[...]


def _adaln_mod(c, ada_w, ada_b):
    L, D, N = ada_w.shape
    tn = 1536
    c8 = jnp.zeros((8, D), F32).at[: c.shape[0]].set(c)
    out = pl.pallas_call(
        _mod_kernel,
        grid=(L, N // tn),
        in_specs=[
            pl.BlockSpec((8, D), lambda l, j: (0, 0)),
            pl.BlockSpec((None, D, tn), lambda l, j: (l, 0, j)),
            pl.BlockSpec((None, 1, tn), lambda l, j: (l, 0, j)),
        ],
        out_specs=pl.BlockSpec((None, 8, tn), lambda l, j: (l, 0, j)),
        out_shape=jax.ShapeDtypeStruct((L, 8, N), F32),
        compiler_params=_cparams("arbitrary", "arbitrary"),
        name="adaln_mod",
    )(c8, ada_w, ada_b.reshape(L, 1, N))
    return out.reshape(L, 8, 1, N)


def _inproj_kernel(x_ref, g_ref, sc_ref, sh_ref, w_ref, cos_ref, sa_ref, sb_ref,
                   rope_ref, bq_ref, v_ref, cmp_ref, small_ref, gdn_ref):
    h = _rms_mod(x_ref[...], g_ref[...], sc_ref[...], sh_ref[...]).astype(BF16)
    yr = _dot(h, w_ref[:, 0:W_ROPE])
    bq_ref[...] = yr[:, A_Q:A_Q + B_Q].astype(BF16)
    c, sa, sb = cos_ref[...], sa_ref[...], sb_ref[...]
    for g in range(W_ROPE // 128):
        xg = yr[:, 128 * g:128 * (g + 1)]
        rot = xg * c + pltpu.roll(xg, 96, 1) * sa + pltpu.roll(xg, 32, 1) * sb
        rope_ref[:, 128 * g:128 * (g + 1)] = rot.astype(BF16)
    o = W_ROPE
    yp = _dot(h, w_ref[:, o:o + W_V + W_CMP + W_SMALL])
    v_ref[...] = yp[:, 0:W_V].astype(BF16)
    cmp_ref[...] = yp[:, W_V:W_V + W_CMP]
    small_ref[...] = yp[:, W_V + W_CMP:W_V + W_CMP + W_SMALL]
    o += W_V + W_CMP + W_SMALL
    gdn_ref[...] = _dot(h, w_ref[:, o:o + W_GDN])


def _in_projection(x, gain, mod, w, cos_t, sa_t, sb_t, tm=512):
    B, S, D = x.shape
    row = lambda b, i: (b, i, 0)
    outs = pl.pallas_call(
        _inproj_kernel,
        grid=(B, S // tm),
        in_specs=[
            pl.BlockSpec((None, tm, D), row),
            pl.BlockSpec((1, D), lambda b, i: (0, 0)),
            pl.BlockSpec((None, 1, D), lambda b, i: (b, 0, 1)),
            pl.BlockSpec((None, 1, D), lambda b, i: (b, 0, 0)),
            pl.BlockSpec((D, W_TOTAL), lambda b, i: (0, 0)),
            pl.BlockSpec((tm, 128), lambda b, i: (i, 0)),
            pl.BlockSpec((tm, 128), lambda b, i: (i, 0)),
            pl.BlockSpec((tm, 128), lambda b, i: (i, 0)),
        ],
        out_specs=[
            pl.BlockSpec((None, tm, W_ROPE), row),
            pl.BlockSpec((None, tm, B_Q), row),
            pl.BlockSpec((None, tm, W_V), row),
            pl.BlockSpec((None, tm, W_CMP), row),
            pl.BlockSpec((None, tm, W_SMALL), row),
            pl.BlockSpec((None, tm, W_GDN), row),
        ],
        out_shape=[
            jax.ShapeDtypeStruct((B, S, W_ROPE), BF16),
            jax.ShapeDtypeStruct((B, S, B_Q), BF16),
            jax.ShapeDtypeStruct((B, S, W_V), BF16),
            jax.ShapeDtypeStruct((B, S, W_CMP), F32),
            jax.ShapeDtypeStruct((B, S, W_SMALL), F32),
            jax.ShapeDtypeStruct((B, S, W_GDN), F32),
        ],
        compiler_params=_cparams("arbitrary", "arbitrary"),
        name="in_projection",
    )(x, gain, mod, mod, w, cos_t, sa_t, sb_t)
    return outs


def _swa_kernel(sink_ref, q_ref, k_ref, vt_ref, o_ref, *, tq, window):
    q0 = pl.program_id(1) * tq
    span = tq + window
    start = pl.multiple_of(jnp.maximum(q0 - window, 0), KV_BLOCK)
    qt = q_ref[...].astype(F32).T * ATTN_SCALE
    kpos = start + _iota((span, tq), 0)
    qpos = q0 + _iota((span, tq), 1)
    bias = jnp.where((kpos <= qpos) & (kpos > qpos - window), 0.0, NEG_INF)
    vb = vt_ref[pl.ds(start // KV_BLOCK, span // KV_BLOCK)]
    group = A_HEADS // A_KV_HEADS
    first_head = _iota((1, group * tq), 1) < tq
    outs = []
    for g in range(A_KV_HEADS):
        heads = [group * g + r for r in range(group)]
        qg = jnp.concatenate([qt[HEAD_DIM * h:HEAD_DIM * (h + 1)] for h in heads], axis=1).astype(BF16)
        s = _dot(k_ref[pl.ds(start, span), HEAD_DIM * g:HEAD_DIM * (g + 1)], qg)
        s = s + jnp.concatenate([bias] * group, axis=1)
        sink = jnp.where(first_head, sink_ref[heads[0]], sink_ref[heads[1]])
        m = jnp.maximum(jnp.max(s, axis=0, keepdims=True), sink)
        p = jnp.exp(s - m)
        den = jnp.sum(p, axis=0, keepdims=True) + jnp.exp(sink - m)
        vt = jnp.concatenate([vb[r, HEAD_DIM * g:HEAD_DIM * (g + 1), :] for r in range(span // KV_BLOCK)], axis=1)
        o_t = _dot(vt, p.astype(BF16)) * (1.0 / den)
        outs += [o_t[:, tq * r:tq * (r + 1)] for r in range(group)]
    o_ref[...] = jnp.concatenate(outs, axis=0).T.astype(BF16)


def _swa_attention(rope, vals_t, sinks, tq=256):
    assert A_HEADS // A_KV_HEADS == 2
    B, S, _ = rope.shape
    return pl.pallas_call(
        functools.partial(_swa_kernel, tq=tq, window=A_WINDOW),
        grid=(B, S // tq),
        in_specs=[
            pl.BlockSpec(memory_space=pltpu.SMEM),
            pl.BlockSpec((None, tq, A_Q), lambda b, i: (b, i, 0)),
            pl.BlockSpec((None, S, A_KV), lambda b, i: (b, 0, (A_Q + B_Q) // A_KV)),
            pl.BlockSpec((None, S // KV_BLOCK, A_KV, KV_BLOCK), lambda b, i: (b, 0, 0, 0)),
        ],
        out_specs=pl.BlockSpec((None, tq, A_Q), lambda b, i: (b, i, 0)),
        out_shape=jax.ShapeDtypeStruct((B, S, A_Q), BF16),
        compiler_params=_cparams("arbitrary", "arbitrary"),
        name="swa_attention",
    )(sinks, rope, rope, vals_t)


def _compress_kernel(x_ref, pea_ref, peb_ref, w1a_ref, w1b_ref, w2_ref, o_ref, ot_ref):
    x = x_ref[...]
    n = x.shape[0]
    a = _dot((x + pea_ref[...]).astype(BF16), w1a_ref[...])
    b = _dot((x + peb_ref[...]).astype(BF16), w1b_ref[...])
    hid = _silu(a + pltpu.roll(b, n - 1, 0))
    out = _dot(hid.astype(BF16), w2_ref[...])
    o_ref[...] = out.astype(BF16)
    ot_ref[...] = out.T.astype(BF16)


def _compress(xc, pea, peb, w1a, w1b, w2):
    B, n, K = xc.shape
    full = lambda a: pl.BlockSpec(a.shape, lambda b: (0,) * a.ndim)
    return pl.pallas_call(
        _compress_kernel,
        grid=(B,),
        in_specs=[pl.BlockSpec((None, n, K), lambda b: (b, 0, 0)), full(pea), full(peb), full(w1a), full(w1b), full(w2)],
        out_specs=[pl.BlockSpec((None, n, 2 * HEAD_DIM), lambda b: (b, 0, 0)),
                   pl.BlockSpec((None, 2 * HEAD_DIM, n), lambda b: (b, 0, 0))],
        out_shape=[jax.ShapeDtypeStruct((B, n, 2 * HEAD_DIM), BF16), jax.ShapeDtypeStruct((B, 2 * HEAD_DIM, n), BF16)],
        compiler_params=_cparams("arbitrary"),
        name="nsa_compress",
    )(xc, pea, peb, w1a, w1b, w2)


def _heads_on_lanes(q):
    qt = q.astype(F32).T * ATTN_SCALE
    return jnp.concatenate([qt[HEAD_DIM * h:HEAD_DIM * (h + 1)] for h in range(B_HEADS)], axis=1).astype(BF16)


def _heads_on_sublanes(ot, tq):
    return jnp.concatenate([ot[:, tq * h:tq * (h + 1)] for h in range(B_HEADS)], axis=0)


def _cmp_select_kernel(q_ref, k_ref, vt_ref, pool_ref, o_ref, sel_ref, *, tq, top_k):
    q0 = pl.program_id(1) * tq
    H = B_HEADS
    n_cmp = k_ref.shape[0]
    n_slc = pool_ref.shape[0]
    qt = _heads_on_lanes(q_ref[...])
    s = _dot(k_ref[:, 0:HEAD_DIM], qt)
    valid = _iota((n_cmp, tq), 0) * CMP_STRIDE + (CMP_LEN - 1) <= q0 + _iota((n_cmp, tq), 1)
    valid = jnp.concatenate([valid] * H, axis=1)
    s = jnp.where(valid, s, NEG_INF)
    e = jnp.where(valid, jnp.exp(s - jnp.max(s, axis=0, keepdims=True)), 0.0)
    den = jnp.sum(e, axis=0, keepdims=True)
    p = e * (1.0 / jnp.where(den > 0.0, den, 1.0))
    o_t = _dot(vt_ref[HEAD_DIM:2 * HEAD_DIM, :], p.astype(BF16))
    o_ref[...] = _heads_on_sublanes(o_t, tq)
    psum = p[:, 0:tq]
    for h in range(1, H):
        psum = psum + p[:, tq * h:tq * (h + 1)]

    imp = _dot(pool_ref[...], psum, HIGHEST)
    blk = _iota((n_slc, tq), 0)
    tt = q0 + _iota((n_slc, tq), 1)
    cur = tt >> 6
    forced = (blk == 0) | (blk == cur) | (blk == cur - 1)
    causal = blk * SLC_LEN <= tt
    score = jnp.where(forced, SLC_FORCED_SCORE, jnp.where(causal, imp, NEG_INF))
    sel = jnp.zeros((n_slc, tq), F32)
    blk_f = blk.astype(F32)
    for _ in range(top_k):
        mx = jnp.max(score, axis=0, keepdims=True)
        first = jnp.min(jnp.where(score == mx, blk_f, float(n_slc)), axis=0, keepdims=True)
        pick = blk_f == first
        sel = jnp.where(pick, 1.0, sel)
        score = jnp.where(pick, -jnp.inf, score)
    sel_ref[...] = jnp.where(causal, sel, 0.0)


def _cmp_select(bq, kv_cmp, kv_cmp_t, pool_t, tq=256):
    B, S, _ = bq.shape
    n_slc, n_cmp = pool_t.shape
    top_k = min(SLC_TOPK, n_slc)
    return pl.pallas_call(
        functools.partial(_cmp_select_kernel, tq=tq, top_k=top_k),
        grid=(B, S // tq),
        in_specs=[
            pl.BlockSpec((None, tq, B_Q), lambda b, i: (b, i, 0)),
            pl.BlockSpec((None, n_cmp, 2 * HEAD_DIM), lambda b, i: (b, 0, 0)),
            pl.BlockSpec((None, 2 * HEAD_DIM, n_cmp), lambda b, i: (b, 0, 0)),
            pl.BlockSpec((n_slc, n_cmp), lambda b, i: (0, 0)),
        ],
        out_specs=[
            pl.BlockSpec((None, B_Q, tq), lambda b, i: (b, 0, i)),
            pl.BlockSpec((None, n_slc, tq), lambda b, i: (b, 0, i)),
        ],
        out_shape=[jax.ShapeDtypeStruct((B, B_Q, S), F32), jax.ShapeDtypeStruct((B, n_slc, S), F32)],
        compiler_params=_cparams("arbitrary", "arbitrary"),
        name="nsa_cmp_select",
    )(bq, kv_cmp, kv_cmp_t, pool_t)


KV_BLOCK = 128
PAD_ROWS = 16
SLC_TILE = 512


def _values_with_ones(vb, row0, n):
    vt = jnp.concatenate([vb[r, row0:row0 + HEAD_DIM, :] for r in range(n)], axis=1)
    return jnp.concatenate([vt, jnp.ones((PAD_ROWS, n * KV_BLOCK), BF16)], axis=0)


def _nsa_attn_kernel(q_ref, k_ref, ka_ref, vt_ref, sel_ref, ocmp_ref, small_ref, o_ref, *, tq, tk, window):
    i = pl.program_id(1)
    q0 = i * tq
    H = B_HEADS
    d = HEAD_DIM
    qt = _heads_on_lanes(q_ref[...])
    bpt = tk // SLC_LEN
    vpt = tk // KV_BLOCK
    q_pad = jnp.zeros((ka_ref.shape[1] - d - PAD_ROWS, H * tq), BF16)
    pen_pad = jnp.zeros((PAD_ROWS - bpt, tq), F32)

    def scores(j):
        sel = sel_ref[pl.ds(pl.multiple_of(j * bpt, bpt), bpt), :]
        pen = jnp.concatenate([jnp.where(sel > 0.5, 0.0, NEG_INF), pen_pad], axis=0)
        q_aug = jnp.concatenate([qt, jnp.concatenate([pen] * H, axis=1).astype(BF16), q_pad], axis=0)
        return _dot(ka_ref[pl.ds(pl.multiple_of(j * tk, tk), tk), :], q_aug)

    def update(j, s, carry):
        m, acc = carry
        m_new = jnp.maximum(m, jnp.max(s, axis=0, keepdims=True))
        alpha = jnp.exp(m - m_new)
        p = jnp.exp(s - m_new)
        vb = vt_ref[pl.ds(pl.multiple_of(j * vpt, vpt), vpt)]
        return m_new, acc * alpha + _dot(_values_with_ones(vb, 0, vpt), p.astype(BF16))

    def full_pair(jp, carry):
        s0 = scores(2 * jp)
        s1 = scores(2 * jp + 1)
        return update(2 * jp + 1, s1, update(2 * jp, s0, carry))

    qpos = q0 + _iota((tk, tq), 1)

    def causal_tile(j, carry):
        causal = jnp.where(j * tk + _iota((tk, tq), 0) <= qpos, 0.0, NEG_INF)
        return update(j, scores(j) + jnp.concatenate([causal] * H, axis=1), carry)

    n_full_pairs = q0 // (2 * tk)
    n_tiles = (q0 + tq + tk - 1) // tk
    carry = (jnp.full((1, H * tq), NEG_INF, F32), jnp.zeros((d + PAD_ROWS, H * tq), F32))
    carry = lax.fori_loop(0, n_full_pairs, full_pair, carry)
    _, acc = lax.fori_loop(2 * n_full_pairs, n_tiles, causal_tile, carry)
    o_slc = _heads_on_sublanes(acc[0:d] * (1.0 / acc[d:d + 1]), tq)

    span = tq + window
    start = pl.multiple_of(jnp.maximum(q0 - window, 0), KV_BLOCK)
    s = _dot(k_ref[pl.ds(start, span), d:2 * d], qt)
    kpos = start + _iota((span, tq), 0)
    qp = q0 + _iota((span, tq), 1)
    bias = jnp.where((kpos <= qp) & (kpos > qp - window), 0.0, NEG_INF)
    s = s + jnp.concatenate([bias] * H, axis=1)
    p = jnp.exp(s - jnp.max(s, axis=0, keepdims=True))
    vb = vt_ref[pl.ds(start // KV_BLOCK, span // KV_BLOCK)]
    res = _dot(_values_with_ones(vb, d, span // KV_BLOCK), p.astype(BF16))
    o_win = _heads_on_sublanes(res[0:d] * (1.0 / res[d:d + 1]), tq)

    gates = jax.nn.sigmoid(small_ref[...]).T
    o_cmp = ocmp_ref[...]
    outs = []
    for h in range(H):
        rows = slice(HEAD_DIM * h, HEAD_DIM * (h + 1))
        outs.append(gates[h:h + 1, :] * o_cmp[rows] + gates[H + h:H + h + 1, :] * o_slc[rows]
                    + gates[2 * H + h:2 * H + h + 1, :] * o_win[rows])
    o_ref[...] = jnp.concatenate(outs, axis=0).T.astype(BF16)


def _slc_keys_with_block_onehot(rope):
    B, S, _ = rope.shape
    t = np.arange(S)[:, None]
    onehot = (t // SLC_LEN) % (SLC_TILE // SLC_LEN) == np.arange(HEAD_DIM)[None, :]
    onehot = jnp.broadcast_to(jnp.asarray(onehot.astype(np.float32)).astype(BF16), (B, S, HEAD_DIM))
    off = A_Q + B_Q + A_KV
    return jnp.concatenate([rope[:, :, off:off + HEAD_DIM], onehot], axis=-1)


def _nsa_attention(rope, vals_t, sel, o_cmp, small, tq=256):
    tk = SLC_TILE
    B, S, _ = rope.shape
    n_slc = sel.shape[1]
    keys_aug = _slc_keys_with_block_onehot(rope)
    return pl.pallas_call(
        functools.partial(_nsa_attn_kernel, tq=tq, tk=tk, window=NSA_WINDOW),
        grid=(B, S // tq),
        in_specs=[
            pl.BlockSpec((None, tq, B_Q), lambda b, i: (b, i, A_Q // B_Q)),
            pl.BlockSpec((None, S, 128), lambda b, i: (b, 0, (A_Q + B_Q + A_KV) // 128)),
            pl.BlockSpec((None, S, 2 * HEAD_DIM), lambda b, i: (b, 0, 0)),
            pl.BlockSpec((None, S // KV_BLOCK, 2 * HEAD_DIM, KV_BLOCK), lambda b, i: (b, 0, A_KV // (2 * HEAD_DIM), 0)),
            pl.BlockSpec((None, n_slc, tq), lambda b, i: (b, 0, i)),
            pl.BlockSpec((None, B_Q, tq), lambda b, i: (b, 0, i)),
            pl.BlockSpec((None, tq, W_SMALL), lambda b, i: (b, i, 0)),
        ],
        out_specs=pl.BlockSpec((None, tq, B_Q), lambda b, i: (b, i, 0)),
        out_shape=jax.ShapeDtypeStruct((B, S, B_Q), BF16),
        compiler_params=_cparams("arbitrary", "arbitrary"),
        name="nsa_attention",
    )(rope, rope, keys_aug, vals_t, sel, o_cmp, small)


def _gdn_kernel(xq_ref, xk_ref, xv_ref, hq_ref, hk_ref, hv_ref, z_ref, small_ref, arow_ref,
                cw_ref, alog_row_ref, dtb_row_ref, alog_col_ref, dtb_col_ref, nw_ref,
                o_ref, q_s, k_s, v_s, g_s, b_s, o_s, state_s, u_s, wq_s, qk_s, kdt_s, gl_s, pq_s, pk_s, pv_s, *, tt):
    i = pl.program_id(1)
    C = GDN_CHUNK
    dk = C_HEAD_DIM
    H = C_HEADS

    @pl.when(i == 0)
    def _():
        state_s[...] = jnp.zeros_like(state_s)

    def conv(x_ref, halo_ref, pad_s, col0):
        pad_s[0:8, :] = jnp.where(i > 0, halo_ref[...], 0.0)
        pad_s[8:8 + tt, :] = x_ref[...]
        w = cw_ref[:, col0:col0 + C_QK]
        y = x_ref[...] * w[3:4]
        for kback in range(1, CONV_WIDTH):
            y = y + pad_s[8 - kback:8 - kback + tt, :] * w[3 - kback:4 - kback]
        return _silu(y)

    qf = conv(xq_ref, hq_ref, pq_s, 0)
    kf = conv(xk_ref, hk_ref, pk_s, C_QK)
    v_s[...] = conv(xv_ref, hv_ref, pv_s, 2 * C_QK)
    for h in range(H):
        cols = slice(dk * h, dk * (h + 1))
        qh = qf[:, cols]
        kh = kf[:, cols]
        q_s[:, cols] = qh * lax.rsqrt(jnp.sum(qh * qh, axis=-1, keepdims=True) + NORM_EPS) * (dk ** -0.5)
        k_s[:, cols] = kh * lax.rsqrt(jnp.sum(kh * kh, axis=-1, keepdims=True) + NORM_EPS)
    small = small_ref[...]
    g_s[...] = -jnp.exp(alog_row_ref[...]) * _softplus(small + dtb_row_ref[...])
    b_s[...] = jax.nn.sigmoid(small)

    r_i = _iota((C, C), 0)
    c_i = _iota((C, C), 1)
    lower = r_i >= c_i
    strict = r_i > c_i
    tri_l = lower.astype(F32)
    tri_u = (r_i <= c_i).astype(F32)
    neg_alog_col = -jnp.exp(alog_col_ref[...])
    dtb_col = dtb_col_ref[...]

    HC = H * C
    rr = _iota((HC, HC), 0)
    cc = _iota((HC, HC), 1)
    same_head = (rr >> 6) == (cc >> 6)
    lower_bd = same_head & (rr >= cc)
    strict_bd = same_head & (rr > cc)
    state_bd = (_iota((HC, H * dk), 0) >> 6) == (_iota((HC, H * dk), 1) >> 7)

    def stack_rows(ref, rows):
        return jnp.concatenate([ref[rows, dk * h:dk * (h + 1)] for h in range(H)], axis=0)

    def prepare(c):
        r0 = pl.multiple_of(c * C, C)
        rows = pl.ds(r0, C)
        gc_col = _dot(tri_l, g_s[rows, :], HIGHEST)
        g_row = neg_alog_col * _softplus(arow_ref[c] + dtb_col)
        gc_row = _dot(g_row, tri_u, HIGHEST)
        beta_all = b_s[rows, :]
        gcol = jnp.concatenate([gc_col[:, SMALL_A + h:SMALL_A + h + 1] for h in range(H)], axis=0)
        grow = jnp.concatenate([gc_row[h:h + 1, :] for h in range(H)], axis=1)
        beta = jnp.concatenate([beta_all[:, SMALL_B + h:SMALL_B + h + 1] for h in range(H)], axis=0)
        g_last = [gc_col[C - 1:C, SMALL_A + h:SMALL_A + h + 1] for h in range(H)]
        glast = jnp.concatenate([jnp.broadcast_to(g, (C, 1)) for g in g_last], axis=0)
        q = stack_rows(q_s, rows)
        k = stack_rows(k_s, rows)
        v = stack_rows(v_s, rows)
        decay = jnp.where(lower_bd, jnp.exp(jnp.where(lower_bd, gcol - grow, 0.0)), 0.0)
        eg = jnp.exp(gcol)
        kb = k * beta
        gram = _dot_nt(jnp.concatenate([kb, q], axis=0).astype(BF16), k.astype(BF16))
        a_mat = jnp.where(strict_bd, gram[0:HC] * decay, 0.0)
        qk = jnp.where(lower_bd, gram[HC:2 * HC] * decay, 0.0)
        pw = -a_mat
        x = pw
        pw16 = pw.astype(BF16)
        pw = _dot(pw16, pw16)
        for _ in range(4):
            pw16 = pw.astype(BF16)
            x, pw = x + pw + _dot(pw16, x.astype(BF16)), _dot(pw16, pw16)
        x = x + pw + _dot(pw.astype(BF16), x.astype(BF16))
        rhs = jnp.concatenate([v * beta, kb * eg], axis=1)
        y = rhs + _dot(x.astype(BF16), rhs.astype(BF16))
        kd = k * jnp.exp(glast - gcol)
        u_s[c] = y[:, 0:dk]
        wq_s[c, 0:HC, :] = y[:, dk:2 * dk].astype(BF16)
        wq_s[c, HC:2 * HC, :] = (q * eg).astype(BF16)
        qk_s[c] = qk.astype(BF16)
        kdt_s[c] = kd.T.astype(BF16)
        gl_s[c, 0:1, :] = jnp.concatenate([jnp.broadcast_to(jnp.exp(g), (1, dk)) for g in g_last], axis=1)

    def prepare_group(cg, carry):
        for r in range(PREP_GROUP):
            prepare(PREP_GROUP * cg + r)
        return carry

    lax.fori_loop(0, tt // (PREP_GROUP * C), prepare_group, 0)

    def head_blocks(x, row0):
        return jnp.concatenate([x[row0 + C * h:row0 + C * (h + 1), dk * h:dk * (h + 1)] for h in range(H)], axis=0)

    def scan(c, carry):
        r0 = pl.multiple_of(c * C, C)
        rows = pl.ds(r0, C)
        st = state_s[...]
        d1 = _dot(wq_s[c], st.astype(BF16))
        v_new = u_s[c] - head_blocks(d1, 0)
        v16 = v_new.astype(BF16)
        o = head_blocks(d1, HC) + _dot(qk_s[c], v16)
        v_bd = jnp.where(state_bd, jnp.concatenate([v16] * H, axis=1), jnp.zeros((), BF16))
        state_s[...] = st * gl_s[c, 0:1, :] + _dot(kdt_s[c], v_bd)
        for h in range(H):
            o_s[rows, dk * h:dk * (h + 1)] = o[C * h:C * (h + 1)]
        return carry

    lax.fori_loop(0, tt // C, scan, 0)

    nw = nw_ref[...]
    z = z_ref[...]
    for h in range(H):
        cols = slice(dk * h, dk * (h + 1))
        oh = o_s[:, cols]
        y = oh * lax.rsqrt(jnp.mean(oh * oh, axis=-1, keepdims=True) + NORM_EPS) * nw
        o_ref[:, cols] = (y * _silu(z[:, cols])).astype(BF16)


def _gated_delta_net(gdn, small, a_rows, conv_w, alog_row, dtb_row, alog_col, dtb_col, norm_w, tt=512):
    B, S, _ = gdn.shape
    nch = tt // GDN_CHUNK
    hc = C_HEADS * GDN_CHUNK
    part = lambda j: pl.BlockSpec((None, tt, C_QK), lambda b, i: (b, i, j))
    halo = lambda j: pl.BlockSpec((None, 8, C_QK), lambda b, i: (b, jnp.maximum(i * (tt // 8) - 1, 0), j))
    full = lambda a: pl.BlockSpec(a.shape, lambda b, i: (0,) * a.ndim)
    return pl.pallas_call(
        functools.partial(_gdn_kernel, tt=tt),
        grid=(B, S // tt),
        in_specs=[part(0), part(1), part(2), halo(0), halo(1), halo(2), part(3),
                  pl.BlockSpec((None, tt, W_SMALL), lambda b, i: (b, i, 0)),
                  pl.BlockSpec((None, nch, 8, GDN_CHUNK), lambda b, i: (b, i, 0, 0)),
                  full(conv_w), full(alog_row), full(dtb_row), full(alog_col), full(dtb_col), full(norm_w)],
        out_specs=pl.BlockSpec((None, tt, C_QK), lambda b, i: (b, i, 0)),
        out_shape=jax.ShapeDtypeStruct((B, S, C_QK), BF16),
        scratch_shapes=[pltpu.VMEM((tt, C_QK), F32), pltpu.VMEM((tt, C_QK), F32), pltpu.VMEM((tt, C_QK), F32),
                        pltpu.VMEM((tt, W_SMALL), F32), pltpu.VMEM((tt, W_SMALL), F32),
                        pltpu.VMEM((tt, C_QK), F32), pltpu.VMEM((C_HEAD_DIM, C_QK), F32),
                        pltpu.VMEM((nch, hc, C_HEAD_DIM), F32), pltpu.VMEM((nch, 2 * hc, C_HEAD_DIM), BF16),
                        pltpu.VMEM((nch, hc, hc), BF16), pltpu.VMEM((nch, C_HEAD_DIM, hc), BF16),
                        pltpu.VMEM((nch, 8, C_QK), F32)] + [pltpu.VMEM((tt + 8, C_QK), F32)] * 3,
        compiler_params=_cparams("arbitrary", "arbitrary"),
        name="gated_delta_net",
    )(gdn, gdn, gdn, gdn, gdn, gdn, gdn, small, a_rows, conv_w, alog_row, dtb_row, alog_col, dtb_col, norm_w)


def _outproj_kernel(x_ref, oa_ref, ob_ref, oc_ref, w_ref, gt_ref, o_ref):
    y = _dot(oa_ref[...], w_ref[0:A_Q, :])
    y = y + _dot(ob_ref[...], w_ref[A_Q:A_Q + B_Q, :])
    y = y + _dot(oc_ref[...], w_ref[A_Q + B_Q:, :])
    o_ref[...] = x_ref[...] + gt_ref[...] * y


def _out_projection(x, o_a, o_b, o_c, w, mod, tm=1024):
    B, S, D = x.shape
    row = lambda b, i: (b, i, 0)
    return pl.pallas_call(
        _outproj_kernel,
        grid=(B, S // tm),
        in_specs=[
            pl.BlockSpec((None, tm, D), row),
            pl.BlockSpec((None, tm, A_Q), row),
            pl.BlockSpec((None, tm, B_Q), row),
            pl.BlockSpec((None, tm, C_QK), row),
            pl.BlockSpec((D, D), lambda b, i: (0, 0)),
            pl.BlockSpec((None, 1, D), lambda b, i: (b, 0, 2)),
        ],
        out_specs=pl.BlockSpec((None, tm, D), row),
        out_shape=jax.ShapeDtypeStruct((B, S, D), F32),
        compiler_params=_cparams("arbitrary", "arbitrary"),
        name="out_projection",
    )(x, o_a, o_b, o_c, w, mod)


def _ffn_kernel(x_ref, g_ref, sc_ref, sh_ref, gt_ref, wgu_ref, wd_ref, o_ref, *, tf):
    x = x_ref[...]
    h = _rms_mod(x, g_ref[...], sc_ref[...], sh_ref[...]).astype(BF16)
    acc = None
    for f in range(D_FF // tf):
        gate = _dot(h, wgu_ref[:, tf * f:tf * (f + 1)])
        up = _dot(h, wgu_ref[:, D_FF + tf * f:D_FF + tf * (f + 1)])
        part = _dot((_silu(gate) * up).astype(BF16), wd_ref[tf * f:tf * (f + 1), :])
        acc = part if acc is None else acc + part
    o_ref[...] = x + gt_ref[...] * acc


def _ffn(x, gain, mod, w_gate_up, w_down, tm=512, tf=1408):
    B, S, D = x.shape
    row = lambda b, i: (b, i, 0)
    resident = lambda a: pl.BlockSpec(a.shape, lambda b, i: (0, 0), pipeline_mode=pl.Buffered(1))
    return pl.pallas_call(
        functools.partial(_ffn_kernel, tf=tf),
        grid=(B, S // tm),
        in_specs=[
            pl.BlockSpec((None, tm, D), row),
            pl.BlockSpec((1, D), lambda b, i: (0, 0)),
            pl.BlockSpec((None, 1, D), lambda b, i: (b, 0, 4)),
            pl.BlockSpec((None, 1, D), lambda b, i: (b, 0, 3)),
            pl.BlockSpec((None, 1, D), lambda b, i: (b, 0, 5)),
            resident(w_gate_up),
            resident(w_down),
        ],
        out_specs=pl.BlockSpec((None, tm, D), row),
        out_shape=jax.ShapeDtypeStruct((B, S, D), F32),
        compiler_params=_cparams("arbitrary", "arbitrary"),
        name="swiglu_ffn",
    )(x, gain, mod, mod, mod, w_gate_up, w_down)


def _final_norm_kernel(x_ref, g_ref, o_ref):
    x = x_ref[...]
    o_ref[...] = x * lax.rsqrt(jnp.mean(x * x, axis=-1, keepdims=True) + NORM_EPS) * g_ref[...]


def _final_norm(x, gain, tm=512):
    B, S, D = x.shape
    return pl.pallas_call(
        _final_norm_kernel,
        grid=(B, S // tm),
        in_specs=[pl.BlockSpec((None, tm, D), lambda b, i: (b, i, 0)), pl.BlockSpec((1, D), lambda b, i: (0, 0))],
        out_specs=pl.BlockSpec((None, tm, D), lambda b, i: (b, i, 0)),
        out_shape=jax.ShapeDtypeStruct((B, S, D), F32),
        compiler_params=_cparams("arbitrary", "arbitrary"),
        name="final_norm",
    )(x, gain)


def _rope_tables(seq):
    inv = 1.0 / (ROPE_THETA ** (jnp.arange(0, HEAD_DIM, 2, dtype=F32) / HEAD_DIM))
    ang = jnp.arange(seq, dtype=F32)[:, None] * inv[None, :]
    cos, sin = jnp.cos(ang), jnp.sin(ang)
    zero = jnp.zeros_like(sin)
    cos_t = jnp.tile(cos, (1, 4))
    sa_t = jnp.tile(jnp.concatenate([-sin, zero], axis=1), (1, 2))
    sb_t = jnp.tile(jnp.concatenate([zero, sin], axis=1), (1, 2))
    return cos_t, sa_t, sb_t


def _permute_w_in(w_in):
    pts = np.cumsum(IN_SPLITS)[:-1].tolist()
    aq, ak, av, bq, bkc, bvc, bks, bvs, bkw, bvw, bg, cqkv, cz, ca, cb = jnp.split(w_in, pts, axis=-1)
    pad = jnp.zeros(w_in.shape[:-1] + (W_SMALL - B_GATES - 2 * C_HEADS,), w_in.dtype)
    return jnp.concatenate([aq, bq, ak, bks, bkw, av, bvs, bvw, bkc, bvc, bg, ca, cb, pad, cqkv, cz], axis=-1).astype(BF16)


def _compress_weights(k_w1, k_w2, v_w1, v_w2, pe_k, pe_v):
    L = k_w1.shape[0]
    half = CMP_STRIDE * HEAD_DIM

    def first_layer(lo):
        wk = k_w1[:, lo:lo + half].reshape(L, CMP_STRIDE, HEAD_DIM, CMP_HIDDEN)
        wv = v_w1[:, lo:lo + half].reshape(L, CMP_STRIDE, HEAD_DIM, CMP_HIDDEN)
        z = jnp.zeros_like(wk)
        w = jnp.concatenate([jnp.concatenate([wk, z], axis=-1), jnp.concatenate([z, wv], axis=-1)], axis=2)
        return w.reshape(L, 2 * half, 2 * CMP_HIDDEN).astype(BF16)

    def pe_rows(lo):
        return jnp.concatenate([pe_k[:, lo:lo + CMP_STRIDE], pe_v[:, lo:lo + CMP_STRIDE]], axis=-1).reshape(L, 1, 2 * half)

    z2 = jnp.zeros_like(k_w2)
    w2 = jnp.concatenate([jnp.concatenate([k_w2, z2], axis=-1), jnp.concatenate([z2, v_w2], axis=-1)], axis=1).astype(BF16)
    return pe_rows(0), pe_rows(CMP_STRIDE), first_layer(0), first_layer(half), w2


def _pool_matrix(n_cmp_rows, n_slc):
    ratio = SLC_LEN // CMP_STRIDE
    n = np.arange(n_cmp_rows)[:, None]
    j = np.arange(n_slc)[None, :]
    return jnp.asarray(((n // ratio == j) | (n == ratio * j - 1)).astype(np.float32))


def _lane_row(v, lane0):
    L, n = v.shape
    return jnp.zeros((L, 1, 128), F32).at[:, 0, lane0:lane0 + n].set(v)


def _sublane_col(v):
    L, n = v.shape
    return jnp.zeros((L, 8, 1), F32).at[:, :n, 0].set(v)


def kernel(x, c, norm_mix, norm_ffn, ada_w, ada_b, w_in, attn_sinks, cmp_k_w1, cmp_k_w2, cmp_v_w1, cmp_v_w2, cmp_pe_k, cmp_pe_v, gdn_conv_w, gdn_A_log, gdn_dt_bias, gdn_norm, w_out, w_gate_up, w_down, final_norm):
    B, S, D = x.shape
    L = w_in.shape[0]
    n16 = S // CMP_STRIDE
    n_slc = S // SLC_LEN

    cos_t, sa_t, sb_t = _rope_tables(S)
    mod = _adaln_mod(c, ada_w, ada_b)
    pea, peb, w1a, w1b, w2c = _compress_weights(cmp_k_w1, cmp_k_w2, cmp_v_w1, cmp_v_w2, cmp_pe_k, cmp_pe_v)
    pool_t = _pool_matrix(n16, n_slc).T
    per_layer = dict(
        mod=mod,
        norm_mix=norm_mix.reshape(L, 1, D), norm_ffn=norm_ffn.reshape(L, 1, D),
        w_in=_permute_w_in(w_in), sinks=attn_sinks,
        pea=pea, peb=peb, w1a=w1a, w1b=w1b, w2c=w2c,
        conv_w=gdn_conv_w,
        alog_row=_lane_row(gdn_A_log, SMALL_A), dtb_row=_lane_row(gdn_dt_bias, SMALL_A),
        alog_col=_sublane_col(gdn_A_log), dtb_col=_sublane_col(gdn_dt_bias),
        gdn_norm=gdn_norm.reshape(L, 1, C_HEAD_DIM),
        w_out=w_out.astype(BF16), w_gate_up=w_gate_up.astype(BF16), w_down=w_down.astype(BF16),
    )

    def layer(xc, p):
        rope, bq, vals, cmp_in, small, gdn = _in_projection(xc, p["norm_mix"], p["mod"], p["w_in"], cos_t, sa_t, sb_t)
        vals_all_t = vals.reshape(B, S // KV_BLOCK, KV_BLOCK, W_V).transpose(0, 1, 3, 2)
        o_a = _swa_attention(rope, vals_all_t, p["sinks"])
        kv_cmp, kv_cmp_t = _compress(cmp_in.reshape(B, n16, CMP_STRIDE * W_CMP), p["pea"], p["peb"], p["w1a"], p["w1b"], p["w2c"])
        o_cmp_t, sel = _cmp_select(bq, kv_cmp, kv_cmp_t, pool_t)
        o_b = _nsa_attention(rope, vals_all_t, sel, o_cmp_t, small)
        a_rows = small[:, :, SMALL_A:SMALL_A + 8].reshape(B, S // GDN_CHUNK, GDN_CHUNK, 8).transpose(0, 1, 3, 2)
        o_c = _gated_delta_net(gdn, small, a_rows, p["conv_w"], p["alog_row"], p["dtb_row"], p["alog_col"], p["dtb_col"], p["gdn_norm"])
        xc = _out_projection(xc, o_a, o_b, o_c, p["w_out"], p["mod"])
        xc = _ffn(xc, p["norm_ffn"], p["mod"], p["w_gate_up"], p["w_down"])
        return xc, None

    x, _ = lax.scan(layer, x, per_layer)
    return _final_norm(x, final_norm.reshape(1, D))
```

```python
import functools
import math

import numpy as np
import jax
import jax.numpy as jnp
from jax import lax
from jax.experimental import pallas as pl
from jax.experimental.pallas import tpu as pltpu

F32 = jnp.float32
BF16 = jnp.bfloat16
HIGHEST = lax.Precision.HIGHEST

D_MODEL = 1024
DEPTH = 4
HEAD_DIM = 64
ATTN_SCALE = HEAD_DIM ** -0.5
ROPE_THETA = 10000.0
NEG_INF = -1e30
NORM_EPS = 1e-6

A_HEADS = 4
A_KV_HEADS = 2
A_WINDOW = 128

B_HEADS = 4
CMP_STRIDE = 16
CMP_LEN = 32
CMP_HIDDEN = 256
SLC_LEN = 64
SLC_TOPK = 16
NSA_WINDOW = 512
SLC_FORCED_SCORE = 1e9

C_HEAD_DIM = 128
C_HEADS = 4
CONV_WIDTH = 4
GDN_CHUNK = 64
C_QK = C_HEADS * C_HEAD_DIM

D_FF = 2816

A_Q = A_HEADS * HEAD_DIM
A_KV = A_KV_HEADS * HEAD_DIM
B_Q = B_HEADS * HEAD_DIM
B_KV = HEAD_DIM
B_GATES = 3 * B_HEADS
IN_SPLITS = (A_Q, A_KV, A_KV, B_Q, B_KV, B_KV, B_KV, B_KV, B_KV, B_KV, B_GATES, 3 * C_QK, C_QK, C_HEADS, C_HEADS)

W_ROPE = 768
W_V = 256
W_CMP = 128
W_SMALL = 128
W_GDN = 2048
W_TOTAL = W_ROPE + W_V + W_CMP + W_SMALL + W_GDN
SMALL_A = B_GATES
SMALL_B = B_GATES + C_HEADS

VMEM_LIMIT = 56 * 1024 * 1024
PREP_GROUP = 8
GDN_GROUP = 2

NT_DIMS = (((1,), (1,)), ((), ()))


def _cparams(*sem):
    return pltpu.CompilerParams(dimension_semantics=sem, vmem_limit_bytes=VMEM_LIMIT)


def _iota(shape, dim):
    return lax.broadcasted_iota(jnp.int32, shape, dim)


def _dot(a, b, precision=None):
    return jnp.dot(a, b, preferred_element_type=F32, precision=precision)


def _dot_nt(a, b):
    return lax.dot_general(a, b, NT_DIMS, preferred_element_type=F32)


def _silu(x):
    return x * jax.nn.sigmoid(x)


def _softplus(x):
    return jnp.maximum(x, 0.0) + jnp.log(1.0 + jnp.exp(-jnp.abs(x)))


def _rms_mod(x, gain, sc, sh):
    y = x * lax.rsqrt(jnp.mean(x * x, axis=-1, keepdims=True) + NORM_EPS)
    return (y * gain) * (1.0 + sc) + sh


def _mod_kernel(c_ref, w_ref, b_ref, o_ref):
    o_ref[...] = _dot(_silu(c_ref[...]), w_ref[...], HIGHEST) + b_ref[...]


def _adaln_mod(c, ada_w, ada_b):
    L, D, N = ada_w.shape
    tn = 1536
    c8 = jnp.zeros((8, D), F32).at[: c.shape[0]].set(c)
    out = pl.pallas_call(
        _mod_kernel,
        grid=(L, N // tn),
        in_specs=[
            pl.BlockSpec((8, D), lambda l, j: (0, 0)),
            pl.BlockSpec((None, D, tn), lambda l, j: (l, 0, j)),
            pl.BlockSpec((None, 1, tn), lambda l, j: (l, 0, j)),
        ],
        out_specs=pl.BlockSpec((None, 8, tn), lambda l, j: (l, 0, j)),
        out_shape=jax.ShapeDtypeStruct((L, 8, N), F32),
        compiler_params=_cparams("arbitrary", "arbitrary"),
        name="adaln_mod",
    )(c8, ada_w, ada_b.reshape(L, 1, N))
    return out.reshape(L, 8, 1, N)


def _inproj_kernel(x_ref, g_ref, sc_ref, sh_ref, w_ref, cos_ref, sa_ref, sb_ref,
                   rope_ref, bq_ref, v_ref, cmp_ref, small_ref, gdn_ref):
    h = _rms_mod(x_ref[...], g_ref[...], sc_ref[...], sh_ref[...]).astype(BF16)
    yr = _dot(h, w_ref[:, 0:W_ROPE])
    bq_ref[...] = yr[:, A_Q:A_Q + B_Q].astype(BF16)
    c, sa, sb = cos_ref[...], sa_ref[...], sb_ref[...]
    for g in range(W_ROPE // 128):
        xg = yr[:, 128 * g:128 * (g + 1)]
        rot = xg * c + pltpu.roll(xg, 96, 1) * sa + pltpu.roll(xg, 32, 1) * sb
        rope_ref[:, 128 * g:128 * (g + 1)] = rot.astype(BF16)
    o = W_ROPE
    yp = _dot(h, w_ref[:, o:o + W_V + W_CMP + W_SMALL])
    v_ref[...] = yp[:, 0:W_V].astype(BF16)
    cmp_ref[...] = yp[:, W_V:W_V + W_CMP]
    small_ref[...] = yp[:, W_V + W_CMP:W_V + W_CMP + W_SMALL]
    o += W_V + W_CMP + W_SMALL
    gdn_ref[...] = _dot(h, w_ref[:, o:o + W_GDN])


def _in_projection(x, gain, mod, w, cos_t, sa_t, sb_t, tm=512):
    B, S, D = x.shape
    row = lambda b, i: (b, i, 0)
    outs = pl.pallas_call(
        _inproj_kernel,
        grid=(B, S // tm),
        in_specs=[
            pl.BlockSpec((None, tm, D), row),
            pl.BlockSpec((1, D), lambda b, i: (0, 0)),
            pl.BlockSpec((None, 1, D), lambda b, i: (b, 0, 1)),
            pl.BlockSpec((None, 1, D), lambda b, i: (b, 0, 0)),
            pl.BlockSpec((D, W_TOTAL), lambda b, i: (0, 0)),
            pl.BlockSpec((tm, 128), lambda b, i: (i, 0)),
            pl.BlockSpec((tm, 128), lambda b, i: (i, 0)),
            pl.BlockSpec((tm, 128), lambda b, i: (i, 0)),
        ],
        out_specs=[
            pl.BlockSpec((None, tm, W_ROPE), row),
            pl.BlockSpec((None, tm, B_Q), row),
            pl.BlockSpec((None, tm, W_V), row),
            pl.BlockSpec((None, tm, W_CMP), row),
            pl.BlockSpec((None, tm, W_SMALL), row),
            pl.BlockSpec((None, tm, W_GDN), row),
        ],
        out_shape=[
            jax.ShapeDtypeStruct((B, S, W_ROPE), BF16),
            jax.ShapeDtypeStruct((B, S, B_Q), BF16),
            jax.ShapeDtypeStruct((B, S, W_V), BF16),
            jax.ShapeDtypeStruct((B, S, W_CMP), F32),
            jax.ShapeDtypeStruct((B, S, W_SMALL), F32),
            jax.ShapeDtypeStruct((B, S, W_GDN), F32),
        ],
        compiler_params=_cparams("arbitrary", "arbitrary"),
        name="in_projection",
    )(x, gain, mod, mod, w, cos_t, sa_t, sb_t)
    return outs


def _swa_kernel(sink_ref, q_ref, k_ref, vt_ref, o_ref, *, tq, window):
    q0 = pl.program_id(1) * tq
    span = tq + window
    start = pl.multiple_of(jnp.maximum(q0 - window, 0), KV_BLOCK)
    qt = q_ref[...].astype(F32).T * ATTN_SCALE
    kpos = start + _iota((span, tq), 0)
    qpos = q0 + _iota((span, tq), 1)
    bias = jnp.where((kpos <= qpos) & (kpos > qpos - window), 0.0, NEG_INF)
    vb = vt_ref[pl.ds(start // KV_BLOCK, span // KV_BLOCK)]
    group = A_HEADS // A_KV_HEADS
    first_head = _iota((1, group * tq), 1) < tq
    outs = []
    for g in range(A_KV_HEADS):
        heads = [group * g + r for r in range(group)]
        qg = jnp.concatenate([qt[HEAD_DIM * h:HEAD_DIM * (h + 1)] for h in heads], axis=1).astype(BF16)
        s = _dot(k_ref[pl.ds(start, span), HEAD_DIM * g:HEAD_DIM * (g + 1)], qg)
        s = s + jnp.concatenate([bias] * group, axis=1)
        sink = jnp.where(first_head, sink_ref[heads[0]], sink_ref[heads[1]])
        m = jnp.maximum(jnp.max(s, axis=0, keepdims=True), sink)
        p = jnp.exp(s - m)
        den = jnp.sum(p, axis=0, keepdims=True) + jnp.exp(sink - m)
        vt = jnp.concatenate([vb[r, HEAD_DIM * g:HEAD_DIM * (g + 1), :] for r in range(span // KV_BLOCK)], axis=1)
        o_t = _dot(vt, p.astype(BF16)) * (1.0 / den)
        outs += [o_t[:, tq * r:tq * (r + 1)] for r in range(group)]
    o_ref[...] = jnp.concatenate(outs, axis=0).T.astype(BF16)


def _swa_attention(rope, vals_t, sinks, tq=256):
    assert A_HEADS // A_KV_HEADS == 2
    B, S, _ = rope.shape
    return pl.pallas_call(
        functools.partial(_swa_kernel, tq=tq, window=A_WINDOW),
        grid=(B, S // tq),
        in_specs=[
            pl.BlockSpec(memory_space=pltpu.SMEM),
            pl.BlockSpec((None, tq, A_Q), lambda b, i: (b, i, 0)),
            pl.BlockSpec((None, S, A_KV), lambda b, i: (b, 0, (A_Q + B_Q) // A_KV)),
            pl.BlockSpec((None, S // KV_BLOCK, A_KV, KV_BLOCK), lambda b, i: (b, 0, 0, 0)),
        ],
        out_specs=pl.BlockSpec((None, tq, A_Q), lambda b, i: (b, i, 0)),
        out_shape=jax.ShapeDtypeStruct((B, S, A_Q), BF16),
        compiler_params=_cparams("arbitrary", "arbitrary"),
        name="swa_attention",
    )(sinks, rope, rope, vals_t)


def _compress_kernel(x_ref, pea_ref, peb_ref, w1a_ref, w1b_ref, w2_ref, o_ref, ot_ref):
    x = x_ref[...]
    n = x.shape[0]
    a = _dot((x + pea_ref[...]).astype(BF16), w1a_ref[...])
    b = _dot((x + peb_ref[...]).astype(BF16), w1b_ref[...])
    hid = _silu(a + pltpu.roll(b, n - 1, 0))
    out = _dot(hid.astype(BF16), w2_ref[...])
    o_ref[...] = out.astype(BF16)
    ot_ref[...] = out.T.astype(BF16)


def _compress(xc, pea, peb, w1a, w1b, w2):
    B, n, K = xc.shape
    full = lambda a: pl.BlockSpec(a.shape, lambda b: (0,) * a.ndim)
    return pl.pallas_call(
        _compress_kernel,
        grid=(B,),
        in_specs=[pl.BlockSpec((None, n, K), lambda b: (b, 0, 0)), full(pea), full(peb), full(w1a), full(w1b), full(w2)],
        out_specs=[pl.BlockSpec((None, n, 2 * HEAD_DIM), lambda b: (b, 0, 0)),
                   pl.BlockSpec((None, 2 * HEAD_DIM, n), lambda b: (b, 0, 0))],
        out_shape=[jax.ShapeDtypeStruct((B, n, 2 * HEAD_DIM), BF16), jax.ShapeDtypeStruct((B, 2 * HEAD_DIM, n), BF16)],
        compiler_params=_cparams("arbitrary"),
        name="nsa_compress",
    )(xc, pea, peb, w1a, w1b, w2)


def _heads_on_lanes(q):
    qt = q.astype(F32).T * ATTN_SCALE
    return jnp.concatenate([qt[HEAD_DIM * h:HEAD_DIM * (h + 1)] for h in range(B_HEADS)], axis=1).astype(BF16)


def _heads_on_sublanes(ot, tq):
    return jnp.concatenate([ot[:, tq * h:tq * (h + 1)] for h in range(B_HEADS)], axis=0)


def _cmp_select_kernel(q_ref, k_ref, vt_ref, pool_ref, o_ref, sel_ref, *, tq, top_k):
    q0 = pl.program_id(1) * tq
    H = B_HEADS
    n_cmp = k_ref.shape[0]
    n_slc = pool_ref.shape[0]
    qt = _heads_on_lanes(q_ref[...])
    s = _dot(k_ref[:, 0:HEAD_DIM], qt)
    valid = _iota((n_cmp, tq), 0) * CMP_STRIDE + (CMP_LEN - 1) <= q0 + _iota((n_cmp, tq), 1)
    valid = jnp.concatenate([valid] * H, axis=1)
    s = jnp.where(valid, s, NEG_INF)
    e = jnp.where(valid, jnp.exp(s - jnp.max(s, axis=0, keepdims=True)), 0.0)
    den = jnp.sum(e, axis=0, keepdims=True)
    p = e * (1.0 / jnp.where(den > 0.0, den, 1.0))
    o_t = _dot(vt_ref[HEAD_DIM:2 * HEAD_DIM, :], p.astype(BF16))
    o_ref[...] = _heads_on_sublanes(o_t, tq)
    psum = p[:, 0:tq]
    for h in range(1, H):
        psum = psum + p[:, tq * h:tq * (h + 1)]

    imp = _dot(pool_ref[...], psum, HIGHEST)
    blk = _iota((n_slc, tq), 0)
    tt = q0 + _iota((n_slc, tq), 1)
    cur = tt >> 6
    forced = (blk == 0) | (blk == cur) | (blk == cur - 1)
    causal = blk * SLC_LEN <= tt
    score = jnp.where(forced, SLC_FORCED_SCORE, jnp.where(causal, imp, NEG_INF))
    sel = jnp.zeros((n_slc, tq), F32)
    blk_f = blk.astype(F32)
    for _ in range(top_k):
        mx = jnp.max(score, axis=0, keepdims=True)
        first = jnp.min(jnp.where(score == mx, blk_f, float(n_slc)), axis=0, keepdims=True)
        pick = blk_f == first
        sel = jnp.where(pick, 1.0, sel)
        score = jnp.where(pick, -jnp.inf, score)
    sel_ref[...] = jnp.where(causal, sel, 0.0)


def _cmp_select(bq, kv_cmp, kv_cmp_t, pool_t, tq=256):
    B, S, _ = bq.shape
    n_slc, n_cmp = pool_t.shape
    top_k = min(SLC_TOPK, n_slc)
    return pl.pallas_call(
        functools.partial(_cmp_select_kernel, tq=tq, top_k=top_k),
        grid=(B, S // tq),
        in_specs=[
            pl.BlockSpec((None, tq, B_Q), lambda b, i: (b, i, 0)),
            pl.BlockSpec((None, n_cmp, 2 * HEAD_DIM), lambda b, i: (b, 0, 0)),
            pl.BlockSpec((None, 2 * HEAD_DIM, n_cmp), lambda b, i: (b, 0, 0)),
            pl.BlockSpec((n_slc, n_cmp), lambda b, i: (0, 0)),
        ],
        out_specs=[
            pl.BlockSpec((None, B_Q, tq), lambda b, i: (b, 0, i)),
            pl.BlockSpec((None, n_slc, tq), lambda b, i: (b, 0, i)),
        ],
        out_shape=[jax.ShapeDtypeStruct((B, B_Q, S), F32), jax.ShapeDtypeStruct((B, n_slc, S), F32)],
        compiler_params=_cparams("arbitrary", "arbitrary"),
        name="nsa_cmp_select",
    )(bq, kv_cmp, kv_cmp_t, pool_t)


KV_BLOCK = 128
PAD_ROWS = 16
SLC_TILE = 512


def _values_with_ones(vb, row0, n):
    vt = jnp.concatenate([vb[r, row0:row0 + HEAD_DIM, :] for r in range(n)], axis=1)
    return jnp.concatenate([vt, jnp.ones((PAD_ROWS, n * KV_BLOCK), BF16)], axis=0)


def _nsa_attn_kernel(q_ref, k_ref, ka_ref, vt_ref, sel_ref, ocmp_ref, small_ref, o_ref, *, tq, tk, window):
    i = pl.program_id(1)
    q0 = i * tq
    H = B_HEADS
    d = HEAD_DIM
    qt = _heads_on_lanes(q_ref[...])
    bpt = tk // SLC_LEN
    vpt = tk // KV_BLOCK
    q_pad = jnp.zeros((ka_ref.shape[1] - d - PAD_ROWS, H * tq), BF16)
    pen_pad = jnp.zeros((PAD_ROWS - bpt, tq), F32)

    def scores(j):
        sel = sel_ref[pl.ds(pl.multiple_of(j * bpt, bpt), bpt), :]
        pen = jnp.concatenate([jnp.where(sel > 0.5, 0.0, NEG_INF), pen_pad], axis=0)
        q_aug = jnp.concatenate([qt, jnp.concatenate([pen] * H, axis=1).astype(BF16), q_pad], axis=0)
        return _dot(ka_ref[pl.ds(pl.multiple_of(j * tk, tk), tk), :], q_aug)

    def update(j, s, carry):
        m, acc = carry
        m_new = jnp.maximum(m, jnp.max(s, axis=0, keepdims=True))
        alpha = jnp.exp(m - m_new)
        p = jnp.exp(s - m_new)
        vb = vt_ref[pl.ds(pl.multiple_of(j * vpt, vpt), vpt)]
        return m_new, acc * alpha + _dot(_values_with_ones(vb, 0, vpt), p.astype(BF16))

    def full_pair(jp, carry):
        s0 = scores(2 * jp)
        s1 = scores(2 * jp + 1)
        return update(2 * jp + 1, s1, update(2 * jp, s0, carry))

    qpos = q0 + _iota((tk, tq), 1)

    def causal_tile(j, carry):
        causal = jnp.where(j * tk + _iota((tk, tq), 0) <= qpos, 0.0, NEG_INF)
        return update(j, scores(j) + jnp.concatenate([causal] * H, axis=1), carry)

    n_full_pairs = q0 // (2 * tk)
    n_tiles = (q0 + tq + tk - 1) // tk
    carry = (jnp.full((1, H * tq), NEG_INF, F32), jnp.zeros((d + PAD_ROWS, H * tq), F32))
    carry = lax.fori_loop(0, n_full_pairs, full_pair, carry)
    _, acc = lax.fori_loop(2 * n_full_pairs, n_tiles, causal_tile, carry)
    o_slc = _heads_on_sublanes(acc[0:d] * (1.0 / acc[d:d + 1]), tq)

    span = tq + window
    start = pl.multiple_of(jnp.maximum(q0 - window, 0), KV_BLOCK)
    s = _dot(k_ref[pl.ds(start, span), d:2 * d], qt)
    kpos = start + _iota((span, tq), 0)
    qp = q0 + _iota((span, tq), 1)
    bias = jnp.where((kpos <= qp) & (kpos > qp - window), 0.0, NEG_INF)
    s = s + jnp.concatenate([bias] * H, axis=1)
    p = jnp.exp(s - jnp.max(s, axis=0, keepdims=True))
    vb = vt_ref[pl.ds(start // KV_BLOCK, span // KV_BLOCK)]
    res = _dot(_values_with_ones(vb, d, span // KV_BLOCK), p.astype(BF16))
    o_win = _heads_on_sublanes(res[0:d] * (1.0 / res[d:d + 1]), tq)

    gates = jax.nn.sigmoid(small_ref[...]).T
    o_cmp = ocmp_ref[...]
    outs = []
    for h in range(H):
        rows = slice(HEAD_DIM * h, HEAD_DIM * (h + 1))
        outs.append(gates[h:h + 1, :] * o_cmp[rows] + gates[H + h:H + h + 1, :] * o_slc[rows]
                    + gates[2 * H + h:2 * H + h + 1, :] * o_win[rows])
    o_ref[...] = jnp.concatenate(outs, axis=0).T.astype(BF16)


def _slc_keys_with_block_onehot(rope):
    B, S, _ = rope.shape
    t = np.arange(S)[:, None]
    onehot = (t // SLC_LEN) % (SLC_TILE // SLC_LEN) == np.arange(HEAD_DIM)[None, :]
    onehot = jnp.broadcast_to(jnp.asarray(onehot.astype(np.float32)).astype(BF16), (B, S, HEAD_DIM))
    off = A_Q + B_Q + A_KV
    return jnp.concatenate([rope[:, :, off:off + HEAD_DIM], onehot], axis=-1)


def _nsa_attention(rope, vals_t, sel, o_cmp, small, tq=256):
    tk = SLC_TILE
    B, S, _ = rope.shape
    n_slc = sel.shape[1]
    keys_aug = _slc_keys_with_block_onehot(rope)
    return pl.pallas_call(
        functools.partial(_nsa_attn_kernel, tq=tq, tk=tk, window=NSA_WINDOW),
        grid=(B, S // tq),
        in_specs=[
            pl.BlockSpec((None, tq, B_Q), lambda b, i: (b, i, A_Q // B_Q)),
            pl.BlockSpec((None, S, 128), lambda b, i: (b, 0, (A_Q + B_Q + A_KV) // 128)),
            pl.BlockSpec((None, S, 2 * HEAD_DIM), lambda b, i: (b, 0, 0)),
            pl.BlockSpec((None, S // KV_BLOCK, 2 * HEAD_DIM, KV_BLOCK), lambda b, i: (b, 0, A_KV // (2 * HEAD_DIM), 0)),
            pl.BlockSpec((None, n_slc, tq), lambda b, i: (b, 0, i)),
            pl.BlockSpec((None, B_Q, tq), lambda b, i: (b, 0, i)),
            pl.BlockSpec((None, tq, W_SMALL), lambda b, i: (b, i, 0)),
        ],
        out_specs=pl.BlockSpec((None, tq, B_Q), lambda b, i: (b, i, 0)),
        out_shape=jax.ShapeDtypeStruct((B, S, B_Q), BF16),
        compiler_params=_cparams("arbitrary", "arbitrary"),
        name="nsa_attention",
    )(rope, rope, keys_aug, vals_t, sel, o_cmp, small)


def _gdn_kernel(xq_ref, xk_ref, xv_ref, hq_ref, hk_ref, hv_ref, z_ref, small_ref, arow_ref,
                cw_ref, alog_row_ref, dtb_row_ref, alog_col_ref, dtb_col_ref, nw_ref,
                o_ref, q_s, k_s, v_s, g_s, b_s, o_s, state_s, u_s, wq_s, qk_s, kdt_s, gl_s, pq_s, pk_s, pv_s, *, tt):
    i = pl.program_id(1)
    C = GDN_CHUNK
    dk = C_HEAD_DIM
    H = C_HEADS

    @pl.when(i == 0)
    def _():
        state_s[...] = jnp.zeros_like(state_s)

    def conv(x_ref, halo_ref, pad_s, col0):
        pad_s[0:8, :] = jnp.where(i > 0, halo_ref[...], 0.0)
        pad_s[8:8 + tt, :] = x_ref[...]
        w = cw_ref[:, col0:col0 + C_QK]
        y = x_ref[...] * w[3:4]
        for kback in range(1, CONV_WIDTH):
            y = y + pad_s[8 - kback:8 - kback + tt, :] * w[3 - kback:4 - kback]
        return _silu(y)

    qf = conv(xq_ref, hq_ref, pq_s, 0)
    kf = conv(xk_ref, hk_ref, pk_s, C_QK)
    v_s[...] = conv(xv_ref, hv_ref, pv_s, 2 * C_QK)
    for h in range(H):
        cols = slice(dk * h, dk * (h + 1))
        qh = qf[:, cols]
        kh = kf[:, cols]
        q_s[:, cols] = qh * lax.rsqrt(jnp.sum(qh * qh, axis=-1, keepdims=True) + NORM_EPS) * (dk ** -0.5)
        k_s[:, cols] = kh * lax.rsqrt(jnp.sum(kh * kh, axis=-1, keepdims=True) + NORM_EPS)
    small = small_ref[...]
    g_s[...] = -jnp.exp(alog_row_ref[...]) * _softplus(small + dtb_row_ref[...])
    b_s[...] = jax.nn.sigmoid(small)

    r_i = _iota((C, C), 0)
    c_i = _iota((C, C), 1)
    lower = r_i >= c_i
    strict = r_i > c_i
    tri_l = lower.astype(F32)
    tri_u = (r_i <= c_i).astype(F32)
    neg_alog_col = -jnp.exp(alog_col_ref[...])
    dtb_col = dtb_col_ref[...]

    HC = H * C
    GC = GDN_GROUP * C
    rr = _iota((GC, GC), 0)
    cc = _iota((GC, GC), 1)
    same_head = (rr >> 6) == (cc >> 6)
    lower_bd = same_head & (rr >= cc)
    strict_bd = same_head & (rr > cc)
    state_bd = (_iota((HC, H * dk), 0) >> 6) == (_iota((HC, H * dk), 1) >> 7)

    def problem(c, grp):
        rows = pl.ds(pl.multiple_of(c * C, C), C)
        gc_col = _dot(tri_l, g_s[rows, :], HIGHEST)
        g_row = neg_alog_col * _softplus(arow_ref[c] + dtb_col)
        gc_row = _dot(g_row, tri_u, HIGHEST)
        beta_all = b_s[rows, :]
        g_last = [gc_col[C - 1:C, SMALL_A + h:SMALL_A + h + 1] for h in range(H)]
        if grp == 0:
            gl_s[c, 0:1, :] = jnp.concatenate([jnp.broadcast_to(jnp.exp(g), (1, dk)) for g in g_last], axis=1)
        heads = range(GDN_GROUP * grp, GDN_GROUP * (grp + 1))
        stack = lambda ref: jnp.concatenate([ref[rows, dk * h:dk * (h + 1)] for h in heads], axis=0)
        gcol = jnp.concatenate([gc_col[:, SMALL_A + h:SMALL_A + h + 1] for h in heads], axis=0)
        grow = jnp.concatenate([gc_row[h:h + 1, :] for h in heads], axis=1)
        beta = jnp.concatenate([beta_all[:, SMALL_B + h:SMALL_B + h + 1] for h in heads], axis=0)
        glast = jnp.concatenate([jnp.broadcast_to(g_last[h], (C, 1)) for h in heads], axis=0)
        k = stack(k_s)
        return dict(c=c, grp=grp, q=stack(q_s), k=k, v=stack(v_s), kb=k * beta, beta=beta, eg=jnp.exp(gcol),
                    kd=k * jnp.exp(glast - gcol),
                    decay=jnp.where(lower_bd, jnp.exp(jnp.where(lower_bd, gcol - grow, 0.0)), 0.0))

    def prepare_group(cg, carry):
        ps = [problem(PREP_GROUP * cg + r, grp) for r in range(PREP_GROUP) for grp in range(H // GDN_GROUP)]
        for p in ps:
            p["gram"] = _dot_nt(jnp.concatenate([p["kb"], p["q"]], axis=0).astype(BF16), p["k"].astype(BF16))
        for p in ps:
            p["qk"] = jnp.where(lower_bd, p["gram"][GC:2 * GC] * p["decay"], 0.0)
            p["x"] = -jnp.where(strict_bd, p["gram"][0:GC] * p["decay"], 0.0)
            p["pw16"] = p["x"].astype(BF16)
        for p in ps:
            p["pw"] = _dot(p["pw16"], p["pw16"])
        for _ in range(4):
            for p in ps:
                pw16 = p["pw"].astype(BF16)
                p["x"], p["pw"] = p["x"] + p["pw"] + _dot(pw16, p["x"].astype(BF16)), _dot(pw16, pw16)
        for p in ps:
            p["x"] = p["x"] + p["pw"] + _dot(p["pw"].astype(BF16), p["x"].astype(BF16))
        for p in ps:
            rhs = jnp.concatenate([p["v"] * p["beta"], p["kb"] * p["eg"]], axis=1)
            p["y"] = rhs + _dot(p["x"].astype(BF16), rhs.astype(BF16))
        for p in ps:
            c, grp, y = p["c"], p["grp"], p["y"]
            grows = slice(GC * grp, GC * (grp + 1))
            u_s[c, grows, :] = y[:, 0:dk]
            wq_s[c, grows, :] = y[:, dk:2 * dk].astype(BF16)
            wq_s[c, HC + GC * grp:HC + GC * (grp + 1), :] = (p["q"] * p["eg"]).astype(BF16)
            qk_s[c, grows, :] = p["qk"].astype(BF16)
            kdt_s[c, :, grows] = p["kd"].T.astype(BF16)
        return carry

    lax.fori_loop(0, tt // (PREP_GROUP * C), prepare_group, 0)

    def head_blocks(x, row0):
        return jnp.concatenate([x[row0 + C * h:row0 + C * (h + 1), dk * h:dk * (h + 1)] for h in range(H)], axis=0)

    def scan(c, carry):
        r0 = pl.multiple_of(c * C, C)
        rows = pl.ds(r0, C)
        st = state_s[...]
        d1 = _dot(wq_s[c], st.astype(BF16))
        v_new = u_s[c] - head_blocks(d1, 0)
        v16 = v_new.astype(BF16)
        o = head_blocks(d1, HC) + jnp.concatenate(
            [_dot(qk_s[c, GC * g:GC * (g + 1), :], v16[GC * g:GC * (g + 1)]) for g in range(H // GDN_GROUP)], axis=0)
        v_bd = jnp.where(state_bd, jnp.concatenate([v16] * H, axis=1), jnp.zeros((), BF16))
        state_s[...] = st * gl_s[c, 0:1, :] + _dot(kdt_s[c], v_bd)
        for h in range(H):
            o_s[rows, dk * h:dk * (h + 1)] = o[C * h:C * (h + 1)]
        return carry

    lax.fori_loop(0, tt // C, scan, 0)

    nw = nw_ref[...]
    z = z_ref[...]
    for h in range(H):
        cols = slice(dk * h, dk * (h + 1))
        oh = o_s[:, cols]
        y = oh * lax.rsqrt(jnp.mean(oh * oh, axis=-1, keepdims=True) + NORM_EPS) * nw
        o_ref[:, cols] = (y * _silu(z[:, cols])).astype(BF16)


def _gated_delta_net(gdn, small, a_rows, conv_w, alog_row, dtb_row, alog_col, dtb_col, norm_w, tt=512):
    B, S, _ = gdn.shape
    nch = tt // GDN_CHUNK
    hc = C_HEADS * GDN_CHUNK
    part = lambda j: pl.BlockSpec((None, tt, C_QK), lambda b, i: (b, i, j))
    halo = lambda j: pl.BlockSpec((None, 8, C_QK), lambda b, i: (b, jnp.maximum(i * (tt // 8) - 1, 0), j))
    full = lambda a: pl.BlockSpec(a.shape, lambda b, i: (0,) * a.ndim)
    return pl.pallas_call(
        functools.partial(_gdn_kernel, tt=tt),
        grid=(B, S // tt),
        in_specs=[part(0), part(1), part(2), halo(0), halo(1), halo(2), part(3),
                  pl.BlockSpec((None, tt, W_SMALL), lambda b, i: (b, i, 0)),
                  pl.BlockSpec((None, nch, 8, GDN_CHUNK), lambda b, i: (b, i, 0, 0)),
                  full(conv_w), full(alog_row), full(dtb_row), full(alog_col), full(dtb_col), full(norm_w)],
        out_specs=pl.BlockSpec((None, tt, C_QK), lambda b, i: (b, i, 0)),
        out_shape=jax.ShapeDtypeStruct((B, S, C_QK), BF16),
        scratch_shapes=[pltpu.VMEM((tt, C_QK), F32), pltpu.VMEM((tt, C_QK), F32), pltpu.VMEM((tt, C_QK), F32),
                        pltpu.VMEM((tt, W_SMALL), F32), pltpu.VMEM((tt, W_SMALL), F32),
                        pltpu.VMEM((tt, C_QK), F32), pltpu.VMEM((C_HEAD_DIM, C_QK), F32),
                        pltpu.VMEM((nch, hc, C_HEAD_DIM), F32), pltpu.VMEM((nch, 2 * hc, C_HEAD_DIM), BF16),
                        pltpu.VMEM((nch, hc, GDN_GROUP * GDN_CHUNK), BF16), pltpu.VMEM((nch, C_HEAD_DIM, hc), BF16),
                        pltpu.VMEM((nch, 8, C_QK), F32)] + [pltpu.VMEM((tt + 8, C_QK), F32)] * 3,
        compiler_params=_cparams("arbitrary", "arbitrary"),
        name="gated_delta_net",
    )(gdn, gdn, gdn, gdn, gdn, gdn, gdn, small, a_rows, conv_w, alog_row, dtb_row, alog_col, dtb_col, norm_w)


def _outproj_kernel(x_ref, oa_ref, ob_ref, oc_ref, w_ref, gt_ref, o_ref):
    y = _dot(oa_ref[...], w_ref[0:A_Q, :])
    y = y + _dot(ob_ref[...], w_ref[A_Q:A_Q + B_Q, :])
    y = y + _dot(oc_ref[...], w_ref[A_Q + B_Q:, :])
    o_ref[...] = x_ref[...] + gt_ref[...] * y


def _out_projection(x, o_a, o_b, o_c, w, mod, tm=1024):
    B, S, D = x.shape
    row = lambda b, i: (b, i, 0)
    return pl.pallas_call(
        _outproj_kernel,
        grid=(B, S // tm),
        in_specs=[
            pl.BlockSpec((None, tm, D), row),
            pl.BlockSpec((None, tm, A_Q), row),
            pl.BlockSpec((None, tm, B_Q), row),
            pl.BlockSpec((None, tm, C_QK), row),
            pl.BlockSpec((D, D), lambda b, i: (0, 0)),
            pl.BlockSpec((None, 1, D), lambda b, i: (b, 0, 2)),
        ],
        out_specs=pl.BlockSpec((None, tm, D), row),
        out_shape=jax.ShapeDtypeStruct((B, S, D), F32),
        compiler_params=_cparams("arbitrary", "arbitrary"),
        name="out_projection",
    )(x, o_a, o_b, o_c, w, mod)


def _ffn_kernel(x_ref, g_ref, sc_ref, sh_ref, gt_ref, wgu_ref, wd_ref, o_ref, *, tf):
    x = x_ref[...]
    h = _rms_mod(x, g_ref[...], sc_ref[...], sh_ref[...]).astype(BF16)
    acc = None
    for f in range(D_FF // tf):
        gate = _dot(h, wgu_ref[:, tf * f:tf * (f + 1)])
        up = _dot(h, wgu_ref[:, D_FF + tf * f:D_FF + tf * (f + 1)])
        part = _dot((_silu(gate) * up).astype(BF16), wd_ref[tf * f:tf * (f + 1), :])
        acc = part if acc is None else acc + part
    o_ref[...] = x + gt_ref[...] * acc


def _ffn(x, gain, mod, w_gate_up, w_down, tm=512, tf=1408):
    B, S, D = x.shape
    row = lambda b, i: (b, i, 0)
    resident = lambda a: pl.BlockSpec(a.shape, lambda b, i: (0, 0), pipeline_mode=pl.Buffered(1))
    return pl.pallas_call(
        functools.partial(_ffn_kernel, tf=tf),
        grid=(B, S // tm),
        in_specs=[
            pl.BlockSpec((None, tm, D), row),
            pl.BlockSpec((1, D), lambda b, i: (0, 0)),
            pl.BlockSpec((None, 1, D), lambda b, i: (b, 0, 4)),
            pl.BlockSpec((None, 1, D), lambda b, i: (b, 0, 3)),
            pl.BlockSpec((None, 1, D), lambda b, i: (b, 0, 5)),
            resident(w_gate_up),
            resident(w_down),
        ],
        out_specs=pl.BlockSpec((None, tm, D), row),
        out_shape=jax.ShapeDtypeStruct((B, S, D), F32),
        compiler_params=_cparams("arbitrary", "arbitrary"),
        name="swiglu_ffn",
    )(x, gain, mod, mod, mod, w_gate_up, w_down)


def _final_norm_kernel(x_ref, g_ref, o_ref):
    x = x_ref[...]
    o_ref[...] = x * lax.rsqrt(jnp.mean(x * x, axis=-1, keepdims=True) + NORM_EPS) * g_ref[...]


def _final_norm(x, gain, tm=512):
    B, S, D = x.shape
    return pl.pallas_call(
        _final_norm_kernel,
        grid=(B, S // tm),
        in_specs=[pl.BlockSpec((None, tm, D), lambda b, i: (b, i, 0)), pl.BlockSpec((1, D), lambda b, i: (0, 0))],
        out_specs=pl.BlockSpec((None, tm, D), lambda b, i: (b, i, 0)),
        out_shape=jax.ShapeDtypeStruct((B, S, D), F32),
        compiler_params=_cparams("arbitrary", "arbitrary"),
        name="final_norm",
    )(x, gain)


def _rope_tables(seq):
    inv = 1.0 / (ROPE_THETA ** (jnp.arange(0, HEAD_DIM, 2, dtype=F32) / HEAD_DIM))
    ang = jnp.arange(seq, dtype=F32)[:, None] * inv[None, :]
    cos, sin = jnp.cos(ang), jnp.sin(ang)
    zero = jnp.zeros_like(sin)
    cos_t = jnp.tile(cos, (1, 4))
    sa_t = jnp.tile(jnp.concatenate([-sin, zero], axis=1), (1, 2))
    sb_t = jnp.tile(jnp.concatenate([zero, sin], axis=1), (1, 2))
    return cos_t, sa_t, sb_t


def _permute_w_in(w_in):
    pts = np.cumsum(IN_SPLITS)[:-1].tolist()
    aq, ak, av, bq, bkc, bvc, bks, bvs, bkw, bvw, bg, cqkv, cz, ca, cb = jnp.split(w_in, pts, axis=-1)
    pad = jnp.zeros(w_in.shape[:-1] + (W_SMALL - B_GATES - 2 * C_HEADS,), w_in.dtype)
    return jnp.concatenate([aq, bq, ak, bks, bkw, av, bvs, bvw, bkc, bvc, bg, ca, cb, pad, cqkv, cz], axis=-1).astype(BF16)


def _compress_weights(k_w1, k_w2, v_w1, v_w2, pe_k, pe_v):
    L = k_w1.shape[0]
    half = CMP_STRIDE * HEAD_DIM

    def first_layer(lo):
        wk = k_w1[:, lo:lo + half].reshape(L, CMP_STRIDE, HEAD_DIM, CMP_HIDDEN)
        wv = v_w1[:, lo:lo + half].reshape(L, CMP_STRIDE, HEAD_DIM, CMP_HIDDEN)
        z = jnp.zeros_like(wk)
        w = jnp.concatenate([jnp.concatenate([wk, z], axis=-1), jnp.concatenate([z, wv], axis=-1)], axis=2)
        return w.reshape(L, 2 * half, 2 * CMP_HIDDEN).astype(BF16)

    def pe_rows(lo):
        return jnp.concatenate([pe_k[:, lo:lo + CMP_STRIDE], pe_v[:, lo:lo + CMP_STRIDE]], axis=-1).reshape(L, 1, 2 * half)

    z2 = jnp.zeros_like(k_w2)
    w2 = jnp.concatenate([jnp.concatenate([k_w2, z2], axis=-1), jnp.concatenate([z2, v_w2], axis=-1)], axis=1).astype(BF16)
    return pe_rows(0), pe_rows(CMP_STRIDE), first_layer(0), first_layer(half), w2


def _pool_matrix(n_cmp_rows, n_slc):
    ratio = SLC_LEN // CMP_STRIDE
    n = np.arange(n_cmp_rows)[:, None]
    j = np.arange(n_slc)[None, :]
    return jnp.asarray(((n // ratio == j) | (n == ratio * j - 1)).astype(np.float32))


def _lane_row(v, lane0):
    L, n = v.shape
    return jnp.zeros((L, 1, 128), F32).at[:, 0, lane0:lane0 + n].set(v)


def _sublane_col(v):
    L, n = v.shape
    return jnp.zeros((L, 8, 1), F32).at[:, :n, 0].set(v)


def kernel(x, c, norm_mix, norm_ffn, ada_w, ada_b, w_in, attn_sinks, cmp_k_w1, cmp_k_w2, cmp_v_w1, cmp_v_w2, cmp_pe_k, cmp_pe_v, gdn_conv_w, gdn_A_log, gdn_dt_bias, gdn_norm, w_out, w_gate_up, w_down, final_norm):
    B, S, D = x.shape
    L = w_in.shape[0]
    n16 = S // CMP_STRIDE
    n_slc = S // SLC_LEN

    cos_t, sa_t, sb_t = _rope_tables(S)
    mod = _adaln_mod(c, ada_w, ada_b)
    pea, peb, w1a, w1b, w2c = _compress_weights(cmp_k_w1, cmp_k_w2, cmp_v_w1, cmp_v_w2, cmp_pe_k, cmp_pe_v)
    pool_t = _pool_matrix(n16, n_slc).T
    per_layer = dict(
        mod=mod,
        norm_mix=norm_mix.reshape(L, 1, D), norm_ffn=norm_ffn.reshape(L, 1, D),
        w_in=_permute_w_in(w_in), sinks=attn_sinks,
        pea=pea, peb=peb, w1a=w1a, w1b=w1b, w2c=w2c,
        conv_w=gdn_conv_w,
        alog_row=_lane_row(gdn_A_log, SMALL_A), dtb_row=_lane_row(gdn_dt_bias, SMALL_A),
        alog_col=_sublane_col(gdn_A_log), dtb_col=_sublane_col(gdn_dt_bias),
        gdn_norm=gdn_norm.reshape(L, 1, C_HEAD_DIM),
        w_out=w_out.astype(BF16), w_gate_up=w_gate_up.astype(BF16), w_down=w_down.astype(BF16),
    )

    def layer(xc, p):
        rope, bq, vals, cmp_in, small, gdn = _in_projection(xc, p["norm_mix"], p["mod"], p["w_in"], cos_t, sa_t, sb_t)
        vals_all_t = vals.reshape(B, S // KV_BLOCK, KV_BLOCK, W_V).transpose(0, 1, 3, 2)
        o_a = _swa_attention(rope, vals_all_t, p["sinks"])
        kv_cmp, kv_cmp_t = _compress(cmp_in.reshape(B, n16, CMP_STRIDE * W_CMP), p["pea"], p["peb"], p["w1a"], p["w1b"], p["w2c"])
        o_cmp_t, sel = _cmp_select(bq, kv_cmp, kv_cmp_t, pool_t)
        o_b = _nsa_attention(rope, vals_all_t, sel, o_cmp_t, small)
        a_rows = small[:, :, SMALL_A:SMALL_A + 8].reshape(B, S // GDN_CHUNK, GDN_CHUNK, 8).transpose(0, 1, 3, 2)
        o_c = _gated_delta_net(gdn, small, a_rows, p["conv_w"], p["alog_row"], p["dtb_row"], p["alog_col"], p["dtb_col"], p["gdn_norm"])
        xc = _out_projection(xc, o_a, o_b, o_c, p["w_out"], p["mod"])
        xc = _ffn(xc, p["norm_ffn"], p["mod"], p["w_gate_up"], p["w_down"])
        return xc, None

    x, _ = lax.scan(layer, x, per_layer)
    return _final_norm(x, final_norm.reshape(1, D))
```

```python
import functools
import math

import numpy as np
import jax
import jax.numpy as jnp
from jax import lax
from jax.experimental import pallas as pl
from jax.experimental.pallas import tpu as pltpu

F32 = jnp.float32
BF16 = jnp.bfloat16
HIGHEST = lax.Precision.HIGHEST

D_MODEL = 1024
DEPTH = 4
HEAD_DIM = 64
ATTN_SCALE = HEAD_DIM ** -0.5
ROPE_THETA = 10000.0
NEG_INF = -1e30
NORM_EPS = 1e-6

A_HEADS = 4
A_KV_HEADS = 2
A_WINDOW = 128

B_HEADS = 4
CMP_STRIDE = 16
CMP_LEN = 32
CMP_HIDDEN = 256
SLC_LEN = 64
SLC_TOPK = 16
NSA_WINDOW = 512
SLC_FORCED_SCORE = 1e9

C_HEAD_DIM = 128
C_HEADS = 4
CONV_WIDTH = 4
GDN_CHUNK = 64
C_QK = C_HEADS * C_HEAD_DIM

D_FF = 2816

A_Q = A_HEADS * HEAD_DIM
A_KV = A_KV_HEADS * HEAD_DIM
B_Q = B_HEADS * HEAD_DIM
B_KV = HEAD_DIM
B_GATES = 3 * B_HEADS
IN_SPLITS = (A_Q, A_KV, A_KV, B_Q, B_KV, B_KV, B_KV, B_KV, B_KV, B_KV, B_GATES, 3 * C_QK, C_QK, C_HEADS, C_HEADS)

W_ROPE = 768
W_V = 256
W_CMP = 128
W_SMALL = 128
W_GDN = 2048
W_TOTAL = W_ROPE + W_V + W_CMP + W_SMALL + W_GDN
SMALL_A = B_GATES
SMALL_B = B_GATES + C_HEADS

VMEM_LIMIT = 56 * 1024 * 1024
PREP_GROUP = 8
GDN_GROUP = 2

NT_DIMS = (((1,), (1,)), ((), ()))


def _cparams(*sem):
    return pltpu.CompilerParams(dimension_semantics=sem, vmem_limit_bytes=VMEM_LIMIT)


def _iota(shape, dim):
    return lax.broadcasted_iota(jnp.int32, shape, dim)


def _dot(a, b, precision=None):
    return jnp.dot(a, b, preferred_element_type=F32, precision=precision)


def _dot_nt(a, b):
    return lax.dot_general(a, b, NT_DIMS, preferred_element_type=F32)


def _silu(x):
    return x * jax.nn.sigmoid(x)


def _softplus(x):
    return jnp.maximum(x, 0.0) + jnp.log(1.0 + jnp.exp(-jnp.abs(x)))


def _rms_mod(x, gain, sc, sh):
    y = x * lax.rsqrt(jnp.mean(x * x, axis=-1, keepdims=True) + NORM_EPS)
    return (y * gain) * (1.0 + sc) + sh


def _mod_kernel(c_ref, w_ref, b_ref, o_ref):
    o_ref[...] = _dot(_silu(c_ref[...]), w_ref[...], HIGHEST) + b_ref[...]


def _adaln_mod(c, ada_w, ada_b):
    L, D, N = ada_w.shape
    tn = 1536
    c8 = jnp.zeros((8, D), F32).at[: c.shape[0]].set(c)
    out = pl.pallas_call(
        _mod_kernel,
        grid=(L, N // tn),
        in_specs=[
            pl.BlockSpec((8, D), lambda l, j: (0, 0)),
            pl.BlockSpec((None, D, tn), lambda l, j: (l, 0, j)),
            pl.BlockSpec((None, 1, tn), lambda l, j: (l, 0, j)),
        ],
        out_specs=pl.BlockSpec((None, 8, tn), lambda l, j: (l, 0, j)),
        out_shape=jax.ShapeDtypeStruct((L, 8, N), F32),
        compiler_params=_cparams("arbitrary", "arbitrary"),
        name="adaln_mod",
    )(c8, ada_w, ada_b.reshape(L, 1, N))
    return out.reshape(L, 8, 1, N)


def _inproj_kernel(x_ref, g_ref, sc_ref, sh_ref, w_ref, cos_ref, sa_ref, sb_ref,
                   rope_ref, bq_ref, v_ref, cmp_ref, small_ref, gdn_ref):
    h = _rms_mod(x_ref[...], g_ref[...], sc_ref[...], sh_ref[...]).astype(BF16)
    yr = _dot(h, w_ref[:, 0:W_ROPE])
    bq_ref[...] = yr[:, A_Q:A_Q + B_Q].astype(BF16)
    c, sa, sb = cos_ref[...], sa_ref[...], sb_ref[...]
    for g in range(W_ROPE // 128):
        xg = yr[:, 128 * g:128 * (g + 1)]
        rot = xg * c + pltpu.roll(xg, 96, 1) * sa + pltpu.roll(xg, 32, 1) * sb
        rope_ref[:, 128 * g:128 * (g + 1)] = rot.astype(BF16)
    o = W_ROPE
    yp = _dot(h, w_ref[:, o:o + W_V + W_CMP + W_SMALL])
    v_ref[...] = yp[:, 0:W_V].astype(BF16)
    cmp_ref[...] = yp[:, W_V:W_V + W_CMP]
    small_ref[...] = yp[:, W_V + W_CMP:W_V + W_CMP + W_SMALL]
    o += W_V + W_CMP + W_SMALL
    gdn_ref[...] = _dot(h, w_ref[:, o:o + W_GDN])


def _in_projection(x, gain, mod, w, cos_t, sa_t, sb_t, tm=512):
    B, S, D = x.shape
    row = lambda b, i: (b, i, 0)
    outs = pl.pallas_call(
        _inproj_kernel,
        grid=(B, S // tm),
        in_specs=[
            pl.BlockSpec((None, tm, D), row),
            pl.BlockSpec((1, D), lambda b, i: (0, 0)),
            pl.BlockSpec((None, 1, D), lambda b, i: (b, 0, 1)),
            pl.BlockSpec((None, 1, D), lambda b, i: (b, 0, 0)),
            pl.BlockSpec((D, W_TOTAL), lambda b, i: (0, 0)),
            pl.BlockSpec((tm, 128), lambda b, i: (i, 0)),
            pl.BlockSpec((tm, 128), lambda b, i: (i, 0)),
            pl.BlockSpec((tm, 128), lambda b, i: (i, 0)),
        ],
        out_specs=[
            pl.BlockSpec((None, tm, W_ROPE), row),
            pl.BlockSpec((None, tm, B_Q), row),
            pl.BlockSpec((None, tm, W_V), row),
            pl.BlockSpec((None, tm, W_CMP), row),
            pl.BlockSpec((None, tm, W_SMALL), row),
            pl.BlockSpec((None, tm, W_GDN), row),
        ],
        out_shape=[
            jax.ShapeDtypeStruct((B, S, W_ROPE), BF16),
            jax.ShapeDtypeStruct((B, S, B_Q), BF16),
            jax.ShapeDtypeStruct((B, S, W_V), BF16),
            jax.ShapeDtypeStruct((B, S, W_CMP), F32),
            jax.ShapeDtypeStruct((B, S, W_SMALL), F32),
            jax.ShapeDtypeStruct((B, S, W_GDN), F32),
        ],
        compiler_params=_cparams("arbitrary", "arbitrary"),
        name="in_projection",
    )(x, gain, mod, mod, w, cos_t, sa_t, sb_t)
    return outs


def _swa_kernel(sink_ref, q_ref, k_ref, vt_ref, o_ref, *, tq, window):
    q0 = pl.program_id(1) * tq
    span = tq + window
    start = pl.multiple_of(jnp.maximum(q0 - window, 0), KV_BLOCK)
    qt = q_ref[...].astype(F32).T * ATTN_SCALE
    kpos = start + _iota((span, tq), 0)
    qpos = q0 + _iota((span, tq), 1)
    bias = jnp.where((kpos <= qpos) & (kpos > qpos - window), 0.0, NEG_INF)
    vb = vt_ref[pl.ds(start // KV_BLOCK, span // KV_BLOCK)]
    group = A_HEADS // A_KV_HEADS
    first_head = _iota((1, group * tq), 1) < tq
    bias = jnp.concatenate([bias] * group, axis=1)
    groups = range(A_KV_HEADS)
    qg = [jnp.concatenate([qt[HEAD_DIM * h:HEAD_DIM * (h + 1)] for h in range(group * g, group * (g + 1))],
                          axis=1).astype(BF16) for g in groups]
    s = [_dot(k_ref[pl.ds(start, span), HEAD_DIM * g:HEAD_DIM * (g + 1)], qg[g]) + bias for g in groups]
    sink = [jnp.where(first_head, sink_ref[group * g], sink_ref[group * g + 1]) for g in groups]
    m = [jnp.maximum(jnp.max(s[g], axis=0, keepdims=True), sink[g]) for g in groups]
    p = [jnp.exp(s[g] - m[g]) for g in groups]
    den = [jnp.sum(p[g], axis=0, keepdims=True) + jnp.exp(sink[g] - m[g]) for g in groups]
    vt = [jnp.concatenate([vb[r, HEAD_DIM * g:HEAD_DIM * (g + 1), :] for r in range(span // KV_BLOCK)], axis=1)
          for g in groups]
    o_t = [_dot(vt[g], p[g].astype(BF16)) * (1.0 / den[g]) for g in groups]
    outs = [o_t[g][:, tq * r:tq * (r + 1)] for g in groups for r in range(group)]
    o_ref[...] = jnp.concatenate(outs, axis=0).T.astype(BF16)


def _swa_attention(rope, vals_t, sinks, tq=256):
    assert A_HEADS // A_KV_HEADS == 2
    B, S, _ = rope.shape
    return pl.pallas_call(
        functools.partial(_swa_kernel, tq=tq, window=A_WINDOW),
        grid=(B, S // tq),
        in_specs=[
            pl.BlockSpec(memory_space=pltpu.SMEM),
            pl.BlockSpec((None, tq, A_Q), lambda b, i: (b, i, 0)),
            pl.BlockSpec((None, S, A_KV), lambda b, i: (b, 0, (A_Q + B_Q) // A_KV)),
            pl.BlockSpec((None, S // KV_BLOCK, A_KV, KV_BLOCK), lambda b, i: (b, 0, 0, 0)),
        ],
        out_specs=pl.BlockSpec((None, tq, A_Q), lambda b, i: (b, i, 0)),
        out_shape=jax.ShapeDtypeStruct((B, S, A_Q), BF16),
        compiler_params=_cparams("arbitrary", "arbitrary"),
        name="swa_attention",
    )(sinks, rope, rope, vals_t)


def _compress_kernel(x_ref, pea_ref, peb_ref, w1a_ref, w1b_ref, w2_ref, o_ref, ot_ref):
    x = x_ref[...]
    n = x.shape[0]
    a = _dot((x + pea_ref[...]).astype(BF16), w1a_ref[...])
    b = _dot((x + peb_ref[...]).astype(BF16), w1b_ref[...])
    hid = _silu(a + pltpu.roll(b, n - 1, 0))
    out = _dot(hid.astype(BF16), w2_ref[...])
    o_ref[...] = out.astype(BF16)
    ot_ref[...] = out.T.astype(BF16)


def _compress(xc, pea, peb, w1a, w1b, w2):
    B, n, K = xc.shape
    full = lambda a: pl.BlockSpec(a.shape, lambda b: (0,) * a.ndim)
    return pl.pallas_call(
        _compress_kernel,
        grid=(B,),
        in_specs=[pl.BlockSpec((None, n, K), lambda b: (b, 0, 0)), full(pea), full(peb), full(w1a), full(w1b), full(w2)],
        out_specs=[pl.BlockSpec((None, n, 2 * HEAD_DIM), lambda b: (b, 0, 0)),
                   pl.BlockSpec((None, 2 * HEAD_DIM, n), lambda b: (b, 0, 0))],
        out_shape=[jax.ShapeDtypeStruct((B, n, 2 * HEAD_DIM), BF16), jax.ShapeDtypeStruct((B, 2 * HEAD_DIM, n), BF16)],
        compiler_params=_cparams("arbitrary"),
        name="nsa_compress",
    )(xc, pea, peb, w1a, w1b, w2)


def _heads_on_lanes(q):
    qt = q.astype(F32).T * ATTN_SCALE
    return jnp.concatenate([qt[HEAD_DIM * h:HEAD_DIM * (h + 1)] for h in range(B_HEADS)], axis=1).astype(BF16)


def _heads_on_sublanes(ot, tq):
    return jnp.concatenate([ot[:, tq * h:tq * (h + 1)] for h in range(B_HEADS)], axis=0)


def _cmp_select_kernel(q_ref, k_ref, vt_ref, pool_ref, o_ref, sel_ref, *, tq, top_k):
    q0 = pl.program_id(1) * tq
    H = B_HEADS
    n_cmp = k_ref.shape[0]
    n_slc = pool_ref.shape[0]
    qt = _heads_on_lanes(q_ref[...])
    valid = _iota((n_cmp, tq), 0) * CMP_STRIDE + (CMP_LEN - 1) <= q0 + _iota((n_cmp, tq), 1)
    valid = jnp.concatenate([valid] * (H // 2), axis=1)
    halves = [slice(0, H // 2 * tq), slice(H // 2 * tq, H * tq)]
    kc = k_ref[:, 0:HEAD_DIM]
    vc_t = vt_ref[HEAD_DIM:2 * HEAD_DIM, :]
    s = [jnp.where(valid, _dot(kc, qt[:, hs]), NEG_INF) for hs in halves]
    e = [jnp.where(valid, jnp.exp(si - jnp.max(si, axis=0, keepdims=True)), 0.0) for si in s]
    den = [jnp.sum(ei, axis=0, keepdims=True) for ei in e]
    p = [ei * (1.0 / jnp.where(di > 0.0, di, 1.0)) for ei, di in zip(e, den)]
    o_t = jnp.concatenate([_dot(vc_t, pi.astype(BF16)) for pi in p], axis=1)
    o_ref[...] = _heads_on_sublanes(o_t, tq)
    psum = None
    for pi in p:
        for r in range(H // 2):
            part = pi[:, tq * r:tq * (r + 1)]
            psum = part if psum is None else psum + part

    imp = _dot(pool_ref[...], psum, HIGHEST)
    blk = _iota((n_slc, tq), 0)
    tt = q0 + _iota((n_slc, tq), 1)
    cur = tt >> 6
    forced = (blk == 0) | (blk == cur) | (blk == cur - 1)
    causal = blk * SLC_LEN <= tt
    score = jnp.where(forced, SLC_FORCED_SCORE, jnp.where(causal, imp, NEG_INF))
    sel = jnp.zeros((n_slc, tq), F32)
    blk_f = blk.astype(F32)
    for _ in range(top_k):
        mx = jnp.max(score, axis=0, keepdims=True)
        first = jnp.min(jnp.where(score == mx, blk_f, float(n_slc)), axis=0, keepdims=True)
        pick = blk_f == first
        sel = jnp.where(pick, 1.0, sel)
        score = jnp.where(pick, -jnp.inf, score)
    sel_ref[...] = jnp.where(causal, sel, 0.0)


def _cmp_select(bq, kv_cmp, kv_cmp_t, pool_t, tq=256):
    B, S, _ = bq.shape
    n_slc, n_cmp = pool_t.shape
    top_k = min(SLC_TOPK, n_slc)
    return pl.pallas_call(
        functools.partial(_cmp_select_kernel, tq=tq, top_k=top_k),
        grid=(B, S // tq),
        in_specs=[
            pl.BlockSpec((None, tq, B_Q), lambda b, i: (b, i, 0)),
            pl.BlockSpec((None, n_cmp, 2 * HEAD_DIM), lambda b, i: (b, 0, 0)),
            pl.BlockSpec((None, 2 * HEAD_DIM, n_cmp), lambda b, i: (b, 0, 0)),
            pl.BlockSpec((n_slc, n_cmp), lambda b, i: (0, 0)),
        ],
        out_specs=[
            pl.BlockSpec((None, B_Q, tq), lambda b, i: (b, 0, i)),
            pl.BlockSpec((None, n_slc, tq), lambda b, i: (b, 0, i)),
        ],
        out_shape=[jax.ShapeDtypeStruct((B, B_Q, S), F32), jax.ShapeDtypeStruct((B, n_slc, S), F32)],
        compiler_params=_cparams("arbitrary", "arbitrary"),
        name="nsa_cmp_select",
    )(bq, kv_cmp, kv_cmp_t, pool_t)


KV_BLOCK = 128
PAD_ROWS = 16
SLC_TILE = 512


def _values_with_ones(vb, row0, n):
    vt = jnp.concatenate([vb[r, row0:row0 + HEAD_DIM, :] for r in range(n)], axis=1)
    return jnp.concatenate([vt, jnp.ones((PAD_ROWS, n * KV_BLOCK), BF16)], axis=0)


def _nsa_attn_kernel(q_ref, k_ref, ka_ref, vt_ref, sel_ref, ocmp_ref, small_ref, o_ref, *, tq, tk, window):
    i = pl.program_id(1)
    q0 = i * tq
    H = B_HEADS
    d = HEAD_DIM
    qt = _heads_on_lanes(q_ref[...])
    bpt = tk // SLC_LEN
    vpt = tk // KV_BLOCK
    q_pad = jnp.zeros((ka_ref.shape[1] - d - PAD_ROWS, H * tq), BF16)
    pen_pad = jnp.zeros((PAD_ROWS - bpt, tq), F32)

    def scores(j):
        sel = sel_ref[pl.ds(pl.multiple_of(j * bpt, bpt), bpt), :]
        pen = jnp.concatenate([jnp.where(sel > 0.5, 0.0, NEG_INF), pen_pad], axis=0)
        q_aug = jnp.concatenate([qt, jnp.concatenate([pen] * H, axis=1).astype(BF16), q_pad], axis=0)
        return _dot(ka_ref[pl.ds(pl.multiple_of(j * tk, tk), tk), :], q_aug)

    def update(j, s, carry):
        m, acc = carry
        m_new = jnp.maximum(m, jnp.max(s, axis=0, keepdims=True))
        alpha = jnp.exp(m - m_new)
        p = jnp.exp(s - m_new)
        vb = vt_ref[pl.ds(pl.multiple_of(j * vpt, vpt), vpt)]
        return m_new, acc * alpha + _dot(_values_with_ones(vb, 0, vpt), p.astype(BF16))

    def full_tiles(n):
        def body(jg, carry):
            s = [scores(n * jg + r) for r in range(n)]
            for r in range(n):
                carry = update(n * jg + r, s[r], carry)
            return carry
        return body

    qpos = q0 + _iota((tk, tq), 1)

    def causal_tile(j, carry):
        causal = jnp.where(j * tk + _iota((tk, tq), 0) <= qpos, 0.0, NEG_INF)
        return update(j, scores(j) + jnp.concatenate([causal] * H, axis=1), carry)

    n_full_pairs = q0 // (2 * tk)
    n_tiles = (q0 + tq + tk - 1) // tk
    carry = (jnp.full((1, H * tq), NEG_INF, F32), jnp.zeros((d + PAD_ROWS, H * tq), F32))
    carry = lax.fori_loop(0, n_full_pairs, full_tiles(2), carry)
    _, acc = lax.fori_loop(2 * n_full_pairs, n_tiles, causal_tile, carry)
    o_slc = _heads_on_sublanes(acc[0:d] * (1.0 / acc[d:d + 1]), tq)

    span = tq + window
    start = pl.multiple_of(jnp.maximum(q0 - window, 0), KV_BLOCK)
    kw = k_ref[pl.ds(start, span), d:2 * d]
    kpos = start + _iota((span, tq), 0)
    qp = q0 + _iota((span, tq), 1)
    bias = jnp.where((kpos <= qp) & (kpos > qp - window), 0.0, NEG_INF)
    bias = jnp.concatenate([bias] * (H // 2), axis=1)
    vw = _values_with_ones(vt_ref[pl.ds(start // KV_BLOCK, span // KV_BLOCK)], d, span // KV_BLOCK)
    halves = [slice(0, H // 2 * tq), slice(H // 2 * tq, H * tq)]
    s = [_dot(kw, qt[:, hs]) + bias for hs in halves]
    p = [jnp.exp(si - jnp.max(si, axis=0, keepdims=True)) for si in s]
    res = jnp.concatenate([_dot(vw, pi.astype(BF16)) for pi in p], axis=1)
    o_win = _heads_on_sublanes(res[0:d] * (1.0 / res[d:d + 1]), tq)

    gates = jax.nn.sigmoid(small_ref[...]).T
    o_cmp = ocmp_ref[...]
    outs = []
    for h in range(H):
        rows = slice(HEAD_DIM * h, HEAD_DIM * (h + 1))
        outs.append(gates[h:h + 1, :] * o_cmp[rows] + gates[H + h:H + h + 1, :] * o_slc[rows]
                    + gates[2 * H + h:2 * H + h + 1, :] * o_win[rows])
    o_ref[...] = jnp.concatenate(outs, axis=0).T.astype(BF16)


def _slc_keys_with_block_onehot(rope):
    B, S, _ = rope.shape
    t = np.arange(S)[:, None]
    onehot = (t // SLC_LEN) % (SLC_TILE // SLC_LEN) == np.arange(HEAD_DIM)[None, :]
    onehot = jnp.broadcast_to(jnp.asarray(onehot.astype(np.float32)).astype(BF16), (B, S, HEAD_DIM))
    off = A_Q + B_Q + A_KV
    return jnp.concatenate([rope[:, :, off:off + HEAD_DIM], onehot], axis=-1)


def _nsa_attention(rope, vals_t, sel, o_cmp, small, tq=256):
    tk = SLC_TILE
    B, S, _ = rope.shape
    n_slc = sel.shape[1]
    keys_aug = _slc_keys_with_block_onehot(rope)
    return pl.pallas_call(
        functools.partial(_nsa_attn_kernel, tq=tq, tk=tk, window=NSA_WINDOW),
        grid=(B, S // tq),
        in_specs=[
            pl.BlockSpec((None, tq, B_Q), lambda b, i: (b, i, A_Q // B_Q)),
            pl.BlockSpec((None, S, 128), lambda b, i: (b, 0, (A_Q + B_Q + A_KV) // 128)),
            pl.BlockSpec((None, S, 2 * HEAD_DIM), lambda b, i: (b, 0, 0)),
            pl.BlockSpec((None, S // KV_BLOCK, 2 * HEAD_DIM, KV_BLOCK), lambda b, i: (b, 0, A_KV // (2 * HEAD_DIM), 0)),
            pl.BlockSpec((None, n_slc, tq), lambda b, i: (b, 0, i)),
            pl.BlockSpec((None, B_Q, tq), lambda b, i: (b, 0, i)),
            pl.BlockSpec((None, tq, W_SMALL), lambda b, i: (b, i, 0)),
        ],
        out_specs=pl.BlockSpec((None, tq, B_Q), lambda b, i: (b, i, 0)),
        out_shape=jax.ShapeDtypeStruct((B, S, B_Q), BF16),
        compiler_params=_cparams("arbitrary", "arbitrary"),
        name="nsa_attention",
    )(rope, rope, keys_aug, vals_t, sel, o_cmp, small)


def _gdn_kernel(xq_ref, xk_ref, xv_ref, hq_ref, hk_ref, hv_ref, z_ref, small_ref, arow_ref,
                cw_ref, alog_row_ref, dtb_row_ref, alog_col_ref, dtb_col_ref, nw_ref,
                o_ref, q_s, k_s, v_s, g_s, b_s, o_s, state_s, u_s, wq_s, qk_s, kdt_s, gl_s, pq_s, pk_s, pv_s, *, tt):
    i = pl.program_id(1)
    C = GDN_CHUNK
    dk = C_HEAD_DIM
    H = C_HEADS

    @pl.when(i == 0)
    def _():
        state_s[...] = jnp.zeros_like(state_s)

    def conv(x_ref, halo_ref, pad_s, col0):
        pad_s[0:8, :] = jnp.where(i > 0, halo_ref[...], 0.0)
        pad_s[8:8 + tt, :] = x_ref[...]
        w = cw_ref[:, col0:col0 + C_QK]
        y = x_ref[...] * w[3:4]
        for kback in range(1, CONV_WIDTH):
            y = y + pad_s[8 - kback:8 - kback + tt, :] * w[3 - kback:4 - kback]
        return _silu(y)

    qf = conv(xq_ref, hq_ref, pq_s, 0)
    kf = conv(xk_ref, hk_ref, pk_s, C_QK)
    v_s[...] = conv(xv_ref, hv_ref, pv_s, 2 * C_QK)
    for h in range(H):
        cols = slice(dk * h, dk * (h + 1))
        qh = qf[:, cols]
        kh = kf[:, cols]
        q_s[:, cols] = qh * lax.rsqrt(jnp.sum(qh * qh, axis=-1, keepdims=True) + NORM_EPS) * (dk ** -0.5)
        k_s[:, cols] = kh * lax.rsqrt(jnp.sum(kh * kh, axis=-1, keepdims=True) + NORM_EPS)
    small = small_ref[...]
    g_s[...] = -jnp.exp(alog_row_ref[...]) * _softplus(small + dtb_row_ref[...])
    b_s[...] = jax.nn.sigmoid(small)

    r_i = _iota((C, C), 0)
    c_i = _iota((C, C), 1)
    lower = r_i >= c_i
    strict = r_i > c_i
    tri_l = lower.astype(F32)
    tri_u = (r_i <= c_i).astype(F32)
    neg_alog_col = -jnp.exp(alog_col_ref[...])
    dtb_col = dtb_col_ref[...]

    HC = H * C
    GC = GDN_GROUP * C
    rr = _iota((GC, GC), 0)
    cc = _iota((GC, GC), 1)
    same_head = (rr >> 6) == (cc >> 6)
    lower_bd = same_head & (rr >= cc)
    strict_bd = same_head & (rr > cc)
    state_bd = (_iota((GC, GDN_GROUP * dk), 0) >> 6) == (_iota((GC, GDN_GROUP * dk), 1) >> 7)

    def problem(c, grp):
        rows = pl.ds(pl.multiple_of(c * C, C), C)
        gc_col = _dot(tri_l, g_s[rows, :], HIGHEST)
        g_row = neg_alog_col * _softplus(arow_ref[c] + dtb_col)
        gc_row = _dot(g_row, tri_u, HIGHEST)
        beta_all = b_s[rows, :]
        g_last = [gc_col[C - 1:C, SMALL_A + h:SMALL_A + h + 1] for h in range(H)]
        if grp == 0:
            gl_s[c, 0:1, :] = jnp.concatenate([jnp.broadcast_to(jnp.exp(g), (1, dk)) for g in g_last], axis=1)
        heads = range(GDN_GROUP * grp, GDN_GROUP * (grp + 1))
        stack = lambda ref: jnp.concatenate([ref[rows, dk * h:dk * (h + 1)] for h in heads], axis=0)
        gcol = jnp.concatenate([gc_col[:, SMALL_A + h:SMALL_A + h + 1] for h in heads], axis=0)
        grow = jnp.concatenate([gc_row[h:h + 1, :] for h in heads], axis=1)
        beta = jnp.concatenate([beta_all[:, SMALL_B + h:SMALL_B + h + 1] for h in heads], axis=0)
        glast = jnp.concatenate([jnp.broadcast_to(g_last[h], (C, 1)) for h in heads], axis=0)
        k = stack(k_s)
        return dict(c=c, grp=grp, q=stack(q_s), k=k, v=stack(v_s), kb=k * beta, beta=beta, eg=jnp.exp(gcol),
                    kd=k * jnp.exp(glast - gcol),
                    decay=jnp.where(lower_bd, jnp.exp(jnp.where(lower_bd, gcol - grow, 0.0)), 0.0))

    def prepare_group(cg, carry):
        ps = [problem(PREP_GROUP * cg + r, grp) for r in range(PREP_GROUP) for grp in range(H // GDN_GROUP)]
        for p in ps:
            p["gram"] = _dot_nt(jnp.concatenate([p["kb"], p["q"]], axis=0).astype(BF16), p["k"].astype(BF16))
        for p in ps:
            p["qk"] = jnp.where(lower_bd, p["gram"][GC:2 * GC] * p["decay"], 0.0)
            p["x"] = -jnp.where(strict_bd, p["gram"][0:GC] * p["decay"], 0.0)
            p["pw16"] = p["x"].astype(BF16)
        for p in ps:
            p["pw"] = _dot(p["pw16"], p["pw16"])
        for _ in range(4):
            for p in ps:
                pw16 = p["pw"].astype(BF16)
                p["x"], p["pw"] = p["x"] + p["pw"] + _dot(pw16, p["x"].astype(BF16)), _dot(pw16, pw16)
        for p in ps:
            p["x"] = p["x"] + p["pw"] + _dot(p["pw"].astype(BF16), p["x"].astype(BF16))
        for p in ps:
            rhs = jnp.concatenate([p["v"] * p["beta"], p["kb"] * p["eg"]], axis=1)
            p["y"] = rhs + _dot(p["x"].astype(BF16), rhs.astype(BF16))
        for p in ps:
            c, grp, y = p["c"], p["grp"], p["y"]
            grows = slice(GC * grp, GC * (grp + 1))
            u_s[c, grows, :] = y[:, 0:dk]
            wq_s[c, grp, 0:GC, :] = y[:, dk:2 * dk].astype(BF16)
            wq_s[c, grp, GC:2 * GC, :] = (p["q"] * p["eg"]).astype(BF16)
            qk_s[c, grows, :] = p["qk"].astype(BF16)
            kdt_s[c, :, grows] = p["kd"].T.astype(BF16)
        return carry

    lax.fori_loop(0, tt // (PREP_GROUP * C), prepare_group, 0)

    def head_blocks(x, row0):
        return jnp.concatenate(
            [x[row0 + C * r:row0 + C * (r + 1), dk * r:dk * (r + 1)] for r in range(GDN_GROUP)], axis=0)

    def scan(c, carry):
        rows = pl.ds(pl.multiple_of(c * C, C), C)
        groups = range(H // GDN_GROUP)
        lanes = [slice(GDN_GROUP * dk * g, GDN_GROUP * dk * (g + 1)) for g in groups]
        grows = [slice(GC * g, GC * (g + 1)) for g in groups]
        st = [state_s[:, lanes[g]] for g in groups]
        d1 = [_dot(wq_s[c, g], st[g].astype(BF16)) for g in groups]
        v16 = [(u_s[c, grows[g], :] - head_blocks(d1[g], 0)).astype(BF16) for g in groups]
        o = [head_blocks(d1[g], GC) + _dot(qk_s[c, grows[g], :], v16[g]) for g in groups]
        for g in groups:
            v_bd = jnp.where(state_bd, jnp.concatenate([v16[g]] * GDN_GROUP, axis=1), jnp.zeros((), BF16))
            state_s[:, lanes[g]] = st[g] * gl_s[c, 0:1, lanes[g]] + _dot(kdt_s[c, :, grows[g]], v_bd)
            for r in range(GDN_GROUP):
                h = GDN_GROUP * g + r
                o_s[rows, dk * h:dk * (h + 1)] = o[g][C * r:C * (r + 1)]
        return carry

    lax.fori_loop(0, tt // C, scan, 0)

    nw = nw_ref[...]
    z = z_ref[...]
    for h in range(H):
        cols = slice(dk * h, dk * (h + 1))
        oh = o_s[:, cols]
        y = oh * lax.rsqrt(jnp.mean(oh * oh, axis=-1, keepdims=True) + NORM_EPS) * nw
        o_ref[:, cols] = (y * _silu(z[:, cols])).astype(BF16)


def _gated_delta_net(gdn, small, a_rows, conv_w, alog_row, dtb_row, alog_col, dtb_col, norm_w, tt=512):
    B, S, _ = gdn.shape
    nch = tt // GDN_CHUNK
    hc = C_HEADS * GDN_CHUNK
    part = lambda j: pl.BlockSpec((None, tt, C_QK), lambda b, i: (b, i, j))
    halo = lambda j: pl.BlockSpec((None, 8, C_QK), lambda b, i: (b, jnp.maximum(i * (tt // 8) - 1, 0), j))
    full = lambda a: pl.BlockSpec(a.shape, lambda b, i: (0,) * a.ndim)
    return pl.pallas_call(
        functools.partial(_gdn_kernel, tt=tt),
        grid=(B, S // tt),
        in_specs=[part(0), part(1), part(2), halo(0), halo(1), halo(2), part(3),
                  pl.BlockSpec((None, tt, W_SMALL), lambda b, i: (b, i, 0)),
                  pl.BlockSpec((None, nch, 8, GDN_CHUNK), lambda b, i: (b, i, 0, 0)),
                  full(conv_w), full(alog_row), full(dtb_row), full(alog_col), full(dtb_col), full(norm_w)],
        out_specs=pl.BlockSpec((None, tt, C_QK), lambda b, i: (b, i, 0)),
        out_shape=jax.ShapeDtypeStruct((B, S, C_QK), BF16),
        scratch_shapes=[pltpu.VMEM((tt, C_QK), F32), pltpu.VMEM((tt, C_QK), F32), pltpu.VMEM((tt, C_QK), F32),
                        pltpu.VMEM((tt, W_SMALL), F32), pltpu.VMEM((tt, W_SMALL), F32),
                        pltpu.VMEM((tt, C_QK), F32), pltpu.VMEM((C_HEAD_DIM, C_QK), F32),
                        pltpu.VMEM((nch, hc, C_HEAD_DIM), F32), pltpu.VMEM((nch, C_HEADS // GDN_GROUP, 2 * GDN_GROUP * GDN_CHUNK, C_HEAD_DIM), BF16),
                        pltpu.VMEM((nch, hc, GDN_GROUP * GDN_CHUNK), BF16), pltpu.VMEM((nch, C_HEAD_DIM, hc), BF16),
                        pltpu.VMEM((nch, 8, C_QK), F32)] + [pltpu.VMEM((tt + 8, C_QK), F32)] * 3,
        compiler_params=_cparams("arbitrary", "arbitrary"),
        name="gated_delta_net",
    )(gdn, gdn, gdn, gdn, gdn, gdn, gdn, small, a_rows, conv_w, alog_row, dtb_row, alog_col, dtb_col, norm_w)


def _ffn_kernel(x_ref, oa_ref, ob_ref, oc_ref, wo_ref, gtm_ref, g_ref, sc_ref, sh_ref, gt_ref, wgu_ref, wd_ref,
                o_ref, *, tf):
    mix = _dot(oa_ref[...], wo_ref[0:A_Q, :])
    mix = mix + _dot(ob_ref[...], wo_ref[A_Q:A_Q + B_Q, :])
    mix = mix + _dot(oc_ref[...], wo_ref[A_Q + B_Q:, :])
    x = x_ref[...] + gtm_ref[...] * mix
    h = _rms_mod(x, g_ref[...], sc_ref[...], sh_ref[...]).astype(BF16)
    acc = None
    for f in range(D_FF // tf):
        gate = _dot(h, wgu_ref[:, tf * f:tf * (f + 1)])
        up = _dot(h, wgu_ref[:, D_FF + tf * f:D_FF + tf * (f + 1)])
        part = _dot((_silu(gate) * up).astype(BF16), wd_ref[tf * f:tf * (f + 1), :])
        acc = part if acc is None else acc + part
    o_ref[...] = x + gt_ref[...] * acc


def _outproj_ffn(x, o_a, o_b, o_c, w_out, gain, mod, w_gate_up, w_down, tm=512, tf=1408):
    B, S, D = x.shape
    row = lambda b, i: (b, i, 0)
    resident = lambda a: pl.BlockSpec(a.shape, lambda b, i: (0, 0), pipeline_mode=pl.Buffered(1))
    mod_col = lambda j: pl.BlockSpec((None, 1, D), lambda b, i: (b, 0, j))
    return pl.pallas_call(
        functools.partial(_ffn_kernel, tf=tf),
        grid=(B, S // tm),
        in_specs=[
            pl.BlockSpec((None, tm, D), row),
            pl.BlockSpec((None, tm, A_Q), row),
            pl.BlockSpec((None, tm, B_Q), row),
            pl.BlockSpec((None, tm, C_QK), row),
            resident(w_out),
            mod_col(2),
            pl.BlockSpec((1, D), lambda b, i: (0, 0)),
            mod_col(4),
            mod_col(3),
            mod_col(5),
            resident(w_gate_up),
            resident(w_down),
        ],
        out_specs=pl.BlockSpec((None, tm, D), row),
        out_shape=jax.ShapeDtypeStruct((B, S, D), F32),
        compiler_params=_cparams("arbitrary", "arbitrary"),
        name="outproj_swiglu_ffn",
    )(x, o_a, o_b, o_c, w_out, mod, gain, mod, mod, mod, w_gate_up, w_down)


def _final_norm_kernel(x_ref, g_ref, o_ref):
    x = x_ref[...]
    o_ref[...] = x * lax.rsqrt(jnp.mean(x * x, axis=-1, keepdims=True) + NORM_EPS) * g_ref[...]


def _final_norm(x, gain, tm=512):
    B, S, D = x.shape
    return pl.pallas_call(
        _final_norm_kernel,
        grid=(B, S // tm),
        in_specs=[pl.BlockSpec((None, tm, D), lambda b, i: (b, i, 0)), pl.BlockSpec((1, D), lambda b, i: (0, 0))],
        out_specs=pl.BlockSpec((None, tm, D), lambda b, i: (b, i, 0)),
        out_shape=jax.ShapeDtypeStruct((B, S, D), F32),
        compiler_params=_cparams("arbitrary", "arbitrary"),
        name="final_norm",
    )(x, gain)


def _rope_tables(seq):
    inv = 1.0 / (ROPE_THETA ** (jnp.arange(0, HEAD_DIM, 2, dtype=F32) / HEAD_DIM))
    ang = jnp.arange(seq, dtype=F32)[:, None] * inv[None, :]
    cos, sin = jnp.cos(ang), jnp.sin(ang)
    zero = jnp.zeros_like(sin)
    cos_t = jnp.tile(cos, (1, 4))
    sa_t = jnp.tile(jnp.concatenate([-sin, zero], axis=1), (1, 2))
    sb_t = jnp.tile(jnp.concatenate([zero, sin], axis=1), (1, 2))
    return cos_t, sa_t, sb_t


def _permute_w_in(w_in):
    pts = np.cumsum(IN_SPLITS)[:-1].tolist()
    aq, ak, av, bq, bkc, bvc, bks, bvs, bkw, bvw, bg, cqkv, cz, ca, cb = jnp.split(w_in, pts, axis=-1)
    pad = jnp.zeros(w_in.shape[:-1] + (W_SMALL - B_GATES - 2 * C_HEADS,), w_in.dtype)
    return jnp.concatenate([aq, bq, ak, bks, bkw, av, bvs, bvw, bkc, bvc, bg, ca, cb, pad, cqkv, cz], axis=-1).astype(BF16)


def _compress_weights(k_w1, k_w2, v_w1, v_w2, pe_k, pe_v):
    L = k_w1.shape[0]
    half = CMP_STRIDE * HEAD_DIM

    def first_layer(lo):
        wk = k_w1[:, lo:lo + half].reshape(L, CMP_STRIDE, HEAD_DIM, CMP_HIDDEN)
        wv = v_w1[:, lo:lo + half].reshape(L, CMP_STRIDE, HEAD_DIM, CMP_HIDDEN)
        z = jnp.zeros_like(wk)
        w = jnp.concatenate([jnp.concatenate([wk, z], axis=-1), jnp.concatenate([z, wv], axis=-1)], axis=2)
        return w.reshape(L, 2 * half, 2 * CMP_HIDDEN).astype(BF16)

    def pe_rows(lo):
        return jnp.concatenate([pe_k[:, lo:lo + CMP_STRIDE], pe_v[:, lo:lo + CMP_STRIDE]], axis=-1).reshape(L, 1, 2 * half)

    z2 = jnp.zeros_like(k_w2)
    w2 = jnp.concatenate([jnp.concatenate([k_w2, z2], axis=-1), jnp.concatenate([z2, v_w2], axis=-1)], axis=1).astype(BF16)
    return pe_rows(0), pe_rows(CMP_STRIDE), first_layer(0), first_layer(half), w2


def _pool_matrix(n_cmp_rows, n_slc):
    ratio = SLC_LEN // CMP_STRIDE
    n = np.arange(n_cmp_rows)[:, None]
    j = np.arange(n_slc)[None, :]
    return jnp.asarray(((n // ratio == j) | (n == ratio * j - 1)).astype(np.float32))


def _lane_row(v, lane0):
    L, n = v.shape
    return jnp.zeros((L, 1, 128), F32).at[:, 0, lane0:lane0 + n].set(v)


def _sublane_col(v):
    L, n = v.shape
    return jnp.zeros((L, 8, 1), F32).at[:, :n, 0].set(v)


def kernel(x, c, norm_mix, norm_ffn, ada_w, ada_b, w_in, attn_sinks, cmp_k_w1, cmp_k_w2, cmp_v_w1, cmp_v_w2, cmp_pe_k, cmp_pe_v, gdn_conv_w, gdn_A_log, gdn_dt_bias, gdn_norm, w_out, w_gate_up, w_down, final_norm):
    B, S, D = x.shape
    L = w_in.shape[0]
    n16 = S // CMP_STRIDE
    n_slc = S // SLC_LEN

    cos_t, sa_t, sb_t = _rope_tables(S)
    mod = _adaln_mod(c, ada_w, ada_b)
    pea, peb, w1a, w1b, w2c = _compress_weights(cmp_k_w1, cmp_k_w2, cmp_v_w1, cmp_v_w2, cmp_pe_k, cmp_pe_v)
    pool_t = _pool_matrix(n16, n_slc).T
    per_layer = dict(
        mod=mod,
        norm_mix=norm_mix.reshape(L, 1, D), norm_ffn=norm_ffn.reshape(L, 1, D),
        w_in=_permute_w_in(w_in), sinks=attn_sinks,
        pea=pea, peb=peb, w1a=w1a, w1b=w1b, w2c=w2c,
        conv_w=gdn_conv_w,
        alog_row=_lane_row(gdn_A_log, SMALL_A), dtb_row=_lane_row(gdn_dt_bias, SMALL_A),
        alog_col=_sublane_col(gdn_A_log), dtb_col=_sublane_col(gdn_dt_bias),
        gdn_norm=gdn_norm.reshape(L, 1, C_HEAD_DIM),
        w_out=w_out.astype(BF16), w_gate_up=w_gate_up.astype(BF16), w_down=w_down.astype(BF16),
    )

    def layer(xc, p):
        rope, bq, vals, cmp_in, small, gdn = _in_projection(xc, p["norm_mix"], p["mod"], p["w_in"], cos_t, sa_t, sb_t)
        vals_all_t = vals.reshape(B, S // KV_BLOCK, KV_BLOCK, W_V).transpose(0, 1, 3, 2)
        o_a = _swa_attention(rope, vals_all_t, p["sinks"])
        kv_cmp, kv_cmp_t = _compress(cmp_in.reshape(B, n16, CMP_STRIDE * W_CMP), p["pea"], p["peb"], p["w1a"], p["w1b"], p["w2c"])
        o_cmp_t, sel = _cmp_select(bq, kv_cmp, kv_cmp_t, pool_t)
        o_b = _nsa_attention(rope, vals_all_t, sel, o_cmp_t, small)
        a_rows = small[:, :, SMALL_A:SMALL_A + 8].reshape(B, S // GDN_CHUNK, GDN_CHUNK, 8).transpose(0, 1, 3, 2)
        o_c = _gated_delta_net(gdn, small, a_rows, p["conv_w"], p["alog_row"], p["dtb_row"], p["alog_col"], p["dtb_col"], p["gdn_norm"])
        xc = _outproj_ffn(xc, o_a, o_b, o_c, p["w_out"], p["norm_ffn"], p["mod"], p["w_gate_up"], p["w_down"])
        return xc, None

    x, _ = lax.scan(layer, x, per_layer)
    return _final_norm(x, final_norm.reshape(1, D))
```

```python
import functools
import math

import numpy as np
import jax
import jax.numpy as jnp
from jax import lax
from jax.experimental import pallas as pl
from jax.experimental.pallas import tpu as pltpu

F32 = jnp.float32
BF16 = jnp.bfloat16
HIGHEST = lax.Precision.HIGHEST

D_MODEL = 1024
DEPTH = 4
HEAD_DIM = 64
ATTN_SCALE = HEAD_DIM ** -0.5
ROPE_THETA = 10000.0
NEG_INF = -1e30
NORM_EPS = 1e-6

A_HEADS = 4
A_KV_HEADS = 2
A_WINDOW = 128

B_HEADS = 4
CMP_STRIDE = 16
CMP_LEN = 32
CMP_HIDDEN = 256
SLC_LEN = 64
SLC_TOPK = 16
NSA_WINDOW = 512
SLC_FORCED_SCORE = 1e9

C_HEAD_DIM = 128
C_HEADS = 4
CONV_WIDTH = 4
GDN_CHUNK = 64
C_QK = C_HEADS * C_HEAD_DIM

D_FF = 2816

A_Q = A_HEADS * HEAD_DIM
A_KV = A_KV_HEADS * HEAD_DIM
B_Q = B_HEADS * HEAD_DIM
B_KV = HEAD_DIM
B_GATES = 3 * B_HEADS
IN_SPLITS = (A_Q, A_KV, A_KV, B_Q, B_KV, B_KV, B_KV, B_KV, B_KV, B_KV, B_GATES, 3 * C_QK, C_QK, C_HEADS, C_HEADS)

W_ROPE = 768
W_V = 256
W_CMP = 128
W_SMALL = 128
W_GDN = 2048
W_TOTAL = W_ROPE + W_V + W_CMP + W_SMALL + W_GDN
SMALL_A = B_GATES
SMALL_B = B_GATES + C_HEADS

VMEM_LIMIT = 56 * 1024 * 1024
PREP_GROUP = 8
GDN_GROUP = 2

NT_DIMS = (((1,), (1,)), ((), ()))


def _cparams(*sem):
    return pltpu.CompilerParams(dimension_semantics=sem, vmem_limit_bytes=VMEM_LIMIT)


def _iota(shape, dim):
    return lax.broadcasted_iota(jnp.int32, shape, dim)


def _dot(a, b, precision=None):
    return jnp.dot(a, b, preferred_element_type=F32, precision=precision)


def _dot_nt(a, b):
    return lax.dot_general(a, b, NT_DIMS, preferred_element_type=F32)


def _silu(x):
    return x * jax.nn.sigmoid(x)


def _softplus(x):
    return jnp.maximum(x, 0.0) + jnp.log(1.0 + jnp.exp(-jnp.abs(x)))


def _rms_mod(x, gain, sc, sh):
    y = x * lax.rsqrt(jnp.mean(x * x, axis=-1, keepdims=True) + NORM_EPS)
    return (y * gain) * (1.0 + sc) + sh


def _mod_kernel(c_ref, w_ref, b_ref, o_ref):
    o_ref[...] = _dot(_silu(c_ref[...]), w_ref[...], HIGHEST) + b_ref[...]


def _adaln_mod(c, ada_w, ada_b):
    L, D, N = ada_w.shape
    tn = 1536
    c8 = jnp.zeros((8, D), F32).at[: c.shape[0]].set(c)
    out = pl.pallas_call(
        _mod_kernel,
        grid=(L, N // tn),
        in_specs=[
            pl.BlockSpec((8, D), lambda l, j: (0, 0)),
            pl.BlockSpec((None, D, tn), lambda l, j: (l, 0, j)),
            pl.BlockSpec((None, 1, tn), lambda l, j: (l, 0, j)),
        ],
        out_specs=pl.BlockSpec((None, 8, tn), lambda l, j: (l, 0, j)),
        out_shape=jax.ShapeDtypeStruct((L, 8, N), F32),
        compiler_params=_cparams("arbitrary", "arbitrary"),
        name="adaln_mod",
    )(c8, ada_w, ada_b.reshape(L, 1, N))
    return out.reshape(L, 8, 1, N)


def _inproj_kernel(x_ref, g_ref, sc_ref, sh_ref, w_ref, cos_ref, sa_ref, sb_ref,
                   rope_ref, bq_ref, v_ref, cmp_ref, small_ref, gdn_ref):
    h = _rms_mod(x_ref[...], g_ref[...], sc_ref[...], sh_ref[...]).astype(BF16)
    yr = _dot(h, w_ref[:, 0:W_ROPE])
    bq_ref[...] = yr[:, A_Q:A_Q + B_Q].astype(BF16)
    c, sa, sb = cos_ref[...], sa_ref[...], sb_ref[...]
    for g in range(W_ROPE // 128):
        xg = yr[:, 128 * g:128 * (g + 1)]
        rot = xg * c + pltpu.roll(xg, 96, 1) * sa + pltpu.roll(xg, 32, 1) * sb
        rope_ref[:, 128 * g:128 * (g + 1)] = rot.astype(BF16)
    o = W_ROPE
    yp = _dot(h, w_ref[:, o:o + W_V + W_CMP + W_SMALL])
    v_ref[...] = yp[:, 0:W_V].astype(BF16)
    cmp_ref[...] = yp[:, W_V:W_V + W_CMP]
    small_ref[...] = yp[:, W_V + W_CMP:W_V + W_CMP + W_SMALL]
    o += W_V + W_CMP + W_SMALL
    gdn_ref[...] = _dot(h, w_ref[:, o:o + W_GDN])


def _in_projection(x, gain, mod, w, cos_t, sa_t, sb_t, tm=512):
    B, S, D = x.shape
    row = lambda b, i: (b, i, 0)
    outs = pl.pallas_call(
        _inproj_kernel,
        grid=(B, S // tm),
        in_specs=[
            pl.BlockSpec((None, tm, D), row),
            pl.BlockSpec((1, D), lambda b, i: (0, 0)),
            pl.BlockSpec((None, 1, D), lambda b, i: (b, 0, 1)),
            pl.BlockSpec((None, 1, D), lambda b, i: (b, 0, 0)),
            pl.BlockSpec((D, W_TOTAL), lambda b, i: (0, 0)),
            pl.BlockSpec((tm, 128), lambda b, i: (i, 0)),
            pl.BlockSpec((tm, 128), lambda b, i: (i, 0)),
            pl.BlockSpec((tm, 128), lambda b, i: (i, 0)),
        ],
        out_specs=[
            pl.BlockSpec((None, tm, W_ROPE), row),
            pl.BlockSpec((None, tm, B_Q), row),
            pl.BlockSpec((None, tm, W_V), row),
            pl.BlockSpec((None, tm, W_CMP), row),
            pl.BlockSpec((None, tm, W_SMALL), row),
            pl.BlockSpec((None, tm, W_GDN), row),
        ],
        out_shape=[
            jax.ShapeDtypeStruct((B, S, W_ROPE), BF16),
            jax.ShapeDtypeStruct((B, S, B_Q), BF16),
            jax.ShapeDtypeStruct((B, S, W_V), BF16),
            jax.ShapeDtypeStruct((B, S, W_CMP), F32),
            jax.ShapeDtypeStruct((B, S, W_SMALL), F32),
            jax.ShapeDtypeStruct((B, S, W_GDN), F32),
        ],
        compiler_params=_cparams("arbitrary", "arbitrary"),
        name="in_projection",
    )(x, gain, mod, mod, w, cos_t, sa_t, sb_t)
    return outs


def _swa_kernel(sink_ref, q_ref, k_ref, vt_ref, o_ref, *, tq, window):
    q0 = pl.program_id(1) * tq
    span = tq + window
    start = pl.multiple_of(jnp.maximum(q0 - window, 0), KV_BLOCK)
    qt = q_ref[...].astype(F32).T * ATTN_SCALE
    kpos = start + _iota((span, tq), 0)
    qpos = q0 + _iota((span, tq), 1)
    bias = jnp.where((kpos <= qpos) & (kpos > qpos - window), 0.0, NEG_INF)
    vb = vt_ref[pl.ds(start // KV_BLOCK, span // KV_BLOCK)]
    group = A_HEADS // A_KV_HEADS
    first_head = _iota((1, group * tq), 1) < tq
    bias = jnp.concatenate([bias] * group, axis=1)
    groups = range(A_KV_HEADS)
    qg = [jnp.concatenate([qt[HEAD_DIM * h:HEAD_DIM * (h + 1)] for h in range(group * g, group * (g + 1))],
                          axis=1).astype(BF16) for g in groups]
    s = [_dot(k_ref[pl.ds(start, span), HEAD_DIM * g:HEAD_DIM * (g + 1)], qg[g]) + bias for g in groups]
    sink = [jnp.where(first_head, sink_ref[group * g], sink_ref[group * g + 1]) for g in groups]
    m = [jnp.maximum(jnp.max(s[g], axis=0, keepdims=True), sink[g]) for g in groups]
    p = [jnp.exp(s[g] - m[g]) for g in groups]
    den = [jnp.sum(p[g], axis=0, keepdims=True) + jnp.exp(sink[g] - m[g]) for g in groups]
    vt = [jnp.concatenate([vb[r, HEAD_DIM * g:HEAD_DIM * (g + 1), :] for r in range(span // KV_BLOCK)], axis=1)
          for g in groups]
    o_t = [_dot(vt[g], p[g].astype(BF16)) * (1.0 / den[g]) for g in groups]
    outs = [o_t[g][:, tq * r:tq * (r + 1)] for g in groups for r in range(group)]
    o_ref[...] = jnp.concatenate(outs, axis=0).T.astype(BF16)


def _swa_attention(rope, vals_t, sinks, tq=256):
    assert A_HEADS // A_KV_HEADS == 2
    B, S, _ = rope.shape
    return pl.pallas_call(
        functools.partial(_swa_kernel, tq=tq, window=A_WINDOW),
        grid=(B, S // tq),
        in_specs=[
            pl.BlockSpec(memory_space=pltpu.SMEM),
            pl.BlockSpec((None, tq, A_Q), lambda b, i: (b, i, 0)),
            pl.BlockSpec((None, S, A_KV), lambda b, i: (b, 0, (A_Q + B_Q) // A_KV)),
            pl.BlockSpec((None, S // KV_BLOCK, A_KV, KV_BLOCK), lambda b, i: (b, 0, 0, 0)),
        ],
        out_specs=pl.BlockSpec((None, tq, A_Q), lambda b, i: (b, i, 0)),
        out_shape=jax.ShapeDtypeStruct((B, S, A_Q), BF16),
        compiler_params=_cparams("arbitrary", "arbitrary"),
        name="swa_attention",
    )(sinks, rope, rope, vals_t)


def _compress_kernel(x_ref, pea_ref, peb_ref, w1a_ref, w1b_ref, w2_ref, o_ref, ot_ref):
    x = x_ref[...]
    n = x.shape[0]
    a = _dot((x + pea_ref[...]).astype(BF16), w1a_ref[...])
    b = _dot((x + peb_ref[...]).astype(BF16), w1b_ref[...])
    hid = _silu(a + pltpu.roll(b, n - 1, 0))
    out = _dot(hid.astype(BF16), w2_ref[...])
    o_ref[...] = out.astype(BF16)
    ot_ref[...] = out.T.astype(BF16)


def _compress(xc, pea, peb, w1a, w1b, w2):
    B, n, K = xc.shape
    full = lambda a: pl.BlockSpec(a.shape, lambda b: (0,) * a.ndim)
    return pl.pallas_call(
        _compress_kernel,
        grid=(B,),
        in_specs=[pl.BlockSpec((None, n, K), lambda b: (b, 0, 0)), full(pea), full(peb), full(w1a), full(w1b), full(w2)],
        out_specs=[pl.BlockSpec((None, n, 2 * HEAD_DIM), lambda b: (b, 0, 0)),
                   pl.BlockSpec((None, 2 * HEAD_DIM, n), lambda b: (b, 0, 0))],
        out_shape=[jax.ShapeDtypeStruct((B, n, 2 * HEAD_DIM), BF16), jax.ShapeDtypeStruct((B, 2 * HEAD_DIM, n), BF16)],
        compiler_params=_cparams("arbitrary"),
        name="nsa_compress",
    )(xc, pea, peb, w1a, w1b, w2)


def _heads_on_lanes(q):
    qt = q.astype(F32).T * ATTN_SCALE
    return jnp.concatenate([qt[HEAD_DIM * h:HEAD_DIM * (h + 1)] for h in range(B_HEADS)], axis=1).astype(BF16)


def _heads_on_sublanes(ot, tq):
    return jnp.concatenate([ot[:, tq * h:tq * (h + 1)] for h in range(B_HEADS)], axis=0)


def _cmp_select_kernel(q_ref, k_ref, vt_ref, pool_ref, o_ref, sel_ref, *, tq, top_k):
    q0 = pl.program_id(1) * tq
    H = B_HEADS
    n_cmp = k_ref.shape[0]
    n_slc = pool_ref.shape[0]
    qt = _heads_on_lanes(q_ref[...])
    valid = _iota((n_cmp, tq), 0) * CMP_STRIDE + (CMP_LEN - 1) <= q0 + _iota((n_cmp, tq), 1)
    valid = jnp.concatenate([valid] * (H // 2), axis=1)
    halves = [slice(0, H // 2 * tq), slice(H // 2 * tq, H * tq)]
    kc = k_ref[:, 0:HEAD_DIM]
    vc_t = vt_ref[HEAD_DIM:2 * HEAD_DIM, :]
    s = [jnp.where(valid, _dot(kc, qt[:, hs]), NEG_INF) for hs in halves]
    e = [jnp.where(valid, jnp.exp(si - jnp.max(si, axis=0, keepdims=True)), 0.0) for si in s]
    den = [jnp.sum(ei, axis=0, keepdims=True) for ei in e]
    p = [ei * (1.0 / jnp.where(di > 0.0, di, 1.0)) for ei, di in zip(e, den)]
    o_t = jnp.concatenate([_dot(vc_t, pi.astype(BF16)) for pi in p], axis=1)
    o_ref[...] = _heads_on_sublanes(o_t, tq)
    psum = None
    for pi in p:
        for r in range(H // 2):
            part = pi[:, tq * r:tq * (r + 1)]
            psum = part if psum is None else psum + part

    imp = _dot(pool_ref[...], psum, HIGHEST)
    blk = _iota((n_slc, tq), 0)
    tt = q0 + _iota((n_slc, tq), 1)
    cur = tt >> 6
    forced = (blk == 0) | (blk == cur) | (blk == cur - 1)
    causal = blk * SLC_LEN <= tt
    score = jnp.where(forced, SLC_FORCED_SCORE, jnp.where(causal, imp, NEG_INF))
    sel = jnp.zeros((n_slc, tq), F32)
    blk_f = blk.astype(F32)
    for _ in range(top_k):
        mx = jnp.max(score, axis=0, keepdims=True)
        first = jnp.min(jnp.where(score == mx, blk_f, float(n_slc)), axis=0, keepdims=True)
        pick = blk_f == first
        sel = jnp.where(pick, 1.0, sel)
        score = jnp.where(pick, -jnp.inf, score)
    sel_ref[...] = jnp.where(causal, sel, 0.0)


def _cmp_select(bq, kv_cmp, kv_cmp_t, pool_t, tq=256):
    B, S, _ = bq.shape
    n_slc, n_cmp = pool_t.shape
    top_k = min(SLC_TOPK, n_slc)
    return pl.pallas_call(
        functools.partial(_cmp_select_kernel, tq=tq, top_k=top_k),
        grid=(B, S // tq),
        in_specs=[
            pl.BlockSpec((None, tq, B_Q), lambda b, i: (b, i, 0)),
            pl.BlockSpec((None, n_cmp, 2 * HEAD_DIM), lambda b, i: (b, 0, 0)),
            pl.BlockSpec((None, 2 * HEAD_DIM, n_cmp), lambda b, i: (b, 0, 0)),
            pl.BlockSpec((n_slc, n_cmp), lambda b, i: (0, 0)),
        ],
        out_specs=[
            pl.BlockSpec((None, B_Q, tq), lambda b, i: (b, 0, i)),
            pl.BlockSpec((None, n_slc, tq), lambda b, i: (b, 0, i)),
        ],
        out_shape=[jax.ShapeDtypeStruct((B, B_Q, S), F32), jax.ShapeDtypeStruct((B, n_slc, S), F32)],
        compiler_params=_cparams("arbitrary", "arbitrary"),
        name="nsa_cmp_select",
    )(bq, kv_cmp, kv_cmp_t, pool_t)


KV_BLOCK = 128
PAD_ROWS = 16
SLC_TILE = 512


def _values_with_ones(vb, row0, n):
    vt = jnp.concatenate([vb[r, row0:row0 + HEAD_DIM, :] for r in range(n)], axis=1)
    return jnp.concatenate([vt, jnp.ones((PAD_ROWS, n * KV_BLOCK), BF16)], axis=0)


def _nsa_attn_kernel(q_ref, k_ref, ka_ref, vt_ref, sel_ref, ocmp_ref, small_ref, o_ref, *, tq, tk, window):
    i = pl.program_id(1)
    q0 = i * tq
    H = B_HEADS
    d = HEAD_DIM
    qt = _heads_on_lanes(q_ref[...])
    bpt = tk // SLC_LEN
    vpt = tk // KV_BLOCK
    q_pad = jnp.zeros((ka_ref.shape[1] - d - PAD_ROWS, H * tq), BF16)
    pen_pad = jnp.zeros((PAD_ROWS - bpt, tq), F32)

    def scores(j):
        sel = sel_ref[pl.ds(pl.multiple_of(j * bpt, bpt), bpt), :]
        pen = jnp.concatenate([jnp.where(sel > 0.5, 0.0, NEG_INF), pen_pad], axis=0)
        q_aug = jnp.concatenate([qt, jnp.concatenate([pen] * H, axis=1).astype(BF16), q_pad], axis=0)
        return _dot(ka_ref[pl.ds(pl.multiple_of(j * tk, tk), tk), :], q_aug)

    def update(j, s, carry):
        m, acc = carry
        m_new = jnp.maximum(m, jnp.max(s, axis=0, keepdims=True))
        alpha = jnp.exp(m - m_new)
        p = jnp.exp(s - m_new)
        vb = vt_ref[pl.ds(pl.multiple_of(j * vpt, vpt), vpt)]
        return m_new, acc * alpha + _dot(_values_with_ones(vb, 0, vpt), p.astype(BF16))

    def full_tiles(n):
        def body(jg, carry):
            s = [scores(n * jg + r) for r in range(n)]
            for r in range(n):
                carry = update(n * jg + r, s[r], carry)
            return carry
        return body

    qpos = q0 + _iota((tk, tq), 1)

    def causal_tile(j, carry):
        causal = jnp.where(j * tk + _iota((tk, tq), 0) <= qpos, 0.0, NEG_INF)
        return update(j, scores(j) + jnp.concatenate([causal] * H, axis=1), carry)

    n_full_pairs = q0 // (2 * tk)
    n_tiles = (q0 + tq + tk - 1) // tk
    carry = (jnp.full((1, H * tq), NEG_INF, F32), jnp.zeros((d + PAD_ROWS, H * tq), F32))
    carry = lax.fori_loop(0, n_full_pairs, full_tiles(2), carry)
    _, acc = lax.fori_loop(2 * n_full_pairs, n_tiles, causal_tile, carry)
    o_slc = _heads_on_sublanes(acc[0:d] * (1.0 / acc[d:d + 1]), tq)

    span = tq + window
    start = pl.multiple_of(jnp.maximum(q0 - window, 0), KV_BLOCK)
    kw = k_ref[pl.ds(start, span), d:2 * d]
    kpos = start + _iota((span, tq), 0)
    qp = q0 + _iota((span, tq), 1)
    bias = jnp.where((kpos <= qp) & (kpos > qp - window), 0.0, NEG_INF)
    bias = jnp.concatenate([bias] * (H // 2), axis=1)
    vw = _values_with_ones(vt_ref[pl.ds(start // KV_BLOCK, span // KV_BLOCK)], d, span // KV_BLOCK)
    halves = [slice(0, H // 2 * tq), slice(H // 2 * tq, H * tq)]
    s = [_dot(kw, qt[:, hs]) + bias for hs in halves]
    p = [jnp.exp(si - jnp.max(si, axis=0, keepdims=True)) for si in s]
    res = jnp.concatenate([_dot(vw, pi.astype(BF16)) for pi in p], axis=1)
    o_win = _heads_on_sublanes(res[0:d] * (1.0 / res[d:d + 1]), tq)

    gates = jax.nn.sigmoid(small_ref[...]).T
    o_cmp = ocmp_ref[...]
    outs = []
    for h in range(H):
        rows = slice(HEAD_DIM * h, HEAD_DIM * (h + 1))
        outs.append(gates[h:h + 1, :] * o_cmp[rows] + gates[H + h:H + h + 1, :] * o_slc[rows]
                    + gates[2 * H + h:2 * H + h + 1, :] * o_win[rows])
    o_ref[...] = jnp.concatenate(outs, axis=0).T.astype(BF16)


def _slc_keys_with_block_onehot(rope):
    B, S, _ = rope.shape
    t = np.arange(S)[:, None]
    onehot = (t // SLC_LEN) % (SLC_TILE // SLC_LEN) == np.arange(HEAD_DIM)[None, :]
    onehot = jnp.broadcast_to(jnp.asarray(onehot.astype(np.float32)).astype(BF16), (B, S, HEAD_DIM))
    off = A_Q + B_Q + A_KV
    return jnp.concatenate([rope[:, :, off:off + HEAD_DIM], onehot], axis=-1)


def _nsa_attention(rope, vals_t, sel, o_cmp, small, tq=256):
    tk = SLC_TILE
    B, S, _ = rope.shape
    n_slc = sel.shape[1]
    keys_aug = _slc_keys_with_block_onehot(rope)
    return pl.pallas_call(
        functools.partial(_nsa_attn_kernel, tq=tq, tk=tk, window=NSA_WINDOW),
        grid=(B, S // tq),
        in_specs=[
            pl.BlockSpec((None, tq, B_Q), lambda b, i: (b, i, A_Q // B_Q)),
            pl.BlockSpec((None, S, 128), lambda b, i: (b, 0, (A_Q + B_Q + A_KV) // 128)),
            pl.BlockSpec((None, S, 2 * HEAD_DIM), lambda b, i: (b, 0, 0)),
            pl.BlockSpec((None, S // KV_BLOCK, 2 * HEAD_DIM, KV_BLOCK), lambda b, i: (b, 0, A_KV // (2 * HEAD_DIM), 0)),
            pl.BlockSpec((None, n_slc, tq), lambda b, i: (b, 0, i)),
            pl.BlockSpec((None, B_Q, tq), lambda b, i: (b, 0, i)),
            pl.BlockSpec((None, tq, W_SMALL), lambda b, i: (b, i, 0)),
        ],
        out_specs=pl.BlockSpec((None, tq, B_Q), lambda b, i: (b, i, 0)),
        out_shape=jax.ShapeDtypeStruct((B, S, B_Q), BF16),
        compiler_params=_cparams("arbitrary", "arbitrary"),
        name="nsa_attention",
    )(rope, rope, keys_aug, vals_t, sel, o_cmp, small)


def _gdn_kernel(xq_ref, xk_ref, xv_ref, hq_ref, hk_ref, hv_ref, z_ref, small_ref, arow_ref,
                cw_ref, alog_row_ref, dtb_row_ref, alog_col_ref, dtb_col_ref, nw_ref,
                o_ref, q_s, k_s, v_s, g_s, b_s, o_s, state_s, u_s, wq_s, qk_s, kdt_s, gl_s, pq_s, pk_s, pv_s, *, tt):
    i = pl.program_id(1)
    C = GDN_CHUNK
    dk = C_HEAD_DIM
    H = C_HEADS

    @pl.when(i == 0)
    def _():
        state_s[...] = jnp.zeros_like(state_s)

    def conv(x_ref, halo_ref, pad_s, col0):
        pad_s[0:8, :] = jnp.where(i > 0, halo_ref[...], 0.0)
        pad_s[8:8 + tt, :] = x_ref[...]
        w = cw_ref[:, col0:col0 + C_QK]
        y = x_ref[...] * w[3:4]
        for kback in range(1, CONV_WIDTH):
            y = y + pad_s[8 - kback:8 - kback + tt, :] * w[3 - kback:4 - kback]
        return _silu(y)

    qf = conv(xq_ref, hq_ref, pq_s, 0)
    kf = conv(xk_ref, hk_ref, pk_s, C_QK)
    v_s[...] = conv(xv_ref, hv_ref, pv_s, 2 * C_QK)
    for h in range(H):
        cols = slice(dk * h, dk * (h + 1))
        qh = qf[:, cols]
        kh = kf[:, cols]
        q_s[:, cols] = qh * lax.rsqrt(jnp.sum(qh * qh, axis=-1, keepdims=True) + NORM_EPS) * (dk ** -0.5)
        k_s[:, cols] = kh * lax.rsqrt(jnp.sum(kh * kh, axis=-1, keepdims=True) + NORM_EPS)
    small = small_ref[...]
    g_s[...] = -jnp.exp(alog_row_ref[...]) * _softplus(small + dtb_row_ref[...])
    b_s[...] = jax.nn.sigmoid(small)

    r_i = _iota((C, C), 0)
    c_i = _iota((C, C), 1)
    lower = r_i >= c_i
    strict = r_i > c_i
    tri_l = lower.astype(F32)
    tri_u = (r_i <= c_i).astype(F32)
    neg_alog_col = -jnp.exp(alog_col_ref[...])
    dtb_col = dtb_col_ref[...]

    HC = H * C
    GC = GDN_GROUP * C
    rr = _iota((GC, GC), 0)
    cc = _iota((GC, GC), 1)
    same_head = (rr >> 6) == (cc >> 6)
    lower_bd = same_head & (rr >= cc)
    strict_bd = same_head & (rr > cc)
    state_bd = (_iota((GC, GDN_GROUP * dk), 0) >> 6) == (_iota((GC, GDN_GROUP * dk), 1) >> 7)

    def problem(c, grp):
        rows = pl.ds(pl.multiple_of(c * C, C), C)
        gc_col = _dot(tri_l, g_s[rows, :], HIGHEST)
        g_row = neg_alog_col * _softplus(arow_ref[c] + dtb_col)
        gc_row = _dot(g_row, tri_u, HIGHEST)
        beta_all = b_s[rows, :]
        g_last = [gc_col[C - 1:C, SMALL_A + h:SMALL_A + h + 1] for h in range(H)]
        if grp == 0:
            gl_s[c, 0:1, :] = jnp.concatenate([jnp.broadcast_to(jnp.exp(g), (1, dk)) for g in g_last], axis=1)
        heads = range(GDN_GROUP * grp, GDN_GROUP * (grp + 1))
        stack = lambda ref: jnp.concatenate([ref[rows, dk * h:dk * (h + 1)] for h in heads], axis=0)
        gcol = jnp.concatenate([gc_col[:, SMALL_A + h:SMALL_A + h + 1] for h in heads], axis=0)
        grow = jnp.concatenate([gc_row[h:h + 1, :] for h in heads], axis=1)
        beta = jnp.concatenate([beta_all[:, SMALL_B + h:SMALL_B + h + 1] for h in heads], axis=0)
        glast = jnp.concatenate([jnp.broadcast_to(g_last[h], (C, 1)) for h in heads], axis=0)
        k = stack(k_s)
        return dict(c=c, grp=grp, q=stack(q_s), k=k, v=stack(v_s), kb=k * beta, beta=beta, eg=jnp.exp(gcol),
                    kd=k * jnp.exp(glast - gcol),
                    decay=jnp.where(lower_bd, jnp.exp(jnp.where(lower_bd, gcol - grow, 0.0)), 0.0))

    def prepare_group(cg, carry):
        ps = [problem(PREP_GROUP * cg + r, grp) for r in range(PREP_GROUP) for grp in range(H // GDN_GROUP)]
        for p in ps:
            p["gram"] = _dot_nt(jnp.concatenate([p["kb"], p["q"]], axis=0).astype(BF16), p["k"].astype(BF16))
        for p in ps:
            p["qk"] = jnp.where(lower_bd, p["gram"][GC:2 * GC] * p["decay"], 0.0)
            p["x"] = -jnp.where(strict_bd, p["gram"][0:GC] * p["decay"], 0.0)
            p["pw16"] = p["x"].astype(BF16)
        for p in ps:
            p["pw"] = _dot(p["pw16"], p["pw16"])
        for _ in range(4):
            for p in ps:
                pw16 = p["pw"].astype(BF16)
                p["x"], p["pw"] = p["x"] + p["pw"] + _dot(pw16, p["x"].astype(BF16)), _dot(pw16, pw16)
        for p in ps:
            p["x"] = p["x"] + p["pw"] + _dot(p["pw"].astype(BF16), p["x"].astype(BF16))
        for p in ps:
            rhs = jnp.concatenate([p["v"] * p["beta"], p["kb"] * p["eg"]], axis=1)
            p["y"] = rhs + _dot(p["x"].astype(BF16), rhs.astype(BF16))
        for p in ps:
            c, grp, y = p["c"], p["grp"], p["y"]
            grows = slice(GC * grp, GC * (grp + 1))
            u_s[c, grows, :] = y[:, 0:dk]
            wq_s[c, grp, 0:GC, :] = y[:, dk:2 * dk].astype(BF16)
            wq_s[c, grp, GC:2 * GC, :] = (p["q"] * p["eg"]).astype(BF16)
            qk_s[c, grows, :] = p["qk"].astype(BF16)
            kdt_s[c, :, grows] = p["kd"].T.astype(BF16)
        return carry

    lax.fori_loop(0, tt // (PREP_GROUP * C), prepare_group, 0)

    def head_blocks(x, row0):
        return jnp.concatenate(
            [x[row0 + C * r:row0 + C * (r + 1), dk * r:dk * (r + 1)] for r in range(GDN_GROUP)], axis=0)

    def scan(c, carry):
        rows = pl.ds(pl.multiple_of(c * C, C), C)
        groups = range(H // GDN_GROUP)
        lanes = [slice(GDN_GROUP * dk * g, GDN_GROUP * dk * (g + 1)) for g in groups]
        grows = [slice(GC * g, GC * (g + 1)) for g in groups]
        st = [state_s[:, lanes[g]] for g in groups]
        d1 = [_dot(wq_s[c, g], st[g].astype(BF16)) for g in groups]
        v16 = [(u_s[c, grows[g], :] - head_blocks(d1[g], 0)).astype(BF16) for g in groups]
        o = [head_blocks(d1[g], GC) + _dot(qk_s[c, grows[g], :], v16[g]) for g in groups]
        for g in groups:
            v_bd = jnp.where(state_bd, jnp.concatenate([v16[g]] * GDN_GROUP, axis=1), jnp.zeros((), BF16))
            state_s[:, lanes[g]] = st[g] * gl_s[c, 0:1, lanes[g]] + _dot(kdt_s[c, :, grows[g]], v_bd)
            for r in range(GDN_GROUP):
                h = GDN_GROUP * g + r
                o_s[rows, dk * h:dk * (h + 1)] = o[g][C * r:C * (r + 1)]
        return carry

    lax.fori_loop(0, tt // C, scan, 0)

    nw = nw_ref[...]
    z = z_ref[...]
    for h in range(H):
        cols = slice(dk * h, dk * (h + 1))
        oh = o_s[:, cols]
        y = oh * lax.rsqrt(jnp.mean(oh * oh, axis=-1, keepdims=True) + NORM_EPS) * nw
        o_ref[:, cols] = (y * _silu(z[:, cols])).astype(BF16)


def _gated_delta_net(gdn, small, a_rows, conv_w, alog_row, dtb_row, alog_col, dtb_col, norm_w, tt=512):
    B, S, _ = gdn.shape
    nch = tt // GDN_CHUNK
    hc = C_HEADS * GDN_CHUNK
    part = lambda j: pl.BlockSpec((None, tt, C_QK), lambda b, i: (b, i, j))
    halo = lambda j: pl.BlockSpec((None, 8, C_QK), lambda b, i: (b, jnp.maximum(i * (tt // 8) - 1, 0), j))
    full = lambda a: pl.BlockSpec(a.shape, lambda b, i: (0,) * a.ndim)
    return pl.pallas_call(
        functools.partial(_gdn_kernel, tt=tt),
        grid=(B, S // tt),
        in_specs=[part(0), part(1), part(2), halo(0), halo(1), halo(2), part(3),
                  pl.BlockSpec((None, tt, W_SMALL), lambda b, i: (b, i, 0)),
                  pl.BlockSpec((None, nch, 8, GDN_CHUNK), lambda b, i: (b, i, 0, 0)),
                  full(conv_w), full(alog_row), full(dtb_row), full(alog_col), full(dtb_col), full(norm_w)],
        out_specs=pl.BlockSpec((None, tt, C_QK), lambda b, i: (b, i, 0)),
        out_shape=jax.ShapeDtypeStruct((B, S, C_QK), BF16),
        scratch_shapes=[pltpu.VMEM((tt, C_QK), F32), pltpu.VMEM((tt, C_QK), F32), pltpu.VMEM((tt, C_QK), F32),
                        pltpu.VMEM((tt, W_SMALL), F32), pltpu.VMEM((tt, W_SMALL), F32),
                        pltpu.VMEM((tt, C_QK), F32), pltpu.VMEM((C_HEAD_DIM, C_QK), F32),
                        pltpu.VMEM((nch, hc, C_HEAD_DIM), F32), pltpu.VMEM((nch, C_HEADS // GDN_GROUP, 2 * GDN_GROUP * GDN_CHUNK, C_HEAD_DIM), BF16),
                        pltpu.VMEM((nch, hc, GDN_GROUP * GDN_CHUNK), BF16), pltpu.VMEM((nch, C_HEAD_DIM, hc), BF16),
                        pltpu.VMEM((nch, 8, C_QK), F32)] + [pltpu.VMEM((tt + 8, C_QK), F32)] * 3,
        compiler_params=_cparams("arbitrary", "arbitrary"),
        name="gated_delta_net",
    )(gdn, gdn, gdn, gdn, gdn, gdn, gdn, small, a_rows, conv_w, alog_row, dtb_row, alog_col, dtb_col, norm_w)


def _outproj_kernel(x_ref, oa_ref, ob_ref, oc_ref, w_ref, gt_ref, o_ref):
    y = _dot(oa_ref[...], w_ref[0:A_Q, :])
    y = y + _dot(ob_ref[...], w_ref[A_Q:A_Q + B_Q, :])
    y = y + _dot(oc_ref[...], w_ref[A_Q + B_Q:, :])
    o_ref[...] = x_ref[...] + gt_ref[...] * y


def _out_projection(x, o_a, o_b, o_c, w, mod, tm=1024):
    B, S, D = x.shape
    row = lambda b, i: (b, i, 0)
    return pl.pallas_call(
        _outproj_kernel,
        grid=(B, S // tm),
        in_specs=[
            pl.BlockSpec((None, tm, D), row),
            pl.BlockSpec((None, tm, A_Q), row),
            pl.BlockSpec((None, tm, B_Q), row),
            pl.BlockSpec((None, tm, C_QK), row),
            pl.BlockSpec((D, D), lambda b, i: (0, 0)),
            pl.BlockSpec((None, 1, D), lambda b, i: (b, 0, 2)),
        ],
        out_specs=pl.BlockSpec((None, tm, D), row),
        out_shape=jax.ShapeDtypeStruct((B, S, D), F32),
        compiler_params=_cparams("arbitrary", "arbitrary"),
        name="out_projection",
    )(x, o_a, o_b, o_c, w, mod)


def _ffn_kernel(x_ref, g_ref, sc_ref, sh_ref, gt_ref, wgu_ref, wd_ref, o_ref, *, tf):
    x = x_ref[...]
    h = _rms_mod(x, g_ref[...], sc_ref[...], sh_ref[...]).astype(BF16)
    acc = None
    for f in range(D_FF // tf):
        gate = _dot(h, wgu_ref[:, tf * f:tf * (f + 1)])
        up = _dot(h, wgu_ref[:, D_FF + tf * f:D_FF + tf * (f + 1)])
        part = _dot((_silu(gate) * up).astype(BF16), wd_ref[tf * f:tf * (f + 1), :])
        acc = part if acc is None else acc + part
    o_ref[...] = x + gt_ref[...] * acc


def _ffn(x, gain, mod, w_gate_up, w_down, tm=512, tf=1408):
    B, S, D = x.shape
    row = lambda b, i: (b, i, 0)
    resident = lambda a: pl.BlockSpec(a.shape, lambda b, i: (0, 0), pipeline_mode=pl.Buffered(1))
    return pl.pallas_call(
        functools.partial(_ffn_kernel, tf=tf),
        grid=(B, S // tm),
        in_specs=[
            pl.BlockSpec((None, tm, D), row),
            pl.BlockSpec((1, D), lambda b, i: (0, 0)),
            pl.BlockSpec((None, 1, D), lambda b, i: (b, 0, 4)),
            pl.BlockSpec((None, 1, D), lambda b, i: (b, 0, 3)),
            pl.BlockSpec((None, 1, D), lambda b, i: (b, 0, 5)),
            resident(w_gate_up),
            resident(w_down),
        ],
        out_specs=pl.BlockSpec((None, tm, D), row),
        out_shape=jax.ShapeDtypeStruct((B, S, D), F32),
        compiler_params=_cparams("arbitrary", "arbitrary"),
        name="swiglu_ffn",
    )(x, gain, mod, mod, mod, w_gate_up, w_down)


def _final_norm_kernel(x_ref, g_ref, o_ref):
    x = x_ref[...]
    o_ref[...] = x * lax.rsqrt(jnp.mean(x * x, axis=-1, keepdims=True) + NORM_EPS) * g_ref[...]


def _final_norm(x, gain, tm=512):
    B, S, D = x.shape
    return pl.pallas_call(
        _final_norm_kernel,
        grid=(B, S // tm),
        in_specs=[pl.BlockSpec((None, tm, D), lambda b, i: (b, i, 0)), pl.BlockSpec((1, D), lambda b, i: (0, 0))],
        out_specs=pl.BlockSpec((None, tm, D), lambda b, i: (b, i, 0)),
        out_shape=jax.ShapeDtypeStruct((B, S, D), F32),
        compiler_params=_cparams("arbitrary", "arbitrary"),
        name="final_norm",
    )(x, gain)


def _rope_tables(seq):
    inv = 1.0 / (ROPE_THETA ** (jnp.arange(0, HEAD_DIM, 2, dtype=F32) / HEAD_DIM))
    ang = jnp.arange(seq, dtype=F32)[:, None] * inv[None, :]
    cos, sin = jnp.cos(ang), jnp.sin(ang)
    zero = jnp.zeros_like(sin)
    cos_t = jnp.tile(cos, (1, 4))
    sa_t = jnp.tile(jnp.concatenate([-sin, zero], axis=1), (1, 2))
    sb_t = jnp.tile(jnp.concatenate([zero, sin], axis=1), (1, 2))
    return cos_t, sa_t, sb_t


def _permute_w_in(w_in):
    pts = np.cumsum(IN_SPLITS)[:-1].tolist()
    aq, ak, av, bq, bkc, bvc, bks, bvs, bkw, bvw, bg, cqkv, cz, ca, cb = jnp.split(w_in, pts, axis=-1)
    pad = jnp.zeros(w_in.shape[:-1] + (W_SMALL - B_GATES - 2 * C_HEADS,), w_in.dtype)
    return jnp.concatenate([aq, bq, ak, bks, bkw, av, bvs, bvw, bkc, bvc, bg, ca, cb, pad, cqkv, cz], axis=-1).astype(BF16)


def _compress_weights(k_w1, k_w2, v_w1, v_w2, pe_k, pe_v):
    L = k_w1.shape[0]
    half = CMP_STRIDE * HEAD_DIM

    def first_layer(lo):
        wk = k_w1[:, lo:lo + half].reshape(L, CMP_STRIDE, HEAD_DIM, CMP_HIDDEN)
        wv = v_w1[:, lo:lo + half].reshape(L, CMP_STRIDE, HEAD_DIM, CMP_HIDDEN)
        z = jnp.zeros_like(wk)
        w = jnp.concatenate([jnp.concatenate([wk, z], axis=-1), jnp.concatenate([z, wv], axis=-1)], axis=2)
        return w.reshape(L, 2 * half, 2 * CMP_HIDDEN).astype(BF16)

    def pe_rows(lo):
        return jnp.concatenate([pe_k[:, lo:lo + CMP_STRIDE], pe_v[:, lo:lo + CMP_STRIDE]], axis=-1).reshape(L, 1, 2 * half)

    z2 = jnp.zeros_like(k_w2)
    w2 = jnp.concatenate([jnp.concatenate([k_w2, z2], axis=-1), jnp.concatenate([z2, v_w2], axis=-1)], axis=1).astype(BF16)
    return pe_rows(0), pe_rows(CMP_STRIDE), first_layer(0), first_layer(half), w2


def _pool_matrix(n_cmp_rows, n_slc):
    ratio = SLC_LEN // CMP_STRIDE
    n = np.arange(n_cmp_rows)[:, None]
    j = np.arange(n_slc)[None, :]
    return jnp.asarray(((n // ratio == j) | (n == ratio * j - 1)).astype(np.float32))


def _lane_row(v, lane0):
    L, n = v.shape
    return jnp.zeros((L, 1, 128), F32).at[:, 0, lane0:lane0 + n].set(v)


def _sublane_col(v):
    L, n = v.shape
    return jnp.zeros((L, 8, 1), F32).at[:, :n, 0].set(v)


def kernel(x, c, norm_mix, norm_ffn, ada_w, ada_b, w_in, attn_sinks, cmp_k_w1, cmp_k_w2, cmp_v_w1, cmp_v_w2, cmp_pe_k, cmp_pe_v, gdn_conv_w, gdn_A_log, gdn_dt_bias, gdn_norm, w_out, w_gate_up, w_down, final_norm):
    B, S, D = x.shape
    L = w_in.shape[0]
    n16 = S // CMP_STRIDE
    n_slc = S // SLC_LEN

    cos_t, sa_t, sb_t = _rope_tables(S)
    mod = _adaln_mod(c, ada_w, ada_b)
    pea, peb, w1a, w1b, w2c = _compress_weights(cmp_k_w1, cmp_k_w2, cmp_v_w1, cmp_v_w2, cmp_pe_k, cmp_pe_v)
    pool_t = _pool_matrix(n16, n_slc).T
    per_layer = dict(
        mod=mod,
        norm_mix=norm_mix.reshape(L, 1, D), norm_ffn=norm_ffn.reshape(L, 1, D),
        w_in=_permute_w_in(w_in), sinks=attn_sinks,
        pea=pea, peb=peb, w1a=w1a, w1b=w1b, w2c=w2c,
        conv_w=gdn_conv_w,
        alog_row=_lane_row(gdn_A_log, SMALL_A), dtb_row=_lane_row(gdn_dt_bias, SMALL_A),
        alog_col=_sublane_col(gdn_A_log), dtb_col=_sublane_col(gdn_dt_bias),
        gdn_norm=gdn_norm.reshape(L, 1, C_HEAD_DIM),
        w_out=w_out.astype(BF16), w_gate_up=w_gate_up.astype(BF16), w_down=w_down.astype(BF16),
    )

    def layer(xc, p):
        rope, bq, vals, cmp_in, small, gdn = _in_projection(xc, p["norm_mix"], p["mod"], p["w_in"], cos_t, sa_t, sb_t)
        vals_all_t = vals.reshape(B, S // KV_BLOCK, KV_BLOCK, W_V).transpose(0, 1, 3, 2)
        o_a = _swa_attention(rope, vals_all_t, p["sinks"])
        kv_cmp, kv_cmp_t = _compress(cmp_in.reshape(B, n16, CMP_STRIDE * W_CMP), p["pea"], p["peb"], p["w1a"], p["w1b"], p["w2c"])
        o_cmp_t, sel = _cmp_select(bq, kv_cmp, kv_cmp_t, pool_t)
        o_b = _nsa_attention(rope, vals_all_t, sel, o_cmp_t, small)
        a_rows = small[:, :, SMALL_A:SMALL_A + 8].reshape(B, S // GDN_CHUNK, GDN_CHUNK, 8).transpose(0, 1, 3, 2)
        o_c = _gated_delta_net(gdn, small, a_rows, p["conv_w"], p["alog_row"], p["dtb_row"], p["alog_col"], p["dtb_col"], p["gdn_norm"])
        xc = _out_projection(xc, o_a, o_b, o_c, p["w_out"], p["mod"])
        xc = _ffn(xc, p["norm_ffn"], p["mod"], p["w_gate_up"], p["w_down"])
        return xc, None

    x, _ = lax.scan(layer, x, per_layer)
    return _final_norm(x, final_norm.reshape(1, D))
```

```python
import functools
import math

import numpy as np
import jax
import jax.numpy as jnp
from jax import lax
from jax.experimental import pallas as pl
from jax.experimental.pallas import tpu as pltpu

F32 = jnp.float32
BF16 = jnp.bfloat16
HIGHEST = lax.Precision.HIGHEST

D_MODEL = 1024
DEPTH = 4
HEAD_DIM = 64
ATTN_SCALE = HEAD_DIM ** -0.5
ROPE_THETA = 10000.0
NEG_INF = -1e30
NORM_EPS = 1e-6

A_HEADS = 4
A_KV_HEADS = 2
A_WINDOW = 128

B_HEADS = 4
CMP_STRIDE = 16
CMP_LEN = 32
CMP_HIDDEN = 256
SLC_LEN = 64
SLC_TOPK = 16
NSA_WINDOW = 512
SLC_FORCED_SCORE = 1e9

C_HEAD_DIM = 128
C_HEADS = 4
CONV_WIDTH = 4
GDN_CHUNK = 64
C_QK = C_HEADS * C_HEAD_DIM

D_FF = 2816

A_Q = A_HEADS * HEAD_DIM
A_KV = A_KV_HEADS * HEAD_DIM
B_Q = B_HEADS * HEAD_DIM
B_KV = HEAD_DIM
B_GATES = 3 * B_HEADS
IN_SPLITS = (A_Q, A_KV, A_KV, B_Q, B_KV, B_KV, B_KV, B_KV, B_KV, B_KV, B_GATES, 3 * C_QK, C_QK, C_HEADS, C_HEADS)

W_ROPE = 768
W_V = 256
W_CMP = 128
W_SMALL = 128
W_GDN = 2048
W_TOTAL = W_ROPE + W_V + W_CMP + W_SMALL + W_GDN
SMALL_A = B_GATES
SMALL_B = B_GATES + C_HEADS

VMEM_LIMIT = 56 * 1024 * 1024
PREP_GROUP = 8
GDN_GROUP = 2

NT_DIMS = (((1,), (1,)), ((), ()))


def _cparams(*sem):
    return pltpu.CompilerParams(dimension_semantics=sem, vmem_limit_bytes=VMEM_LIMIT)


def _iota(shape, dim):
    return lax.broadcasted_iota(jnp.int32, shape, dim)


def _dot(a, b, precision=None):
    return jnp.dot(a, b, preferred_element_type=F32, precision=precision)


def _dot_nt(a, b):
    return lax.dot_general(a, b, NT_DIMS, preferred_element_type=F32)


def _silu(x):
    return x * jax.nn.sigmoid(x)


def _softplus(x):
    return jnp.maximum(x, 0.0) + jnp.log(1.0 + jnp.exp(-jnp.abs(x)))


def _rms_mod(x, gain, sc, sh):
    y = x * lax.rsqrt(jnp.mean(x * x, axis=-1, keepdims=True) + NORM_EPS)
    return (y * gain) * (1.0 + sc) + sh


def _mod_kernel(c_ref, w_ref, b_ref, o_ref):
    o_ref[...] = _dot(_silu(c_ref[...]), w_ref[...], HIGHEST) + b_ref[...]


def _adaln_mod(c, ada_w, ada_b):
    L, D, N = ada_w.shape
    tn = 1536
    c8 = jnp.zeros((8, D), F32).at[: c.shape[0]].set(c)
    out = pl.pallas_call(
        _mod_kernel,
        grid=(L, N // tn),
        in_specs=[
            pl.BlockSpec((8, D), lambda l, j: (0, 0)),
            pl.BlockSpec((None, D, tn), lambda l, j: (l, 0, j)),
            pl.BlockSpec((None, 1, tn), lambda l, j: (l, 0, j)),
        ],
        out_specs=pl.BlockSpec((None, 8, tn), lambda l, j: (l, 0, j)),
        out_shape=jax.ShapeDtypeStruct((L, 8, N), F32),
        compiler_params=_cparams("arbitrary", "arbitrary"),
        name="adaln_mod",
    )(c8, ada_w, ada_b.reshape(L, 1, N))
    return out.reshape(L, 8, 1, N)


def _inproj_kernel(x_ref, g_ref, sc_ref, sh_ref, w_ref, cos_ref, sa_ref, sb_ref,
                   rope_ref, bq_ref, v_ref, cmp_ref, small_ref, gdn_ref):
    h = _rms_mod(x_ref[...], g_ref[...], sc_ref[...], sh_ref[...]).astype(BF16)
    yr = _dot(h, w_ref[:, 0:W_ROPE])
    bq_ref[...] = yr[:, A_Q:A_Q + B_Q].astype(BF16)
    c, sa, sb = cos_ref[...], sa_ref[...], sb_ref[...]
    for g in range(W_ROPE // 128):
        xg = yr[:, 128 * g:128 * (g + 1)]
        rot = xg * c + pltpu.roll(xg, 96, 1) * sa + pltpu.roll(xg, 32, 1) * sb
        rope_ref[:, 128 * g:128 * (g + 1)] = rot.astype(BF16)
    o = W_ROPE
    yp = _dot(h, w_ref[:, o:o + W_V + W_CMP + W_SMALL])
    v_ref[...] = yp[:, 0:W_V].astype(BF16)
    cmp_ref[...] = yp[:, W_V:W_V + W_CMP]
    small_ref[...] = yp[:, W_V + W_CMP:W_V + W_CMP + W_SMALL]
    o += W_V + W_CMP + W_SMALL
    gdn_ref[...] = _dot(h, w_ref[:, o:o + W_GDN])


def _in_projection(x, gain, mod, w, cos_t, sa_t, sb_t, tm=512):
    B, S, D = x.shape
    row = lambda b, i: (b, i, 0)
    outs = pl.pallas_call(
        _inproj_kernel,
        grid=(B, S // tm),
        in_specs=[
            pl.BlockSpec((None, tm, D), row),
            pl.BlockSpec((1, D), lambda b, i: (0, 0)),
            pl.BlockSpec((None, 1, D), lambda b, i: (b, 0, 1)),
            pl.BlockSpec((None, 1, D), lambda b, i: (b, 0, 0)),
            pl.BlockSpec((D, W_TOTAL), lambda b, i: (0, 0)),
            pl.BlockSpec((tm, 128), lambda b, i: (i, 0)),
            pl.BlockSpec((tm, 128), lambda b, i: (i, 0)),
            pl.BlockSpec((tm, 128), lambda b, i: (i, 0)),
        ],
        out_specs=[
            pl.BlockSpec((None, tm, W_ROPE), row),
            pl.BlockSpec((None, tm, B_Q), row),
            pl.BlockSpec((None, tm, W_V), row),
            pl.BlockSpec((None, tm, W_CMP), row),
            pl.BlockSpec((None, tm, W_SMALL), row),
            pl.BlockSpec((None, tm, W_GDN), row),
        ],
        out_shape=[
            jax.ShapeDtypeStruct((B, S, W_ROPE), BF16),
            jax.ShapeDtypeStruct((B, S, B_Q), BF16),
            jax.ShapeDtypeStruct((B, S, W_V), BF16),
            jax.ShapeDtypeStruct((B, S, W_CMP), F32),
            jax.ShapeDtypeStruct((B, S, W_SMALL), F32),
            jax.ShapeDtypeStruct((B, S, W_GDN), F32),
        ],
        compiler_params=_cparams("arbitrary", "arbitrary"),
        name="in_projection",
    )(x, gain, mod, mod, w, cos_t, sa_t, sb_t)
    return outs


def _swa_kernel(sink_ref, q_ref, k_ref, vt_ref, o_ref, *, tq, window):
    q0 = pl.program_id(1) * tq
    span = tq + window
    start = pl.multiple_of(jnp.maximum(q0 - window, 0), KV_BLOCK)
    qt = q_ref[...].astype(F32).T * ATTN_SCALE
    kpos = start + _iota((span, tq), 0)
    qpos = q0 + _iota((span, tq), 1)
    bias = jnp.where((kpos <= qpos) & (kpos > qpos - window), 0.0, NEG_INF)
    vb = vt_ref[pl.ds(start // KV_BLOCK, span // KV_BLOCK)]
    group = A_HEADS // A_KV_HEADS
    first_head = _iota((1, group * tq), 1) < tq
    bias = jnp.concatenate([bias] * group, axis=1)
    groups = range(A_KV_HEADS)
    qg = [jnp.concatenate([qt[HEAD_DIM * h:HEAD_DIM * (h + 1)] for h in range(group * g, group * (g + 1))],
                          axis=1).astype(BF16) for g in groups]
    s = [_dot(k_ref[pl.ds(start, span), HEAD_DIM * g:HEAD_DIM * (g + 1)], qg[g]) + bias for g in groups]
    sink = [jnp.where(first_head, sink_ref[group * g], sink_ref[group * g + 1]) for g in groups]
    m = [jnp.maximum(jnp.max(s[g], axis=0, keepdims=True), sink[g]) for g in groups]
    p = [jnp.exp(s[g] - m[g]) for g in groups]
    den = [jnp.sum(p[g], axis=0, keepdims=True) + jnp.exp(sink[g] - m[g]) for g in groups]
    vt = [jnp.concatenate([vb[r, HEAD_DIM * g:HEAD_DIM * (g + 1), :] for r in range(span // KV_BLOCK)], axis=1)
          for g in groups]
    o_t = [_dot(vt[g], p[g].astype(BF16)) * (1.0 / den[g]) for g in groups]
    outs = [o_t[g][:, tq * r:tq * (r + 1)] for g in groups for r in range(group)]
    o_ref[...] = jnp.concatenate(outs, axis=0).T.astype(BF16)


def _swa_attention(rope, vals_t, sinks, tq=256):
    assert A_HEADS // A_KV_HEADS == 2
    B, S, _ = rope.shape
    return pl.pallas_call(
        functools.partial(_swa_kernel, tq=tq, window=A_WINDOW),
        grid=(B, S // tq),
        in_specs=[
            pl.BlockSpec(memory_space=pltpu.SMEM),
            pl.BlockSpec((None, tq, A_Q), lambda b, i: (b, i, 0)),
            pl.BlockSpec((None, S, A_KV), lambda b, i: (b, 0, (A_Q + B_Q) // A_KV)),
            pl.BlockSpec((None, S // KV_BLOCK, A_KV, KV_BLOCK), lambda b, i: (b, 0, 0, 0)),
        ],
        out_specs=pl.BlockSpec((None, tq, A_Q), lambda b, i: (b, i, 0)),
        out_shape=jax.ShapeDtypeStruct((B, S, A_Q), BF16),
        compiler_params=_cparams("arbitrary", "arbitrary"),
        name="swa_attention",
    )(sinks, rope, rope, vals_t)


def _compress_kernel(x_ref, pea_ref, peb_ref, w1a_ref, w1b_ref, w2_ref, o_ref, ot_ref):
    x = x_ref[...]
    n = x.shape[0]
    a = _dot((x + pea_ref[...]).astype(BF16), w1a_ref[...])
    b = _dot((x + peb_ref[...]).astype(BF16), w1b_ref[...])
    hid = _silu(a + pltpu.roll(b, n - 1, 0))
    out = _dot(hid.astype(BF16), w2_ref[...])
    o_ref[...] = out.astype(BF16)
    ot_ref[...] = out.T.astype(BF16)


def _compress(xc, pea, peb, w1a, w1b, w2):
    B, n, K = xc.shape
    full = lambda a: pl.BlockSpec(a.shape, lambda b: (0,) * a.ndim)
    return pl.pallas_call(
        _compress_kernel,
        grid=(B,),
        in_specs=[pl.BlockSpec((None, n, K), lambda b: (b, 0, 0)), full(pea), full(peb), full(w1a), full(w1b), full(w2)],
        out_specs=[pl.BlockSpec((None, n, 2 * HEAD_DIM), lambda b: (b, 0, 0)),
                   pl.BlockSpec((None, 2 * HEAD_DIM, n), lambda b: (b, 0, 0))],
        out_shape=[jax.ShapeDtypeStruct((B, n, 2 * HEAD_DIM), BF16), jax.ShapeDtypeStruct((B, 2 * HEAD_DIM, n), BF16)],
        compiler_params=_cparams("arbitrary"),
        name="nsa_compress",
    )(xc, pea, peb, w1a, w1b, w2)


def _heads_on_lanes(q):
    qt = q.astype(F32).T * ATTN_SCALE
    return jnp.concatenate([qt[HEAD_DIM * h:HEAD_DIM * (h + 1)] for h in range(B_HEADS)], axis=1).astype(BF16)


def _heads_on_sublanes(ot, tq):
    return jnp.concatenate([ot[:, tq * h:tq * (h + 1)] for h in range(B_HEADS)], axis=0)


def _cmp_select_kernel(q_ref, k_ref, vt_ref, pool_ref, o_ref, sel_ref, *, tq, top_k):
    q0 = pl.program_id(1) * tq
    H = B_HEADS
    n_cmp = k_ref.shape[0]
    n_slc = pool_ref.shape[0]
    qt = _heads_on_lanes(q_ref[...])
    valid = _iota((n_cmp, tq), 0) * CMP_STRIDE + (CMP_LEN - 1) <= q0 + _iota((n_cmp, tq), 1)
    valid = jnp.concatenate([valid] * (H // 2), axis=1)
    halves = [slice(0, H // 2 * tq), slice(H // 2 * tq, H * tq)]
    kc = k_ref[:, 0:HEAD_DIM]
    vc_t = vt_ref[HEAD_DIM:2 * HEAD_DIM, :]
    s = [jnp.where(valid, _dot(kc, qt[:, hs]), NEG_INF) for hs in halves]
    e = [jnp.where(valid, jnp.exp(si - jnp.max(si, axis=0, keepdims=True)), 0.0) for si in s]
    den = [jnp.sum(ei, axis=0, keepdims=True) for ei in e]
    p = [ei * (1.0 / jnp.where(di > 0.0, di, 1.0)) for ei, di in zip(e, den)]
    o_t = jnp.concatenate([_dot(vc_t, pi.astype(BF16)) for pi in p], axis=1)
    o_ref[...] = _heads_on_sublanes(o_t, tq)
    psum = None
    for pi in p:
        for r in range(H // 2):
            part = pi[:, tq * r:tq * (r + 1)]
            psum = part if psum is None else psum + part

    imp = _dot(pool_ref[...], psum, HIGHEST)
    blk = _iota((n_slc, tq), 0)
    tt = q0 + _iota((n_slc, tq), 1)
    cur = tt >> 6
    forced = (blk == 0) | (blk == cur) | (blk == cur - 1)
    causal = blk * SLC_LEN <= tt
    score = jnp.where(forced, SLC_FORCED_SCORE, jnp.where(causal, imp, NEG_INF))
    sel = jnp.zeros((n_slc, tq), F32)
    blk_f = blk.astype(F32)
    for _ in range(top_k):
        mx = jnp.max(score, axis=0, keepdims=True)
        first = jnp.min(jnp.where(score == mx, blk_f, float(n_slc)), axis=0, keepdims=True)
        pick = blk_f == first
        sel = jnp.where(pick, 1.0, sel)
        score = jnp.where(pick, -jnp.inf, score)
    sel_ref[...] = jnp.where(causal, sel, 0.0)


def _cmp_select(bq, kv_cmp, kv_cmp_t, pool_t, tq=256):
    B, S, _ = bq.shape
    n_slc, n_cmp = pool_t.shape
    top_k = min(SLC_TOPK, n_slc)
    return pl.pallas_call(
        functools.partial(_cmp_select_kernel, tq=tq, top_k=top_k),
        grid=(B, S // tq),
        in_specs=[
            pl.BlockSpec((None, tq, B_Q), lambda b, i: (b, i, 0)),
            pl.BlockSpec((None, n_cmp, 2 * HEAD_DIM), lambda b, i: (b, 0, 0)),
            pl.BlockSpec((None, 2 * HEAD_DIM, n_cmp), lambda b, i: (b, 0, 0)),
            pl.BlockSpec((n_slc, n_cmp), lambda b, i: (0, 0)),
        ],
        out_specs=[
            pl.BlockSpec((None, B_Q, tq), lambda b, i: (b, 0, i)),
            pl.BlockSpec((None, n_slc, tq), lambda b, i: (b, 0, i)),
        ],
        out_shape=[jax.ShapeDtypeStruct((B, B_Q, S), F32), jax.ShapeDtypeStruct((B, n_slc, S), F32)],
        compiler_params=_cparams("arbitrary", "arbitrary"),
        name="nsa_cmp_select",
    )(bq, kv_cmp, kv_cmp_t, pool_t)


KV_BLOCK = 128
PAD_ROWS = 16
SLC_TILE = 512


def _values_with_ones(vb, row0, n):
    vt = jnp.concatenate([vb[r, row0:row0 + HEAD_DIM, :] for r in range(n)], axis=1)
    return jnp.concatenate([vt, jnp.ones((PAD_ROWS, n * KV_BLOCK), BF16)], axis=0)


def _nsa_attn_kernel(q_ref, k_ref, ka_ref, vt_ref, sel_ref, ocmp_ref, small_ref, o_ref, *, tq, tk, window):
    i = pl.program_id(1)
    q0 = i * tq
    H = B_HEADS
    d = HEAD_DIM
    qt = _heads_on_lanes(q_ref[...])
    bpt = tk // SLC_LEN
    vpt = tk // KV_BLOCK
    q_pad = jnp.zeros((ka_ref.shape[1] - d - PAD_ROWS, H * tq), BF16)
    pen_pad = jnp.zeros((PAD_ROWS - bpt, tq), F32)

    def scores(j):
        sel = sel_ref[pl.ds(pl.multiple_of(j * bpt, bpt), bpt), :]
        pen = jnp.concatenate([jnp.where(sel > 0.5, 0.0, NEG_INF), pen_pad], axis=0)
        q_aug = jnp.concatenate([qt, jnp.concatenate([pen] * H, axis=1).astype(BF16), q_pad], axis=0)
        return _dot(ka_ref[pl.ds(pl.multiple_of(j * tk, tk), tk), :], q_aug)

    def update(j, s, carry):
        m, acc = carry
        m_new = jnp.maximum(m, jnp.max(s, axis=0, keepdims=True))
        alpha = jnp.exp(m - m_new)
        p = jnp.exp(s - m_new)
        vb = vt_ref[pl.ds(pl.multiple_of(j * vpt, vpt), vpt)]
        return m_new, acc * alpha + _dot(_values_with_ones(vb, 0, vpt), p.astype(BF16))

    def full_tiles(n):
        def body(jg, carry):
            s = [scores(n * jg + r) for r in range(n)]
            for r in range(n):
                carry = update(n * jg + r, s[r], carry)
            return carry
        return body

    qpos = q0 + _iota((tk, tq), 1)

    def causal_tile(j, carry):
        causal = jnp.where(j * tk + _iota((tk, tq), 0) <= qpos, 0.0, NEG_INF)
        return update(j, scores(j) + jnp.concatenate([causal] * H, axis=1), carry)

    n_full_pairs = q0 // (2 * tk)
    n_tiles = (q0 + tq + tk - 1) // tk
    carry = (jnp.full((1, H * tq), NEG_INF, F32), jnp.zeros((d + PAD_ROWS, H * tq), F32))
    carry = lax.fori_loop(0, n_full_pairs, full_tiles(2), carry)
    _, acc = lax.fori_loop(2 * n_full_pairs, n_tiles, causal_tile, carry)
    o_slc = _heads_on_sublanes(acc[0:d] * (1.0 / acc[d:d + 1]), tq)

    span = tq + window
    start = pl.multiple_of(jnp.maximum(q0 - window, 0), KV_BLOCK)
    kw = k_ref[pl.ds(start, span), d:2 * d]
    kpos = start + _iota((span, tq), 0)
    qp = q0 + _iota((span, tq), 1)
    bias = jnp.where((kpos <= qp) & (kpos > qp - window), 0.0, NEG_INF)
    bias = jnp.concatenate([bias] * (H // 2), axis=1)
    vw = _values_with_ones(vt_ref[pl.ds(start // KV_BLOCK, span // KV_BLOCK)], d, span // KV_BLOCK)
    halves = [slice(0, H // 2 * tq), slice(H // 2 * tq, H * tq)]
    s = [_dot(kw, qt[:, hs]) + bias for hs in halves]
    p = [jnp.exp(si - jnp.max(si, axis=0, keepdims=True)) for si in s]
    res = jnp.concatenate([_dot(vw, pi.astype(BF16)) for pi in p], axis=1)
    o_win = _heads_on_sublanes(res[0:d] * (1.0 / res[d:d + 1]), tq)

    gates = jax.nn.sigmoid(small_ref[...]).T
    o_cmp = ocmp_ref[...]
    outs = []
    for h in range(H):
        rows = slice(HEAD_DIM * h, HEAD_DIM * (h + 1))
        outs.append(gates[h:h + 1, :] * o_cmp[rows] + gates[H + h:H + h + 1, :] * o_slc[rows]
                    + gates[2 * H + h:2 * H + h + 1, :] * o_win[rows])
    o_ref[...] = jnp.concatenate(outs, axis=0).T.astype(BF16)


def _slc_keys_with_block_onehot(rope):
    B, S, _ = rope.shape
    t = np.arange(S)[:, None]
    onehot = (t // SLC_LEN) % (SLC_TILE // SLC_LEN) == np.arange(HEAD_DIM)[None, :]
    onehot = jnp.broadcast_to(jnp.asarray(onehot.astype(np.float32)).astype(BF16), (B, S, HEAD_DIM))
    off = A_Q + B_Q + A_KV
    return jnp.concatenate([rope[:, :, off:off + HEAD_DIM], onehot], axis=-1)


def _nsa_attention(rope, vals_t, sel, o_cmp, small, tq=256):
    tk = SLC_TILE
    B, S, _ = rope.shape
    n_slc = sel.shape[1]
    keys_aug = _slc_keys_with_block_onehot(rope)
    return pl.pallas_call(
        functools.partial(_nsa_attn_kernel, tq=tq, tk=tk, window=NSA_WINDOW),
        grid=(B, S // tq),
        in_specs=[
            pl.BlockSpec((None, tq, B_Q), lambda b, i: (b, i, A_Q // B_Q)),
            pl.BlockSpec((None, S, 128), lambda b, i: (b, 0, (A_Q + B_Q + A_KV) // 128)),
            pl.BlockSpec((None, S, 2 * HEAD_DIM), lambda b, i: (b, 0, 0)),
            pl.BlockSpec((None, S // KV_BLOCK, 2 * HEAD_DIM, KV_BLOCK), lambda b, i: (b, 0, A_KV // (2 * HEAD_DIM), 0)),
            pl.BlockSpec((None, n_slc, tq), lambda b, i: (b, 0, i)),
            pl.BlockSpec((None, B_Q, tq), lambda b, i: (b, 0, i)),
            pl.BlockSpec((None, tq, W_SMALL), lambda b, i: (b, i, 0)),
        ],
        out_specs=pl.BlockSpec((None, tq, B_Q), lambda b, i: (b, i, 0)),
        out_shape=jax.ShapeDtypeStruct((B, S, B_Q), BF16),
        compiler_params=_cparams("arbitrary", "arbitrary"),
        name="nsa_attention",
    )(rope, rope, keys_aug, vals_t, sel, o_cmp, small)


def _gdn_kernel(xq_ref, xk_ref, xv_ref, hq_ref, hk_ref, hv_ref, z_ref, small_ref, arow_ref,
                cw_ref, alog_row_ref, dtb_row_ref, alog_col_ref, dtb_col_ref, nw_ref,
                o_ref, q_s, k_s, v_s, g_s, b_s, o_s, state_s, u_s, wq_s, qk_s, kdt_s, gl_s, pq_s, pk_s, pv_s, *, tt):
    i = pl.program_id(1)
    C = GDN_CHUNK
    dk = C_HEAD_DIM
    H = C_HEADS

    @pl.when(i == 0)
    def _():
        state_s[...] = jnp.zeros_like(state_s)

    def conv(x_ref, halo_ref, pad_s, col0):
        pad_s[0:8, :] = jnp.where(i > 0, halo_ref[...], 0.0)
        pad_s[8:8 + tt, :] = x_ref[...]
        w = cw_ref[:, col0:col0 + C_QK]
        y = x_ref[...] * w[3:4]
        for kback in range(1, CONV_WIDTH):
            y = y + pad_s[8 - kback:8 - kback + tt, :] * w[3 - kback:4 - kback]
        return _silu(y)

    qf = conv(xq_ref, hq_ref, pq_s, 0)
    kf = conv(xk_ref, hk_ref, pk_s, C_QK)
    v_s[...] = conv(xv_ref, hv_ref, pv_s, 2 * C_QK)
    for h in range(H):
        cols = slice(dk * h, dk * (h + 1))
        qh = qf[:, cols]
        kh = kf[:, cols]
        q_s[:, cols] = qh * lax.rsqrt(jnp.sum(qh * qh, axis=-1, keepdims=True) + NORM_EPS) * (dk ** -0.5)
        k_s[:, cols] = kh * lax.rsqrt(jnp.sum(kh * kh, axis=-1, keepdims=True) + NORM_EPS)
    small = small_ref[...]
    g_s[...] = -jnp.exp(alog_row_ref[...]) * _softplus(small + dtb_row_ref[...])
    b_s[...] = jax.nn.sigmoid(small)

    r_i = _iota((C, C), 0)
    c_i = _iota((C, C), 1)
    lower = r_i >= c_i
    strict = r_i > c_i
    tri_l = lower.astype(F32)
    tri_u = (r_i <= c_i).astype(F32)
    neg_alog_col = -jnp.exp(alog_col_ref[...])
    dtb_col = dtb_col_ref[...]

    HC = H * C
    GC = GDN_GROUP * C
    rr = _iota((GC, GC), 0)
    cc = _iota((GC, GC), 1)
    same_head = (rr >> 6) == (cc >> 6)
    lower_bd = same_head & (rr >= cc)
    strict_bd = same_head & (rr > cc)
    state_bd = (_iota((GC, GDN_GROUP * dk), 0) >> 6) == (_iota((GC, GDN_GROUP * dk), 1) >> 7)

    def problem(c, grp):
        rows = pl.ds(pl.multiple_of(c * C, C), C)
        gc_col = _dot(tri_l, g_s[rows, :], HIGHEST)
        g_row = neg_alog_col * _softplus(arow_ref[c] + dtb_col)
        gc_row = _dot(g_row, tri_u, HIGHEST)
        beta_all = b_s[rows, :]
        g_last = [gc_col[C - 1:C, SMALL_A + h:SMALL_A + h + 1] for h in range(H)]
        if grp == 0:
            gl_s[c, 0:1, :] = jnp.concatenate([jnp.broadcast_to(jnp.exp(g), (1, dk)) for g in g_last], axis=1)
        heads = range(GDN_GROUP * grp, GDN_GROUP * (grp + 1))
        stack = lambda ref: jnp.concatenate([ref[rows, dk * h:dk * (h + 1)] for h in heads], axis=0)
        gcol = jnp.concatenate([gc_col[:, SMALL_A + h:SMALL_A + h + 1] for h in heads], axis=0)
        grow = jnp.concatenate([gc_row[h:h + 1, :] for h in heads], axis=1)
        beta = jnp.concatenate([beta_all[:, SMALL_B + h:SMALL_B + h + 1] for h in heads], axis=0)
        glast = jnp.concatenate([jnp.broadcast_to(g_last[h], (C, 1)) for h in heads], axis=0)
        k = stack(k_s)
        return dict(c=c, grp=grp, q=stack(q_s), k=k, v=stack(v_s), kb=k * beta, beta=beta, eg=jnp.exp(gcol),
                    kd=k * jnp.exp(glast - gcol),
                    decay=jnp.where(lower_bd, jnp.exp(jnp.where(lower_bd, gcol - grow, 0.0)), 0.0))

    def prepare_group(cg, carry):
        ps = [problem(PREP_GROUP * cg + r, grp) for r in range(PREP_GROUP) for grp in range(H // GDN_GROUP)]
        for p in ps:
            p["gram"] = _dot_nt(jnp.concatenate([p["kb"], p["q"]], axis=0).astype(BF16), p["k"].astype(BF16))
        for p in ps:
            p["qk"] = jnp.where(lower_bd, p["gram"][GC:2 * GC] * p["decay"], 0.0)
            p["x"] = -jnp.where(strict_bd, p["gram"][0:GC] * p["decay"], 0.0)
            p["pw16"] = p["x"].astype(BF16)
        for p in ps:
            p["pw"] = _dot(p["pw16"], p["pw16"])
        for _ in range(4):
            for p in ps:
                pw16 = p["pw"].astype(BF16)
                p["x"], p["pw"] = p["x"] + p["pw"] + _dot(pw16, p["x"].astype(BF16)), _dot(pw16, pw16)
        for p in ps:
            p["x"] = p["x"] + p["pw"] + _dot(p["pw"].astype(BF16), p["x"].astype(BF16))
        for p in ps:
            rhs = jnp.concatenate([p["v"] * p["beta"], p["kb"] * p["eg"]], axis=1)
            p["y"] = rhs + _dot(p["x"].astype(BF16), rhs.astype(BF16))
        for p in ps:
            c, grp, y = p["c"], p["grp"], p["y"]
            grows = slice(GC * grp, GC * (grp + 1))
            u_s[c, grows, :] = y[:, 0:dk]
            wq_s[c, grp, 0:GC, :] = y[:, dk:2 * dk].astype(BF16)
            wq_s[c, grp, GC:2 * GC, :] = (p["q"] * p["eg"]).astype(BF16)
            qk_s[c, grows, :] = p["qk"].astype(BF16)
            kdt_s[c, :, grows] = p["kd"].T.astype(BF16)
        return carry

    lax.fori_loop(0, tt // (PREP_GROUP * C), prepare_group, 0)

    def head_blocks(x, row0):
        return jnp.concatenate(
            [x[row0 + C * r:row0 + C * (r + 1), dk * r:dk * (r + 1)] for r in range(GDN_GROUP)], axis=0)

    def scan(c, carry):
        rows = pl.ds(pl.multiple_of(c * C, C), C)
        groups = range(H // GDN_GROUP)
        lanes = [slice(GDN_GROUP * dk * g, GDN_GROUP * dk * (g + 1)) for g in groups]
        grows = [slice(GC * g, GC * (g + 1)) for g in groups]
        st = [state_s[:, lanes[g]] for g in groups]
        d1 = [_dot(wq_s[c, g], st[g].astype(BF16)) for g in groups]
        v16 = [(u_s[c, grows[g], :] - head_blocks(d1[g], 0)).astype(BF16) for g in groups]
        o = [head_blocks(d1[g], GC) + _dot(qk_s[c, grows[g], :], v16[g]) for g in groups]
        for g in groups:
            v_bd = jnp.where(state_bd, jnp.concatenate([v16[g]] * GDN_GROUP, axis=1), jnp.zeros((), BF16))
            state_s[:, lanes[g]] = st[g] * gl_s[c, 0:1, lanes[g]] + _dot(kdt_s[c, :, grows[g]], v_bd)
            for r in range(GDN_GROUP):
                h = GDN_GROUP * g + r
                o_s[rows, dk * h:dk * (h + 1)] = o[g][C * r:C * (r + 1)]
        return carry

    lax.fori_loop(0, tt // C, scan, 0)

    nw = nw_ref[...]
    z = z_ref[...]
    for h in range(H):
        cols = slice(dk * h, dk * (h + 1))
        oh = o_s[:, cols]
        y = oh * lax.rsqrt(jnp.mean(oh * oh, axis=-1, keepdims=True) + NORM_EPS) * nw
        o_ref[:, cols] = (y * _silu(z[:, cols])).astype(BF16)


def _gated_delta_net(gdn, small, a_rows, conv_w, alog_row, dtb_row, alog_col, dtb_col, norm_w, tt=512):
    B, S, _ = gdn.shape
    nch = tt // GDN_CHUNK
    hc = C_HEADS * GDN_CHUNK
    part = lambda j: pl.BlockSpec((None, tt, C_QK), lambda b, i: (b, i, j))
    halo = lambda j: pl.BlockSpec((None, 8, C_QK), lambda b, i: (b, jnp.maximum(i * (tt // 8) - 1, 0), j))
    full = lambda a: pl.BlockSpec(a.shape, lambda b, i: (0,) * a.ndim)
    return pl.pallas_call(
        functools.partial(_gdn_kernel, tt=tt),
        grid=(B, S // tt),
        in_specs=[part(0), part(1), part(2), halo(0), halo(1), halo(2), part(3),
                  pl.BlockSpec((None, tt, W_SMALL), lambda b, i: (b, i, 0)),
                  pl.BlockSpec((None, nch, 8, GDN_CHUNK), lambda b, i: (b, i, 0, 0)),
                  full(conv_w), full(alog_row), full(dtb_row), full(alog_col), full(dtb_col), full(norm_w)],
        out_specs=pl.BlockSpec((None, tt, C_QK), lambda b, i: (b, i, 0)),
        out_shape=jax.ShapeDtypeStruct((B, S, C_QK), BF16),
        scratch_shapes=[pltpu.VMEM((tt, C_QK), F32), pltpu.VMEM((tt, C_QK), F32), pltpu.VMEM((tt, C_QK), F32),
                        pltpu.VMEM((tt, W_SMALL), F32), pltpu.VMEM((tt, W_SMALL), F32),
                        pltpu.VMEM((tt, C_QK), F32), pltpu.VMEM((C_HEAD_DIM, C_QK), F32),
                        pltpu.VMEM((nch, hc, C_HEAD_DIM), F32), pltpu.VMEM((nch, C_HEADS // GDN_GROUP, 2 * GDN_GROUP * GDN_CHUNK, C_HEAD_DIM), BF16),
                        pltpu.VMEM((nch, hc, GDN_GROUP * GDN_CHUNK), BF16), pltpu.VMEM((nch, C_HEAD_DIM, hc), BF16),
                        pltpu.VMEM((nch, 8, C_QK), F32)] + [pltpu.VMEM((tt + 8, C_QK), F32)] * 3,
        compiler_params=_cparams("arbitrary", "arbitrary"),
        name="gated_delta_net",
    )(gdn, gdn, gdn, gdn, gdn, gdn, gdn, small, a_rows, conv_w, alog_row, dtb_row, alog_col, dtb_col, norm_w)


def _outproj_kernel(x_ref, oa_ref, ob_ref, oc_ref, w_ref, gt_ref, o_ref):
    y = _dot(oa_ref[...], w_ref[0:A_Q, :])
    y = y + _dot(ob_ref[...], w_ref[A_Q:A_Q + B_Q, :])
    y = y + _dot(oc_ref[...], w_ref[A_Q + B_Q:, :])
    o_ref[...] = x_ref[...] + gt_ref[...] * y


def _out_projection(x, o_a, o_b, o_c, w, mod, tm=1024):
    B, S, D = x.shape
    row = lambda b, i: (b, i, 0)
    return pl.pallas_call(
        _outproj_kernel,
        grid=(B, S // tm),
        in_specs=[
            pl.BlockSpec((None, tm, D), row),
            pl.BlockSpec((None, tm, A_Q), row),
            pl.BlockSpec((None, tm, B_Q), row),
            pl.BlockSpec((None, tm, C_QK), row),
            pl.BlockSpec((D, D), lambda b, i: (0, 0)),
            pl.BlockSpec((None, 1, D), lambda b, i: (b, 0, 2)),
        ],
        out_specs=pl.BlockSpec((None, tm, D), row),
        out_shape=jax.ShapeDtypeStruct((B, S, D), F32),
        compiler_params=_cparams("arbitrary", "arbitrary"),
        name="out_projection",
    )(x, o_a, o_b, o_c, w, mod)


def _ffn_kernel(layer_ref, x_ref, g_ref, sc_ref, sh_ref, gt_ref, wgu_ref, wd_ref, o_ref, *, tf):
    del layer_ref
    x = x_ref[...]
    h = _rms_mod(x, g_ref[...], sc_ref[...], sh_ref[...]).astype(BF16)
    acc = None
    for f in range(D_FF // tf):
        gate = _dot(h, wgu_ref[:, tf * f:tf * (f + 1)])
        up = _dot(h, wgu_ref[:, D_FF + tf * f:D_FF + tf * (f + 1)])
        part = _dot((_silu(gate) * up).astype(BF16), wd_ref[tf * f:tf * (f + 1), :])
        acc = part if acc is None else acc + part
    o_ref[...] = x + gt_ref[...] * acc


def _ffn(x, gain, mod, w_gate_up_all, w_down_all, layer, tm=512, tf=1408):
    B, S, D = x.shape
    row = lambda b, i, l: (b, i, 0)
    resident = lambda a: pl.BlockSpec((None,) + a.shape[1:], lambda b, i, l: (l[0], 0, 0), pipeline_mode=pl.Buffered(1))
    return pl.pallas_call(
        functools.partial(_ffn_kernel, tf=tf),
        grid_spec=pltpu.PrefetchScalarGridSpec(
            num_scalar_prefetch=1,
            grid=(B, S // tm),
            in_specs=[
                pl.BlockSpec((None, tm, D), row),
                pl.BlockSpec((1, D), lambda b, i, l: (0, 0)),
                pl.BlockSpec((None, 1, D), lambda b, i, l: (b, 0, 4)),
                pl.BlockSpec((None, 1, D), lambda b, i, l: (b, 0, 3)),
                pl.BlockSpec((None, 1, D), lambda b, i, l: (b, 0, 5)),
                resident(w_gate_up_all),
                resident(w_down_all),
            ],
            out_specs=pl.BlockSpec((None, tm, D), row),
        ),
        out_shape=jax.ShapeDtypeStruct((B, S, D), F32),
        compiler_params=_cparams("arbitrary", "arbitrary"),
        name="swiglu_ffn",
    )(layer, x, gain, mod, mod, mod, w_gate_up_all, w_down_all)


def _final_norm_kernel(x_ref, g_ref, o_ref):
    x = x_ref[...]
    o_ref[...] = x * lax.rsqrt(jnp.mean(x * x, axis=-1, keepdims=True) + NORM_EPS) * g_ref[...]


def _final_norm(x, gain, tm=512):
    B, S, D = x.shape
    return pl.pallas_call(
        _final_norm_kernel,
        grid=(B, S // tm),
        in_specs=[pl.BlockSpec((None, tm, D), lambda b, i: (b, i, 0)), pl.BlockSpec((1, D), lambda b, i: (0, 0))],
        out_specs=pl.BlockSpec((None, tm, D), lambda b, i: (b, i, 0)),
        out_shape=jax.ShapeDtypeStruct((B, S, D), F32),
        compiler_params=_cparams("arbitrary", "arbitrary"),
        name="final_norm",
    )(x, gain)


def _rope_tables(seq):
    inv = 1.0 / (ROPE_THETA ** (jnp.arange(0, HEAD_DIM, 2, dtype=F32) / HEAD_DIM))
    ang = jnp.arange(seq, dtype=F32)[:, None] * inv[None, :]
    cos, sin = jnp.cos(ang), jnp.sin(ang)
    zero = jnp.zeros_like(sin)
    cos_t = jnp.tile(cos, (1, 4))
    sa_t = jnp.tile(jnp.concatenate([-sin, zero], axis=1), (1, 2))
    sb_t = jnp.tile(jnp.concatenate([zero, sin], axis=1), (1, 2))
    return cos_t, sa_t, sb_t


def _permute_w_in(w_in):
    pts = np.cumsum(IN_SPLITS)[:-1].tolist()
    aq, ak, av, bq, bkc, bvc, bks, bvs, bkw, bvw, bg, cqkv, cz, ca, cb = jnp.split(w_in, pts, axis=-1)
    pad = jnp.zeros(w_in.shape[:-1] + (W_SMALL - B_GATES - 2 * C_HEADS,), w_in.dtype)
    return jnp.concatenate([aq, bq, ak, bks, bkw, av, bvs, bvw, bkc, bvc, bg, ca, cb, pad, cqkv, cz], axis=-1).astype(BF16)


def _compress_weights(k_w1, k_w2, v_w1, v_w2, pe_k, pe_v):
    L = k_w1.shape[0]
    half = CMP_STRIDE * HEAD_DIM

    def first_layer(lo):
        wk = k_w1[:, lo:lo + half].reshape(L, CMP_STRIDE, HEAD_DIM, CMP_HIDDEN)
        wv = v_w1[:, lo:lo + half].reshape(L, CMP_STRIDE, HEAD_DIM, CMP_HIDDEN)
        z = jnp.zeros_like(wk)
        w = jnp.concatenate([jnp.concatenate([wk, z], axis=-1), jnp.concatenate([z, wv], axis=-1)], axis=2)
        return w.reshape(L, 2 * half, 2 * CMP_HIDDEN).astype(BF16)

    def pe_rows(lo):
        return jnp.concatenate([pe_k[:, lo:lo + CMP_STRIDE], pe_v[:, lo:lo + CMP_STRIDE]], axis=-1).reshape(L, 1, 2 * half)

    z2 = jnp.zeros_like(k_w2)
    w2 = jnp.concatenate([jnp.concatenate([k_w2, z2], axis=-1), jnp.concatenate([z2, v_w2], axis=-1)], axis=1).astype(BF16)
    return pe_rows(0), pe_rows(CMP_STRIDE), first_layer(0), first_layer(half), w2


def _pool_matrix(n_cmp_rows, n_slc):
    ratio = SLC_LEN // CMP_STRIDE
    n = np.arange(n_cmp_rows)[:, None]
    j = np.arange(n_slc)[None, :]
    return jnp.asarray(((n // ratio == j) | (n == ratio * j - 1)).astype(np.float32))


def _lane_row(v, lane0):
    L, n = v.shape
    return jnp.zeros((L, 1, 128), F32).at[:, 0, lane0:lane0 + n].set(v)


def _sublane_col(v):
    L, n = v.shape
    return jnp.zeros((L, 8, 1), F32).at[:, :n, 0].set(v)


def kernel(x, c, norm_mix, norm_ffn, ada_w, ada_b, w_in, attn_sinks, cmp_k_w1, cmp_k_w2, cmp_v_w1, cmp_v_w2, cmp_pe_k, cmp_pe_v, gdn_conv_w, gdn_A_log, gdn_dt_bias, gdn_norm, w_out, w_gate_up, w_down, final_norm):
    B, S, D = x.shape
    L = w_in.shape[0]
    n16 = S // CMP_STRIDE
    n_slc = S // SLC_LEN

    cos_t, sa_t, sb_t = _rope_tables(S)
    mod = _adaln_mod(c, ada_w, ada_b)
    pea, peb, w1a, w1b, w2c = _compress_weights(cmp_k_w1, cmp_k_w2, cmp_v_w1, cmp_v_w2, cmp_pe_k, cmp_pe_v)
    pool_t = _pool_matrix(n16, n_slc).T
    w_gate_up_all = w_gate_up.astype(BF16)
    w_down_all = w_down.astype(BF16)
    per_layer = dict(
        mod=mod,
        norm_mix=norm_mix.reshape(L, 1, D), norm_ffn=norm_ffn.reshape(L, 1, D),
        w_in=_permute_w_in(w_in), sinks=attn_sinks,
        pea=pea, peb=peb, w1a=w1a, w1b=w1b, w2c=w2c,
        conv_w=gdn_conv_w,
        alog_row=_lane_row(gdn_A_log, SMALL_A), dtb_row=_lane_row(gdn_dt_bias, SMALL_A),
        alog_col=_sublane_col(gdn_A_log), dtb_col=_sublane_col(gdn_dt_bias),
        gdn_norm=gdn_norm.reshape(L, 1, C_HEAD_DIM),
        w_out=w_out.astype(BF16), layer=jnp.arange(L, dtype=jnp.int32).reshape(L, 1),
    )

    def layer(xc, p):
        rope, bq, vals, cmp_in, small, gdn = _in_projection(xc, p["norm_mix"], p["mod"], p["w_in"], cos_t, sa_t, sb_t)
        vals_all_t = vals.reshape(B, S // KV_BLOCK, KV_BLOCK, W_V).transpose(0, 1, 3, 2)
        o_a = _swa_attention(rope, vals_all_t, p["sinks"])
        kv_cmp, kv_cmp_t = _compress(cmp_in.reshape(B, n16, CMP_STRIDE * W_CMP), p["pea"], p["peb"], p["w1a"], p["w1b"], p["w2c"])
        o_cmp_t, sel = _cmp_select(bq, kv_cmp, kv_cmp_t, pool_t)
        o_b = _nsa_attention(rope, vals_all_t, sel, o_cmp_t, small)
        a_rows = small[:, :, SMALL_A:SMALL_A + 8].reshape(B, S // GDN_CHUNK, GDN_CHUNK, 8).transpose(0, 1, 3, 2)
        o_c = _gated_delta_net(gdn, small, a_rows, p["conv_w"], p["alog_row"], p["dtb_row"], p["alog_col"], p["dtb_col"], p["gdn_norm"])
        xc = _out_projection(xc, o_a, o_b, o_c, p["w_out"], p["mod"])
        xc = _ffn(xc, p["norm_ffn"], p["mod"], w_gate_up_all, w_down_all, p["layer"])
        return xc, None

    x, _ = lax.scan(layer, x, per_layer)
    return _final_norm(x, final_norm.reshape(1, D))
```

```python
import functools
import math

import numpy as np
import jax
import jax.numpy as jnp
from jax import lax
from jax.experimental import pallas as pl
from jax.experimental.pallas import tpu as pltpu

F32 = jnp.float32
BF16 = jnp.bfloat16
HIGHEST = lax.Precision.HIGHEST

D_MODEL = 1024
DEPTH = 4
HEAD_DIM = 64
ATTN_SCALE = HEAD_DIM ** -0.5
ROPE_THETA = 10000.0
NEG_INF = -1e30
NORM_EPS = 1e-6

A_HEADS = 4
A_KV_HEADS = 2
A_WINDOW = 128

B_HEADS = 4
CMP_STRIDE = 16
CMP_LEN = 32
CMP_HIDDEN = 256
SLC_LEN = 64
SLC_TOPK = 16
NSA_WINDOW = 512
SLC_FORCED_SCORE = 1e9

C_HEAD_DIM = 128
C_HEADS = 4
CONV_WIDTH = 4
GDN_CHUNK = 64
C_QK = C_HEADS * C_HEAD_DIM

D_FF = 2816

A_Q = A_HEADS * HEAD_DIM
A_KV = A_KV_HEADS * HEAD_DIM
B_Q = B_HEADS * HEAD_DIM
B_KV = HEAD_DIM
B_GATES = 3 * B_HEADS
IN_SPLITS = (A_Q, A_KV, A_KV, B_Q, B_KV, B_KV, B_KV, B_KV, B_KV, B_KV, B_GATES, 3 * C_QK, C_QK, C_HEADS, C_HEADS)

W_ROPE = 768
W_V = 256
W_CMP = 128
W_SMALL = 128
W_GDN = 2048
W_TOTAL = W_ROPE + W_V + W_CMP + W_SMALL + W_GDN
SMALL_A = B_GATES
SMALL_B = B_GATES + C_HEADS

VMEM_LIMIT = 56 * 1024 * 1024
PREP_GROUP = 8
GDN_GROUP = 1

NT_DIMS = (((1,), (1,)), ((), ()))


def _cparams(*sem):
    return pltpu.CompilerParams(dimension_semantics=sem, vmem_limit_bytes=VMEM_LIMIT)


def _iota(shape, dim):
    return lax.broadcasted_iota(jnp.int32, shape, dim)


def _dot(a, b, precision=None):
    return jnp.dot(a, b, preferred_element_type=F32, precision=precision)


def _dot_nt(a, b):
    return lax.dot_general(a, b, NT_DIMS, preferred_element_type=F32)


def _silu(x):
    return x * jax.nn.sigmoid(x)


def _softplus(x):
    return jnp.maximum(x, 0.0) + jnp.log(1.0 + jnp.exp(-jnp.abs(x)))


def _rms_mod(x, gain, sc, sh):
    y = x * lax.rsqrt(jnp.mean(x * x, axis=-1, keepdims=True) + NORM_EPS)
    return (y * gain) * (1.0 + sc) + sh


def _mod_kernel(c_ref, w_ref, b_ref, o_ref):
    o_ref[...] = _dot(_silu(c_ref[...]), w_ref[...], HIGHEST) + b_ref[...]


def _adaln_mod(c, ada_w, ada_b):
    L, D, N = ada_w.shape
    tn = 1536
    c8 = jnp.zeros((8, D), F32).at[: c.shape[0]].set(c)
    out = pl.pallas_call(
        _mod_kernel,
        grid=(L, N // tn),
        in_specs=[
            pl.BlockSpec((8, D), lambda l, j: (0, 0)),
            pl.BlockSpec((None, D, tn), lambda l, j: (l, 0, j)),
            pl.BlockSpec((None, 1, tn), lambda l, j: (l, 0, j)),
        ],
        out_specs=pl.BlockSpec((None, 8, tn), lambda l, j: (l, 0, j)),
        out_shape=jax.ShapeDtypeStruct((L, 8, N), F32),
        compiler_params=_cparams("arbitrary", "arbitrary"),
        name="adaln_mod",
    )(c8, ada_w, ada_b.reshape(L, 1, N))
    return out.reshape(L, 8, 1, N)


def _inproj_kernel(x_ref, g_ref, sc_ref, sh_ref, w_ref, cos_ref, sa_ref, sb_ref,
                   rope_ref, bq_ref, v_ref, cmp_ref, small_ref, gdn_ref):
    h = _rms_mod(x_ref[...], g_ref[...], sc_ref[...], sh_ref[...]).astype(BF16)
    yr = _dot(h, w_ref[:, 0:W_ROPE])
    bq_ref[...] = yr[:, A_Q:A_Q + B_Q].astype(BF16)
    c, sa, sb = cos_ref[...], sa_ref[...], sb_ref[...]
    for g in range(W_ROPE // 128):
        xg = yr[:, 128 * g:128 * (g + 1)]
        rot = xg * c + pltpu.roll(xg, 96, 1) * sa + pltpu.roll(xg, 32, 1) * sb
        rope_ref[:, 128 * g:128 * (g + 1)] = rot.astype(BF16)
    o = W_ROPE
    yp = _dot(h, w_ref[:, o:o + W_V + W_CMP + W_SMALL])
    v_ref[...] = yp[:, 0:W_V].astype(BF16)
    cmp_ref[...] = yp[:, W_V:W_V + W_CMP]
    small_ref[...] = yp[:, W_V + W_CMP:W_V + W_CMP + W_SMALL]
    o += W_V + W_CMP + W_SMALL
    gdn_ref[...] = _dot(h, w_ref[:, o:o + W_GDN])


def _in_projection(x, gain, mod, w, cos_t, sa_t, sb_t, tm=512):
    B, S, D = x.shape
    row = lambda b, i: (b, i, 0)
    outs = pl.pallas_call(
        _inproj_kernel,
        grid=(B, S // tm),
        in_specs=[
            pl.BlockSpec((None, tm, D), row),
            pl.BlockSpec((1, D), lambda b, i: (0, 0)),
            pl.BlockSpec((None, 1, D), lambda b, i: (b, 0, 1)),
            pl.BlockSpec((None, 1, D), lambda b, i: (b, 0, 0)),
            pl.BlockSpec((D, W_TOTAL), lambda b, i: (0, 0)),
            pl.BlockSpec((tm, 128), lambda b, i: (i, 0)),
            pl.BlockSpec((tm, 128), lambda b, i: (i, 0)),
            pl.BlockSpec((tm, 128), lambda b, i: (i, 0)),
        ],
        out_specs=[
            pl.BlockSpec((None, tm, W_ROPE), row),
            pl.BlockSpec((None, tm, B_Q), row),
            pl.BlockSpec((None, tm, W_V), row),
            pl.BlockSpec((None, tm, W_CMP), row),
            pl.BlockSpec((None, tm, W_SMALL), row),
            pl.BlockSpec((None, tm, W_GDN), row),
        ],
        out_shape=[
            jax.ShapeDtypeStruct((B, S, W_ROPE), BF16),
            jax.ShapeDtypeStruct((B, S, B_Q), BF16),
            jax.ShapeDtypeStruct((B, S, W_V), BF16),
            jax.ShapeDtypeStruct((B, S, W_CMP), F32),
            jax.ShapeDtypeStruct((B, S, W_SMALL), F32),
            jax.ShapeDtypeStruct((B, S, W_GDN), F32),
        ],
        compiler_params=_cparams("arbitrary", "arbitrary"),
        name="in_projection",
    )(x, gain, mod, mod, w, cos_t, sa_t, sb_t)
    return outs


def _swa_kernel(sink_ref, q_ref, k_ref, vt_ref, o_ref, *, tq, window):
    q0 = pl.program_id(1) * tq
    span = tq + window
    start = pl.multiple_of(jnp.maximum(q0 - window, 0), KV_BLOCK)
    qt = q_ref[...].astype(F32).T * ATTN_SCALE
    kpos = start + _iota((span, tq), 0)
    qpos = q0 + _iota((span, tq), 1)
    bias = jnp.where((kpos <= qpos) & (kpos > qpos - window), 0.0, NEG_INF)
    vb = vt_ref[pl.ds(start // KV_BLOCK, span // KV_BLOCK)]
    group = A_HEADS // A_KV_HEADS
    first_head = _iota((1, group * tq), 1) < tq
    bias = jnp.concatenate([bias] * group, axis=1)
    groups = range(A_KV_HEADS)
    qg = [jnp.concatenate([qt[HEAD_DIM * h:HEAD_DIM * (h + 1)] for h in range(group * g, group * (g + 1))],
                          axis=1).astype(BF16) for g in groups]
    s = [_dot(k_ref[pl.ds(start, span), HEAD_DIM * g:HEAD_DIM * (g + 1)], qg[g]) + bias for g in groups]
    sink = [jnp.where(first_head, sink_ref[group * g], sink_ref[group * g + 1]) for g in groups]
    m = [jnp.maximum(jnp.max(s[g], axis=0, keepdims=True), sink[g]) for g in groups]
    p = [jnp.exp(s[g] - m[g]) for g in groups]
    den = [jnp.sum(p[g], axis=0, keepdims=True) + jnp.exp(sink[g] - m[g]) for g in groups]
    vt = [jnp.concatenate([vb[r, HEAD_DIM * g:HEAD_DIM * (g + 1), :] for r in range(span // KV_BLOCK)], axis=1)
          for g in groups]
    o_t = [_dot(vt[g], p[g].astype(BF16)) * (1.0 / den[g]) for g in groups]
    outs = [o_t[g][:, tq * r:tq * (r + 1)] for g in groups for r in range(group)]
    o_ref[...] = jnp.concatenate(outs, axis=0).T.astype(BF16)


def _swa_attention(rope, vals_t, sinks, tq=256):
    assert A_HEADS // A_KV_HEADS == 2
    B, S, _ = rope.shape
    return pl.pallas_call(
        functools.partial(_swa_kernel, tq=tq, window=A_WINDOW),
        grid=(B, S // tq),
        in_specs=[
            pl.BlockSpec(memory_space=pltpu.SMEM),
            pl.BlockSpec((None, tq, A_Q), lambda b, i: (b, i, 0)),
            pl.BlockSpec((None, S, A_KV), lambda b, i: (b, 0, (A_Q + B_Q) // A_KV)),
            pl.BlockSpec((None, S // KV_BLOCK, A_KV, KV_BLOCK), lambda b, i: (b, 0, 0, 0)),
        ],
        out_specs=pl.BlockSpec((None, tq, A_Q), lambda b, i: (b, i, 0)),
        out_shape=jax.ShapeDtypeStruct((B, S, A_Q), BF16),
        compiler_params=_cparams("arbitrary", "arbitrary"),
        name="swa_attention",
    )(sinks, rope, rope, vals_t)


def _compress_kernel(x_ref, pea_ref, peb_ref, w1a_ref, w1b_ref, w2_ref, o_ref, ot_ref):
    x = x_ref[...]
    n = x.shape[0]
    a = _dot((x + pea_ref[...]).astype(BF16), w1a_ref[...])
    b = _dot((x + peb_ref[...]).astype(BF16), w1b_ref[...])
    hid = _silu(a + pltpu.roll(b, n - 1, 0))
    out = _dot(hid.astype(BF16), w2_ref[...])
    o_ref[...] = out.astype(BF16)
    ot_ref[...] = out.T.astype(BF16)


def _compress(xc, pea, peb, w1a, w1b, w2):
    B, n, K = xc.shape
    full = lambda a: pl.BlockSpec(a.shape, lambda b: (0,) * a.ndim)
    return pl.pallas_call(
        _compress_kernel,
        grid=(B,),
        in_specs=[pl.BlockSpec((None, n, K), lambda b: (b, 0, 0)), full(pea), full(peb), full(w1a), full(w1b), full(w2)],
        out_specs=[pl.BlockSpec((None, n, 2 * HEAD_DIM), lambda b: (b, 0, 0)),
                   pl.BlockSpec((None, 2 * HEAD_DIM, n), lambda b: (b, 0, 0))],
        out_shape=[jax.ShapeDtypeStruct((B, n, 2 * HEAD_DIM), BF16), jax.ShapeDtypeStruct((B, 2 * HEAD_DIM, n), BF16)],
        compiler_params=_cparams("arbitrary"),
        name="nsa_compress",
    )(xc, pea, peb, w1a, w1b, w2)


def _heads_on_lanes(q):
    qt = q.astype(F32).T * ATTN_SCALE
    return jnp.concatenate([qt[HEAD_DIM * h:HEAD_DIM * (h + 1)] for h in range(B_HEADS)], axis=1).astype(BF16)


def _heads_on_sublanes(ot, tq):
    return jnp.concatenate([ot[:, tq * h:tq * (h + 1)] for h in range(B_HEADS)], axis=0)


def _cmp_select_kernel(q_ref, k_ref, vt_ref, pool_ref, o_ref, sel_ref, *, tq, top_k):
    q0 = pl.program_id(1) * tq
    H = B_HEADS
    n_cmp = k_ref.shape[0]
    n_slc = pool_ref.shape[0]
    qt = _heads_on_lanes(q_ref[...])
    valid = _iota((n_cmp, tq), 0) * CMP_STRIDE + (CMP_LEN - 1) <= q0 + _iota((n_cmp, tq), 1)
    valid = jnp.concatenate([valid] * (H // 2), axis=1)
    halves = [slice(0, H // 2 * tq), slice(H // 2 * tq, H * tq)]
    kc = k_ref[:, 0:HEAD_DIM]
    vc_t = vt_ref[HEAD_DIM:2 * HEAD_DIM, :]
    s = [jnp.where(valid, _dot(kc, qt[:, hs]), NEG_INF) for hs in halves]
    e = [jnp.where(valid, jnp.exp(si - jnp.max(si, axis=0, keepdims=True)), 0.0) for si in s]
    den = [jnp.sum(ei, axis=0, keepdims=True) for ei in e]
    p = [ei * (1.0 / jnp.where(di > 0.0, di, 1.0)) for ei, di in zip(e, den)]
    o_t = jnp.concatenate([_dot(vc_t, pi.astype(BF16)) for pi in p], axis=1)
    o_ref[...] = _heads_on_sublanes(o_t, tq)
    psum = None
    for pi in p:
        for r in range(H // 2):
            part = pi[:, tq * r:tq * (r + 1)]
            psum = part if psum is None else psum + part

    imp = _dot(pool_ref[...], psum, HIGHEST)
    blk = _iota((n_slc, tq), 0)
    tt = q0 + _iota((n_slc, tq), 1)
    cur = tt >> 6
    forced = (blk == 0) | (blk == cur) | (blk == cur - 1)
    causal = blk * SLC_LEN <= tt
    score = jnp.where(forced, SLC_FORCED_SCORE, jnp.where(causal, imp, NEG_INF))
    sel = jnp.zeros((n_slc, tq), F32)
    blk_f = blk.astype(F32)
    for _ in range(top_k):
        mx = jnp.max(score, axis=0, keepdims=True)
        first = jnp.min(jnp.where(score == mx, blk_f, float(n_slc)), axis=0, keepdims=True)
        pick = blk_f == first
        sel = jnp.where(pick, 1.0, sel)
        score = jnp.where(pick, -jnp.inf, score)
    sel_ref[...] = jnp.where(causal, sel, 0.0)


def _cmp_select(bq, kv_cmp, kv_cmp_t, pool_t, tq=256):
    B, S, _ = bq.shape
    n_slc, n_cmp = pool_t.shape
    top_k = min(SLC_TOPK, n_slc)
    return pl.pallas_call(
        functools.partial(_cmp_select_kernel, tq=tq, top_k=top_k),
        grid=(B, S // tq),
        in_specs=[
            pl.BlockSpec((None, tq, B_Q), lambda b, i: (b, i, 0)),
            pl.BlockSpec((None, n_cmp, 2 * HEAD_DIM), lambda b, i: (b, 0, 0)),
            pl.BlockSpec((None, 2 * HEAD_DIM, n_cmp), lambda b, i: (b, 0, 0)),
            pl.BlockSpec((n_slc, n_cmp), lambda b, i: (0, 0)),
        ],
        out_specs=[
            pl.BlockSpec((None, B_Q, tq), lambda b, i: (b, 0, i)),
            pl.BlockSpec((None, n_slc, tq), lambda b, i: (b, 0, i)),
        ],
        out_shape=[jax.ShapeDtypeStruct((B, B_Q, S), F32), jax.ShapeDtypeStruct((B, n_slc, S), F32)],
        compiler_params=_cparams("arbitrary", "arbitrary"),
        name="nsa_cmp_select",
    )(bq, kv_cmp, kv_cmp_t, pool_t)


KV_BLOCK = 128
PAD_ROWS = 16
SLC_TILE = 512


def _values_with_ones(vb, row0, n):
    vt = jnp.concatenate([vb[r, row0:row0 + HEAD_DIM, :] for r in range(n)], axis=1)
    return jnp.concatenate([vt, jnp.ones((PAD_ROWS, n * KV_BLOCK), BF16)], axis=0)


def _nsa_attn_kernel(q_ref, k_ref, ka_ref, vt_ref, sel_ref, ocmp_ref, small_ref, o_ref, *, tq, tk, window):
    i = pl.program_id(1)
    q0 = i * tq
    H = B_HEADS
    d = HEAD_DIM
    qt = _heads_on_lanes(q_ref[...])
    bpt = tk // SLC_LEN
    vpt = tk // KV_BLOCK
    q_pad = jnp.zeros((ka_ref.shape[1] - d - PAD_ROWS, H * tq), BF16)
    pen_pad = jnp.zeros((PAD_ROWS - bpt, tq), F32)

    def scores(j):
        sel = sel_ref[pl.ds(pl.multiple_of(j * bpt, bpt), bpt), :]
        pen = jnp.concatenate([jnp.where(sel > 0.5, 0.0, NEG_INF), pen_pad], axis=0)
        q_aug = jnp.concatenate([qt, jnp.concatenate([pen] * H, axis=1).astype(BF16), q_pad], axis=0)
        return _dot(ka_ref[pl.ds(pl.multiple_of(j * tk, tk), tk), :], q_aug)

    def update(j, s, carry):
        m, acc = carry
        m_new = jnp.maximum(m, jnp.max(s, axis=0, keepdims=True))
        alpha = jnp.exp(m - m_new)
        p = jnp.exp(s - m_new)
        vb = vt_ref[pl.ds(pl.multiple_of(j * vpt, vpt), vpt)]
        return m_new, acc * alpha + _dot(_values_with_ones(vb, 0, vpt), p.astype(BF16))

    def full_tiles(n):
        def body(jg, carry):
            s = [scores(n * jg + r) for r in range(n)]
            for r in range(n):
                carry = update(n * jg + r, s[r], carry)
            return carry
        return body

    qpos = q0 + _iota((tk, tq), 1)

    def causal_tile(j, carry):
        causal = jnp.where(j * tk + _iota((tk, tq), 0) <= qpos, 0.0, NEG_INF)
        return update(j, scores(j) + jnp.concatenate([causal] * H, axis=1), carry)

    n_full_pairs = q0 // (2 * tk)
    n_tiles = (q0 + tq + tk - 1) // tk
    carry = (jnp.full((1, H * tq), NEG_INF, F32), jnp.zeros((d + PAD_ROWS, H * tq), F32))
    carry = lax.fori_loop(0, n_full_pairs, full_tiles(2), carry)
    _, acc = lax.fori_loop(2 * n_full_pairs, n_tiles, causal_tile, carry)
    o_slc = _heads_on_sublanes(acc[0:d] * (1.0 / acc[d:d + 1]), tq)

    span = tq + window
    start = pl.multiple_of(jnp.maximum(q0 - window, 0), KV_BLOCK)
    kw = k_ref[pl.ds(start, span), d:2 * d]
    kpos = start + _iota((span, tq), 0)
    qp = q0 + _iota((span, tq), 1)
    bias = jnp.where((kpos <= qp) & (kpos > qp - window), 0.0, NEG_INF)
    bias = jnp.concatenate([bias] * (H // 2), axis=1)
    vw = _values_with_ones(vt_ref[pl.ds(start // KV_BLOCK, span // KV_BLOCK)], d, span // KV_BLOCK)
    halves = [slice(0, H // 2 * tq), slice(H // 2 * tq, H * tq)]
    s = [_dot(kw, qt[:, hs]) + bias for hs in halves]
    p = [jnp.exp(si - jnp.max(si, axis=0, keepdims=True)) for si in s]
    res = jnp.concatenate([_dot(vw, pi.astype(BF16)) for pi in p], axis=1)
    o_win = _heads_on_sublanes(res[0:d] * (1.0 / res[d:d + 1]), tq)

    gates = jax.nn.sigmoid(small_ref[...]).T
    o_cmp = ocmp_ref[...]
    outs = []
    for h in range(H):
        rows = slice(HEAD_DIM * h, HEAD_DIM * (h + 1))
        outs.append(gates[h:h + 1, :] * o_cmp[rows] + gates[H + h:H + h + 1, :] * o_slc[rows]
                    + gates[2 * H + h:2 * H + h + 1, :] * o_win[rows])
    o_ref[...] = jnp.concatenate(outs, axis=0).T.astype(BF16)


def _slc_keys_with_block_onehot(rope):
    B, S, _ = rope.shape
    t = np.arange(S)[:, None]
    onehot = (t // SLC_LEN) % (SLC_TILE // SLC_LEN) == np.arange(HEAD_DIM)[None, :]
    onehot = jnp.broadcast_to(jnp.asarray(onehot.astype(np.float32)).astype(BF16), (B, S, HEAD_DIM))
    off = A_Q + B_Q + A_KV
    return jnp.concatenate([rope[:, :, off:off + HEAD_DIM], onehot], axis=-1)


def _nsa_attention(rope, vals_t, sel, o_cmp, small, tq=256):
    tk = SLC_TILE
    B, S, _ = rope.shape
    n_slc = sel.shape[1]
    keys_aug = _slc_keys_with_block_onehot(rope)
    return pl.pallas_call(
        functools.partial(_nsa_attn_kernel, tq=tq, tk=tk, window=NSA_WINDOW),
        grid=(B, S // tq),
        in_specs=[
            pl.BlockSpec((None, tq, B_Q), lambda b, i: (b, i, A_Q // B_Q)),
            pl.BlockSpec((None, S, 128), lambda b, i: (b, 0, (A_Q + B_Q + A_KV) // 128)),
            pl.BlockSpec((None, S, 2 * HEAD_DIM), lambda b, i: (b, 0, 0)),
            pl.BlockSpec((None, S // KV_BLOCK, 2 * HEAD_DIM, KV_BLOCK), lambda b, i: (b, 0, A_KV // (2 * HEAD_DIM), 0)),
            pl.BlockSpec((None, n_slc, tq), lambda b, i: (b, 0, i)),
            pl.BlockSpec((None, B_Q, tq), lambda b, i: (b, 0, i)),
            pl.BlockSpec((None, tq, W_SMALL), lambda b, i: (b, i, 0)),
        ],
        out_specs=pl.BlockSpec((None, tq, B_Q), lambda b, i: (b, i, 0)),
        out_shape=jax.ShapeDtypeStruct((B, S, B_Q), BF16),
        compiler_params=_cparams("arbitrary", "arbitrary"),
        name="nsa_attention",
    )(rope, rope, keys_aug, vals_t, sel, o_cmp, small)


def _gdn_kernel(xq_ref, xk_ref, xv_ref, hq_ref, hk_ref, hv_ref, z_ref, small_ref, arow_ref,
                cw_ref, alog_row_ref, dtb_row_ref, alog_col_ref, dtb_col_ref, nw_ref,
                o_ref, q_s, k_s, v_s, g_s, b_s, o_s, state_s, u_s, wq_s, qk_s, kdt_s, gl_s, pq_s, pk_s, pv_s, *, tt):
    i = pl.program_id(1)
    C = GDN_CHUNK
    dk = C_HEAD_DIM
    H = C_HEADS

    @pl.when(i == 0)
    def _():
        state_s[...] = jnp.zeros_like(state_s)

    def conv(x_ref, halo_ref, pad_s, col0):
        pad_s[0:8, :] = jnp.where(i > 0, halo_ref[...], 0.0)
        pad_s[8:8 + tt, :] = x_ref[...]
        w = cw_ref[:, col0:col0 + C_QK]
        y = x_ref[...] * w[3:4]
        for kback in range(1, CONV_WIDTH):
            y = y + pad_s[8 - kback:8 - kback + tt, :] * w[3 - kback:4 - kback]
        return _silu(y)

    qf = conv(xq_ref, hq_ref, pq_s, 0)
    kf = conv(xk_ref, hk_ref, pk_s, C_QK)
    v_s[...] = conv(xv_ref, hv_ref, pv_s, 2 * C_QK)
    for h in range(H):
        cols = slice(dk * h, dk * (h + 1))
        qh = qf[:, cols]
        kh = kf[:, cols]
        q_s[:, cols] = qh * lax.rsqrt(jnp.sum(qh * qh, axis=-1, keepdims=True) + NORM_EPS) * (dk ** -0.5)
        k_s[:, cols] = kh * lax.rsqrt(jnp.sum(kh * kh, axis=-1, keepdims=True) + NORM_EPS)
    small = small_ref[...]
    g_s[...] = -jnp.exp(alog_row_ref[...]) * _softplus(small + dtb_row_ref[...])
    b_s[...] = jax.nn.sigmoid(small)

    r_i = _iota((C, C), 0)
    c_i = _iota((C, C), 1)
    lower = r_i >= c_i
    strict = r_i > c_i
    tri_l = lower.astype(F32)
    tri_u = (r_i <= c_i).astype(F32)
    neg_alog_col = -jnp.exp(alog_col_ref[...])
    dtb_col = dtb_col_ref[...]

    HC = H * C
    GC = GDN_GROUP * C
    rr = _iota((GC, GC), 0)
    cc = _iota((GC, GC), 1)
    same_head = (rr >> 6) == (cc >> 6)
    lower_bd = same_head & (rr >= cc)
    strict_bd = same_head & (rr > cc)
    state_bd = (_iota((GC, GDN_GROUP * dk), 0) >> 6) == (_iota((GC, GDN_GROUP * dk), 1) >> 7)

    def problem(c, grp):
        rows = pl.ds(pl.multiple_of(c * C, C), C)
        gc_col = _dot(tri_l, g_s[rows, :], HIGHEST)
        g_row = neg_alog_col * _softplus(arow_ref[c] + dtb_col)
        gc_row = _dot(g_row, tri_u, HIGHEST)
        beta_all = b_s[rows, :]
        g_last = [gc_col[C - 1:C, SMALL_A + h:SMALL_A + h + 1] for h in range(H)]
        if grp == 0:
            gl_s[c, 0:1, :] = jnp.concatenate([jnp.broadcast_to(jnp.exp(g), (1, dk)) for g in g_last], axis=1)
        heads = range(GDN_GROUP * grp, GDN_GROUP * (grp + 1))
        stack = lambda ref: jnp.concatenate([ref[rows, dk * h:dk * (h + 1)] for h in heads], axis=0)
        gcol = jnp.concatenate([gc_col[:, SMALL_A + h:SMALL_A + h + 1] for h in heads], axis=0)
        grow = jnp.concatenate([gc_row[h:h + 1, :] for h in heads], axis=1)
        beta = jnp.concatenate([beta_all[:, SMALL_B + h:SMALL_B + h + 1] for h in heads], axis=0)
        glast = jnp.concatenate([jnp.broadcast_to(g_last[h], (C, 1)) for h in heads], axis=0)
        k = stack(k_s)
        return dict(c=c, grp=grp, q=stack(q_s), k=k, v=stack(v_s), kb=k * beta, beta=beta, eg=jnp.exp(gcol),
                    kd=k * jnp.exp(glast - gcol),
                    decay=jnp.where(lower_bd, jnp.exp(jnp.where(lower_bd, gcol - grow, 0.0)), 0.0))

    def prepare_group(cg, carry):
        ps = [problem(PREP_GROUP * cg + r, grp) for r in range(PREP_GROUP) for grp in range(H // GDN_GROUP)]
        for p in ps:
            p["gram"] = _dot_nt(jnp.concatenate([p["kb"], p["q"]], axis=0).astype(BF16), p["k"].astype(BF16))
        for p in ps:
            p["qk"] = jnp.where(lower_bd, p["gram"][GC:2 * GC] * p["decay"], 0.0)
            p["x"] = -jnp.where(strict_bd, p["gram"][0:GC] * p["decay"], 0.0)
            p["pw16"] = p["x"].astype(BF16)
        for p in ps:
            p["pw"] = _dot(p["pw16"], p["pw16"])
        for _ in range(4):
            for p in ps:
                pw16 = p["pw"].astype(BF16)
                p["x"], p["pw"] = p["x"] + p["pw"] + _dot(pw16, p["x"].astype(BF16)), _dot(pw16, pw16)
        for p in ps:
            p["x"] = p["x"] + p["pw"] + _dot(p["pw"].astype(BF16), p["x"].astype(BF16))
        for p in ps:
            rhs = jnp.concatenate([p["v"] * p["beta"], p["kb"] * p["eg"]], axis=1)
            p["y"] = rhs + _dot(p["x"].astype(BF16), rhs.astype(BF16))
        for p in ps:
            c, grp, y = p["c"], p["grp"], p["y"]
            grows = slice(GC * grp, GC * (grp + 1))
            u_s[c, grows, :] = y[:, 0:dk]
            wq_s[c, grp, 0:GC, :] = y[:, dk:2 * dk].astype(BF16)
            wq_s[c, grp, GC:2 * GC, :] = (p["q"] * p["eg"]).astype(BF16)
            qk_s[c, grows, :] = p["qk"].astype(BF16)
            kdt_s[c, :, grows] = p["kd"].T.astype(BF16)
        return carry

    lax.fori_loop(0, tt // (PREP_GROUP * C), prepare_group, 0)

    def head_blocks(x, row0):
        return jnp.concatenate(
            [x[row0 + C * r:row0 + C * (r + 1), dk * r:dk * (r + 1)] for r in range(GDN_GROUP)], axis=0)

    def scan(c, carry):
        rows = pl.ds(pl.multiple_of(c * C, C), C)
        groups = range(H // GDN_GROUP)
        lanes = [slice(GDN_GROUP * dk * g, GDN_GROUP * dk * (g + 1)) for g in groups]
        grows = [slice(GC * g, GC * (g + 1)) for g in groups]
        st = [state_s[:, lanes[g]] for g in groups]
        d1 = [_dot(wq_s[c, g], st[g].astype(BF16)) for g in groups]
        v16 = [(u_s[c, grows[g], :] - head_blocks(d1[g], 0)).astype(BF16) for g in groups]
        o = [head_blocks(d1[g], GC) + _dot(qk_s[c, grows[g], :], v16[g]) for g in groups]
        for g in groups:
            v_bd = jnp.where(state_bd, jnp.concatenate([v16[g]] * GDN_GROUP, axis=1), jnp.zeros((), BF16))
            state_s[:, lanes[g]] = st[g] * gl_s[c, 0:1, lanes[g]] + _dot(kdt_s[c, :, grows[g]], v_bd)
            for r in range(GDN_GROUP):
                h = GDN_GROUP * g + r
                o_s[rows, dk * h:dk * (h + 1)] = o[g][C * r:C * (r + 1)]
        return carry

    lax.fori_loop(0, tt // C, scan, 0)

    nw = nw_ref[...]
    z = z_ref[...]
    for h in range(H):
        cols = slice(dk * h, dk * (h + 1))
        oh = o_s[:, cols]
        y = oh * lax.rsqrt(jnp.mean(oh * oh, axis=-1, keepdims=True) + NORM_EPS) * nw
        o_ref[:, cols] = (y * _silu(z[:, cols])).astype(BF16)


def _gated_delta_net(gdn, small, a_rows, conv_w, alog_row, dtb_row, alog_col, dtb_col, norm_w, tt=512):
    B, S, _ = gdn.shape
    nch = tt // GDN_CHUNK
    hc = C_HEADS * GDN_CHUNK
    part = lambda j: pl.BlockSpec((None, tt, C_QK), lambda b, i: (b, i, j))
    halo = lambda j: pl.BlockSpec((None, 8, C_QK), lambda b, i: (b, jnp.maximum(i * (tt // 8) - 1, 0), j))
    full = lambda a: pl.BlockSpec(a.shape, lambda b, i: (0,) * a.ndim)
    return pl.pallas_call(
        functools.partial(_gdn_kernel, tt=tt),
        grid=(B, S // tt),
        in_specs=[part(0), part(1), part(2), halo(0), halo(1), halo(2), part(3),
                  pl.BlockSpec((None, tt, W_SMALL), lambda b, i: (b, i, 0)),
                  pl.BlockSpec((None, nch, 8, GDN_CHUNK), lambda b, i: (b, i, 0, 0)),
                  full(conv_w), full(alog_row), full(dtb_row), full(alog_col), full(dtb_col), full(norm_w)],
        out_specs=pl.BlockSpec((None, tt, C_QK), lambda b, i: (b, i, 0)),
        out_shape=jax.ShapeDtypeStruct((B, S, C_QK), BF16),
        scratch_shapes=[pltpu.VMEM((tt, C_QK), F32), pltpu.VMEM((tt, C_QK), F32), pltpu.VMEM((tt, C_QK), F32),
                        pltpu.VMEM((tt, W_SMALL), F32), pltpu.VMEM((tt, W_SMALL), F32),
                        pltpu.VMEM((tt, C_QK), F32), pltpu.VMEM((C_HEAD_DIM, C_QK), F32),
                        pltpu.VMEM((nch, hc, C_HEAD_DIM), F32), pltpu.VMEM((nch, C_HEADS // GDN_GROUP, 2 * GDN_GROUP * GDN_CHUNK, C_HEAD_DIM), BF16),
                        pltpu.VMEM((nch, hc, GDN_GROUP * GDN_CHUNK), BF16), pltpu.VMEM((nch, C_HEAD_DIM, hc), BF16),
                        pltpu.VMEM((nch, 8, C_QK), F32)] + [pltpu.VMEM((tt + 8, C_QK), F32)] * 3,
        compiler_params=_cparams("arbitrary", "arbitrary"),
        name="gated_delta_net",
    )(gdn, gdn, gdn, gdn, gdn, gdn, gdn, small, a_rows, conv_w, alog_row, dtb_row, alog_col, dtb_col, norm_w)


def _outproj_kernel(x_ref, oa_ref, ob_ref, oc_ref, w_ref, gt_ref, o_ref):
    y = _dot(oa_ref[...], w_ref[0:A_Q, :])
    y = y + _dot(ob_ref[...], w_ref[A_Q:A_Q + B_Q, :])
    y = y + _dot(oc_ref[...], w_ref[A_Q + B_Q:, :])
    o_ref[...] = x_ref[...] + gt_ref[...] * y


def _out_projection(x, o_a, o_b, o_c, w, mod, tm=1024):
    B, S, D = x.shape
    row = lambda b, i: (b, i, 0)
    return pl.pallas_call(
        _outproj_kernel,
        grid=(B, S // tm),
        in_specs=[
            pl.BlockSpec((None, tm, D), row),
            pl.BlockSpec((None, tm, A_Q), row),
            pl.BlockSpec((None, tm, B_Q), row),
            pl.BlockSpec((None, tm, C_QK), row),
            pl.BlockSpec((D, D), lambda b, i: (0, 0)),
            pl.BlockSpec((None, 1, D), lambda b, i: (b, 0, 2)),
        ],
        out_specs=pl.BlockSpec((None, tm, D), row),
        out_shape=jax.ShapeDtypeStruct((B, S, D), F32),
        compiler_params=_cparams("arbitrary", "arbitrary"),
        name="out_projection",
    )(x, o_a, o_b, o_c, w, mod)


def _ffn_kernel(layer_ref, x_ref, g_ref, sc_ref, sh_ref, gt_ref, wgu_ref, wd_ref, o_ref, *, tf):
    del layer_ref
    x = x_ref[...]
    h = _rms_mod(x, g_ref[...], sc_ref[...], sh_ref[...]).astype(BF16)
    acc = None
    for f in range(D_FF // tf):
        gate = _dot(h, wgu_ref[:, tf * f:tf * (f + 1)])
        up = _dot(h, wgu_ref[:, D_FF + tf * f:D_FF + tf * (f + 1)])
        part = _dot((_silu(gate) * up).astype(BF16), wd_ref[tf * f:tf * (f + 1), :])
        acc = part if acc is None else acc + part
    o_ref[...] = x + gt_ref[...] * acc


def _ffn(x, gain, mod, w_gate_up_all, w_down_all, layer, tm=512, tf=1408):
    B, S, D = x.shape
    row = lambda b, i, l: (b, i, 0)
    resident = lambda a: pl.BlockSpec((None,) + a.shape[1:], lambda b, i, l: (l[0], 0, 0), pipeline_mode=pl.Buffered(1))
    return pl.pallas_call(
        functools.partial(_ffn_kernel, tf=tf),
        grid_spec=pltpu.PrefetchScalarGridSpec(
            num_scalar_prefetch=1,
            grid=(B, S // tm),
            in_specs=[
                pl.BlockSpec((None, tm, D), row),
                pl.BlockSpec((1, D), lambda b, i, l: (0, 0)),
                pl.BlockSpec((None, 1, D), lambda b, i, l: (b, 0, 4)),
                pl.BlockSpec((None, 1, D), lambda b, i, l: (b, 0, 3)),
                pl.BlockSpec((None, 1, D), lambda b, i, l: (b, 0, 5)),
                resident(w_gate_up_all),
                resident(w_down_all),
            ],
            out_specs=pl.BlockSpec((None, tm, D), row),
        ),
        out_shape=jax.ShapeDtypeStruct((B, S, D), F32),
        compiler_params=_cparams("arbitrary", "arbitrary"),
        name="swiglu_ffn",
    )(layer, x, gain, mod, mod, mod, w_gate_up_all, w_down_all)


def _final_norm_kernel(x_ref, g_ref, o_ref):
    x = x_ref[...]
    o_ref[...] = x * lax.rsqrt(jnp.mean(x * x, axis=-1, keepdims=True) + NORM_EPS) * g_ref[...]


def _final_norm(x, gain, tm=512):
    B, S, D = x.shape
    return pl.pallas_call(
        _final_norm_kernel,
        grid=(B, S // tm),
        in_specs=[pl.BlockSpec((None, tm, D), lambda b, i: (b, i, 0)), pl.BlockSpec((1, D), lambda b, i: (0, 0))],
        out_specs=pl.BlockSpec((None, tm, D), lambda b, i: (b, i, 0)),
        out_shape=jax.ShapeDtypeStruct((B, S, D), F32),
        compiler_params=_cparams("arbitrary", "arbitrary"),
        name="final_norm",
    )(x, gain)


def _rope_tables(seq):
    inv = 1.0 / (ROPE_THETA ** (jnp.arange(0, HEAD_DIM, 2, dtype=F32) / HEAD_DIM))
    ang = jnp.arange(seq, dtype=F32)[:, None] * inv[None, :]
    cos, sin = jnp.cos(ang), jnp.sin(ang)
    zero = jnp.zeros_like(sin)
    cos_t = jnp.tile(cos, (1, 4))
    sa_t = jnp.tile(jnp.concatenate([-sin, zero], axis=1), (1, 2))
    sb_t = jnp.tile(jnp.concatenate([zero, sin], axis=1), (1, 2))
    return cos_t, sa_t, sb_t


def _permute_w_in(w_in):
    pts = np.cumsum(IN_SPLITS)[:-1].tolist()
    aq, ak, av, bq, bkc, bvc, bks, bvs, bkw, bvw, bg, cqkv, cz, ca, cb = jnp.split(w_in, pts, axis=-1)
    pad = jnp.zeros(w_in.shape[:-1] + (W_SMALL - B_GATES - 2 * C_HEADS,), w_in.dtype)
    return jnp.concatenate([aq, bq, ak, bks, bkw, av, bvs, bvw, bkc, bvc, bg, ca, cb, pad, cqkv, cz], axis=-1).astype(BF16)


def _compress_weights(k_w1, k_w2, v_w1, v_w2, pe_k, pe_v):
    L = k_w1.shape[0]
    half = CMP_STRIDE * HEAD_DIM

    def first_layer(lo):
        wk = k_w1[:, lo:lo + half].reshape(L, CMP_STRIDE, HEAD_DIM, CMP_HIDDEN)
        wv = v_w1[:, lo:lo + half].reshape(L, CMP_STRIDE, HEAD_DIM, CMP_HIDDEN)
        z = jnp.zeros_like(wk)
        w = jnp.concatenate([jnp.concatenate([wk, z], axis=-1), jnp.concatenate([z, wv], axis=-1)], axis=2)
        return w.reshape(L, 2 * half, 2 * CMP_HIDDEN).astype(BF16)

    def pe_rows(lo):
        return jnp.concatenate([pe_k[:, lo:lo + CMP_STRIDE], pe_v[:, lo:lo + CMP_STRIDE]], axis=-1).reshape(L, 1, 2 * half)

    z2 = jnp.zeros_like(k_w2)
    w2 = jnp.concatenate([jnp.concatenate([k_w2, z2], axis=-1), jnp.concatenate([z2, v_w2], axis=-1)], axis=1).astype(BF16)
    return pe_rows(0), pe_rows(CMP_STRIDE), first_layer(0), first_layer(half), w2


def _pool_matrix(n_cmp_rows, n_slc):
    ratio = SLC_LEN // CMP_STRIDE
    n = np.arange(n_cmp_rows)[:, None]
    j = np.arange(n_slc)[None, :]
    return jnp.asarray(((n // ratio == j) | (n == ratio * j - 1)).astype(np.float32))


def _lane_row(v, lane0):
    L, n = v.shape
    return jnp.zeros((L, 1, 128), F32).at[:, 0, lane0:lane0 + n].set(v)


def _sublane_col(v):
    L, n = v.shape
    return jnp.zeros((L, 8, 1), F32).at[:, :n, 0].set(v)


def kernel(x, c, norm_mix, norm_ffn, ada_w, ada_b, w_in, attn_sinks, cmp_k_w1, cmp_k_w2, cmp_v_w1, cmp_v_w2, cmp_pe_k, cmp_pe_v, gdn_conv_w, gdn_A_log, gdn_dt_bias, gdn_norm, w_out, w_gate_up, w_down, final_norm):
    B, S, D = x.shape
    L = w_in.shape[0]
    n16 = S // CMP_STRIDE
    n_slc = S // SLC_LEN

    cos_t, sa_t, sb_t = _rope_tables(S)
    mod = _adaln_mod(c, ada_w, ada_b)
    pea, peb, w1a, w1b, w2c = _compress_weights(cmp_k_w1, cmp_k_w2, cmp_v_w1, cmp_v_w2, cmp_pe_k, cmp_pe_v)
    pool_t = _pool_matrix(n16, n_slc).T
    w_gate_up_all = w_gate_up.astype(BF16)
    w_down_all = w_down.astype(BF16)
    per_layer = dict(
        mod=mod,
        norm_mix=norm_mix.reshape(L, 1, D), norm_ffn=norm_ffn.reshape(L, 1, D),
        w_in=_permute_w_in(w_in), sinks=attn_sinks,
        pea=pea, peb=peb, w1a=w1a, w1b=w1b, w2c=w2c,
        conv_w=gdn_conv_w,
        alog_row=_lane_row(gdn_A_log, SMALL_A), dtb_row=_lane_row(gdn_dt_bias, SMALL_A),
        alog_col=_sublane_col(gdn_A_log), dtb_col=_sublane_col(gdn_dt_bias),
        gdn_norm=gdn_norm.reshape(L, 1, C_HEAD_DIM),
        w_out=w_out.astype(BF16), layer=jnp.arange(L, dtype=jnp.int32).reshape(L, 1),
    )

    def layer(xc, p):
        rope, bq, vals, cmp_in, small, gdn = _in_projection(xc, p["norm_mix"], p["mod"], p["w_in"], cos_t, sa_t, sb_t)
        vals_all_t = vals.reshape(B, S // KV_BLOCK, KV_BLOCK, W_V).transpose(0, 1, 3, 2)
        o_a = _swa_attention(rope, vals_all_t, p["sinks"])
        kv_cmp, kv_cmp_t = _compress(cmp_in.reshape(B, n16, CMP_STRIDE * W_CMP), p["pea"], p["peb"], p["w1a"], p["w1b"], p["w2c"])
        o_cmp_t, sel = _cmp_select(bq, kv_cmp, kv_cmp_t, pool_t)
        o_b = _nsa_attention(rope, vals_all_t, sel, o_cmp_t, small)
        a_rows = small[:, :, SMALL_A:SMALL_A + 8].reshape(B, S // GDN_CHUNK, GDN_CHUNK, 8).transpose(0, 1, 3, 2)
        o_c = _gated_delta_net(gdn, small, a_rows, p["conv_w"], p["alog_row"], p["dtb_row"], p["alog_col"], p["dtb_col"], p["gdn_norm"])
        xc = _out_projection(xc, o_a, o_b, o_c, p["w_out"], p["mod"])
        xc = _ffn(xc, p["norm_ffn"], p["mod"], w_gate_up_all, w_down_all, p["layer"])
        return xc, None

    x, _ = lax.scan(layer, x, per_layer)
    return _final_norm(x, final_norm.reshape(1, D))
```
